```python
import math
import jax, jax.numpy as jnp
from jax import lax
import numpy as np

D_MODEL = 4096
BATCH = 2
SEQ = 4096
DEPTH = 2

CHUNK = 64
Q_BLOCK = 128
HEAD_DIM = 128
MIX_WIDTH = D_MODEL
H_SB = (MIX_WIDTH // 2) // HEAD_DIM
H_DIFF = (MIX_WIDTH // 2) // (2 * HEAD_DIM)
H_CH = (MIX_WIDTH // 2) // HEAD_DIM
H_MLA = (MIX_WIDTH // 2) // HEAD_DIM
W_SB = H_SB * HEAD_DIM
W_DIFF = H_DIFF * 2 * HEAD_DIM
W_CH = H_CH * HEAD_DIM
LEFT_CHUNKS = 8
BAND = (LEFT_CHUNKS + 1) * CHUNK
REL_CLIP = 128
Q_LORA = D_MODEL // 4
KV_LORA = D_MODEL // 8
QK_NOPE = 128
QK_ROPE = 64
V_MLA = 128
ROPE_THETA = 10000.0
D_FF = 256 * ((8 * D_MODEL // 3 + 255) // 256)
CONV_W = 3
EPS = 1e-6
NEG = -1e30
N_EVEN = (DEPTH + 1) // 2
N_ODD = DEPTH // 2
EVEN_SIZES = [W_SB, W_SB, W_SB, W_DIFF, W_DIFF, W_DIFF]
ODD_SIZES = [W_CH, W_CH, W_CH, Q_LORA, KV_LORA, QK_ROPE]
EVEN_IN = sum(EVEN_SIZES)
ODD_IN = sum(ODD_SIZES)

kernel_name = "hybrid_stickbreak_diff_chunkrel_mla_convffn"


def rmsnorm(x, g):
    xf = x.astype(jnp.float32)
    y = xf * lax.rsqrt(jnp.mean(xf * xf, axis=-1, keepdims=True) + EPS)
    return (y * g.astype(jnp.float32)).astype(x.dtype)


def split_cols(p, sizes):
    idx = np.cumsum(sizes)[:-1].tolist()
    return jnp.split(p, idx, axis=-1)


def to_blocks(a, size):
    b, s = a.shape[:2]
    return a.reshape(b, s // size, size, *a.shape[2:]).swapaxes(0, 1)


def from_blocks(o):
    nb, b, size = o.shape[:3]
    return o.swapaxes(0, 1).reshape(b, nb * size, *o.shape[3:])


def chunk_causal_mask(qpos, kpos):
    return (kpos // CHUNK)[None, :] <= (qpos // CHUNK)[:, None]


def stick_breaking_attention(q, k, v):
    s_len = q.shape[1]
    scale = 1.0 / math.sqrt(q.shape[-1])
    kpos = jnp.arange(s_len)

    def body(args):
        qb, i = args
        qpos = i * Q_BLOCK + jnp.arange(Q_BLOCK)
        z = jnp.einsum('bqhd,bkhd->bhqk', qb, k).astype(jnp.float32) * scale
        mask = kpos[None, :] < qpos[:, None]
        log_1mb = jnp.where(mask, jax.nn.log_sigmoid(-z), 0.0)
        between = lax.cumsum(log_1mb, axis=3, reverse=True) - log_1mb
        w = jnp.where(mask, jnp.exp(jax.nn.log_sigmoid(z) + between), 0.0)
        return jnp.einsum('bhqk,bkhd->bqhd', w.astype(v.dtype), v)

    out = lax.map(body, (to_blocks(q, Q_BLOCK), jnp.arange(s_len // Q_BLOCK)))
    return from_blocks(out)


def diff_attention(q1, q2, k1, k2, v, lam, lam_init, subln_g):
    s_len, n_heads = q1.shape[1], q1.shape[2]
    scale = 1.0 / math.sqrt(q1.shape[-1])
    kpos = jnp.arange(s_len)
    slopes = 2.0 ** (-8.0 * jnp.arange(1, n_heads + 1, dtype=jnp.float32) / n_heads)

    def body(args):
        qb1, qb2, i = args
        qpos = i * Q_BLOCK + jnp.arange(Q_BLOCK)
        mask = chunk_causal_mask(qpos, kpos)[None, None]
        dist = jnp.abs(qpos[:, None] - kpos[None, :]).astype(jnp.float32)
        alibi = -slopes[:, None, None] * dist
        s1 = jnp.einsum('bqhd,bkhd->bhqk', qb1, k1).astype(jnp.float32) * scale + alibi
        s2 = jnp.einsum('bqhd,bkhd->bhqk', qb2, k2).astype(jnp.float32) * scale + alibi
        p1 = jax.nn.softmax(jnp.where(mask, s1, NEG), axis=-1)
        p2 = jax.nn.softmax(jnp.where(mask, s2, NEG), axis=-1)
        p = p1 - lam * p2
        return jnp.einsum('bhqk,bkhd->bqhd', p.astype(v.dtype), v)

    out = from_blocks(lax.map(body, (to_blocks(q1, Q_BLOCK), to_blocks(q2, Q_BLOCK),
                                     jnp.arange(s_len // Q_BLOCK))))
    return rmsnorm(out, subln_g) * (1.0 - lam_init)


def chunked_relbias_attention(q, k, v, rel_bias):
    s_len = q.shape[1]
    n_chunks = s_len // CHUNK
    pad = LEFT_CHUNKS * CHUNK
    scale = 1.0 / math.sqrt(q.shape[-1])
    kp = jnp.pad(k, ((0, 0), (pad, 0), (0, 0), (0, 0)))
    vp = jnp.pad(v, ((0, 0), (pad, 0), (0, 0), (0, 0)))
    rel = jnp.arange(CHUNK)[:, None] + pad - jnp.arange(BAND)[None, :]
    bias = rel_bias[:, jnp.clip(rel, -REL_CLIP, REL_CLIP) + REL_CLIP].astype(jnp.float32)

    def body(args):
        qc, c = args
        kb = lax.dynamic_slice_in_dim(kp, c * CHUNK, BAND, axis=1)
        vb = lax.dynamic_slice_in_dim(vp, c * CHUNK, BAND, axis=1)
        kpos = (c - LEFT_CHUNKS) * CHUNK + jnp.arange(BAND)
        s = jnp.einsum('bqhd,bkhd->bhqk', qc, kb).astype(jnp.float32) * scale + bias
        s = jnp.where((kpos >= 0)[None, None, None, :], s, NEG)
        p = jax.nn.softmax(s, axis=-1)
        return jnp.einsum('bhqk,bkhd->bqhd', p.astype(vb.dtype), vb)

    out = lax.map(body, (to_blocks(q, CHUNK), jnp.arange(n_chunks)))
    return from_blocks(out)


def rope(x, cos, sin):
    half = x.shape[-1] // 2
    x1, x2 = x[..., :half], x[..., half:]
    return jnp.concatenate([x1 * cos - x2 * sin, x2 * cos + x1 * sin], axis=-1).astype(x.dtype)


def mla_attention(c_q, c_kv, k_rope_raw, q_norm_g, w_uq, kv_norm_g, w_ukv):
    b, s_len = c_q.shape[:2]
    q = (rmsnorm(c_q, q_norm_g) @ w_uq).reshape(b, s_len, H_MLA, QK_NOPE + QK_ROPE)
    q_nope, q_rope = q[..., :QK_NOPE], q[..., QK_NOPE:]
    kv = (rmsnorm(c_kv, kv_norm_g) @ w_ukv).reshape(b, s_len, H_MLA, QK_NOPE + V_MLA)
    k_nope, v = kv[..., :QK_NOPE], kv[..., QK_NOPE:]
    pos = jnp.arange(s_len, dtype=jnp.float32)
    inv_freq = ROPE_THETA ** (-jnp.arange(0, QK_ROPE, 2, dtype=jnp.float32) / QK_ROPE)
    ang = pos[:, None] * inv_freq[None, :]
    cos, sin = jnp.cos(ang), jnp.sin(ang)
    q_rope = rope(q_rope, cos[:, None, :], sin[:, None, :])
    k_rope = rope(k_rope_raw, cos, sin)
    scale = 1.0 / math.sqrt(QK_NOPE + QK_ROPE)
    kpos = jnp.arange(s_len)

    def body(args):
        qn, qr, i = args
        qpos = i * Q_BLOCK + jnp.arange(Q_BLOCK)
        s = (jnp.einsum('bqhd,bkhd->bhqk', qn, k_nope)
             + jnp.einsum('bqhr,bkr->bhqk', qr, k_rope)).astype(jnp.float32) * scale
        s = jnp.where(chunk_causal_mask(qpos, kpos)[None, None], s, NEG)
        p = jax.nn.softmax(s, axis=-1)
        return jnp.einsum('bhqk,bkhd->bqhd', p.astype(v.dtype), v)

    out = lax.map(body, (to_blocks(q_nope, Q_BLOCK), to_blocks(q_rope, Q_BLOCK),
                         jnp.arange(s_len // Q_BLOCK)))
    return from_blocks(out)


def conv_ffn(h, w_in, conv_w, conv_b, w_out):
    s_len = h.shape[1]
    u = h @ w_in
    up = jnp.pad(u, ((0, 0), (CONV_W - 1, 0), (0, 0)))
    y = conv_b + sum(conv_w[j] * up[:, j:j + s_len] for j in range(CONV_W))
    gate, val = jnp.split(y, 2, axis=-1)
    return (jax.nn.gelu(gate, approximate=True) * val) @ w_out


def even_mixer(h, w_in, w_out, diff_lambda, diff_subln_g, layer):
    b, s_len = h.shape[:2]
    sb_q, sb_k, sb_v, d_q, d_k, d_v = split_cols(h @ w_in, EVEN_SIZES)
    hd = (b, s_len, H_SB, HEAD_DIM)
    a = stick_breaking_attention(sb_q.reshape(hd), sb_k.reshape(hd), sb_v.reshape(hd))
    d_q = d_q.reshape(b, s_len, H_DIFF, 2, HEAD_DIM)
    d_k = d_k.reshape(b, s_len, H_DIFF, 2, HEAD_DIM)
    d_v = d_v.reshape(b, s_len, H_DIFF, 2 * HEAD_DIM)
    lam_init = 0.8 - 0.6 * math.exp(-0.3 * layer)
    lf = diff_lambda.astype(jnp.float32)
    lam = jnp.exp(jnp.sum(lf[0] * lf[1])) - jnp.exp(jnp.sum(lf[2] * lf[3])) + lam_init
    bo = diff_attention(d_q[:, :, :, 0], d_q[:, :, :, 1], d_k[:, :, :, 0], d_k[:, :, :, 1],
                        d_v, lam, lam_init, diff_subln_g)
    mix = jnp.concatenate([a.reshape(b, s_len, W_SB), bo.reshape(b, s_len, W_DIFF)], axis=-1)
    return mix @ w_out


def odd_mixer(h, w_in, w_out, rel_bias, q_norm_g, w_uq, kv_norm_g, w_ukv):
    b, s_len = h.shape[:2]
    c_q_, c_k_, c_v_, lat_q, lat_kv, k_rope = split_cols(h @ w_in, ODD_SIZES)
    hd = (b, s_len, H_CH, HEAD_DIM)
    c = chunked_relbias_attention(c_q_.reshape(hd), c_k_.reshape(hd), c_v_.reshape(hd), rel_bias)
    d = mla_attention(lat_q, lat_kv, k_rope, q_norm_g, w_uq, kv_norm_g, w_ukv)
    mix = jnp.concatenate([c.reshape(b, s_len, W_CH), d.reshape(b, s_len, H_MLA * V_MLA)], axis=-1)
    return mix @ w_out


def setup_inputs(seed: int = 0) -> dict:
    key = jax.random.key(seed)
    ks = jax.random.split(key, 20)
    f32 = jnp.float32

    def nrm(k, shape, scale):
        return jax.random.normal(k, shape, f32) * scale

    return {
        "x": nrm(ks[0], (BATCH, SEQ, D_MODEL), 1.0),
        "norm_g": 1.0 + nrm(ks[1], (DEPTH, 4, D_MODEL), 0.05),
        "even_w_in": nrm(ks[2], (N_EVEN, D_MODEL, EVEN_IN), D_MODEL ** -0.5),
        "even_w_out": nrm(ks[3], (N_EVEN, MIX_WIDTH, D_MODEL), MIX_WIDTH ** -0.5),
        "diff_lambda": nrm(ks[4], (N_EVEN, 4, HEAD_DIM), 0.1),
        "diff_subln_g": 1.0 + nrm(ks[5], (N_EVEN, 2 * HEAD_DIM), 0.05),
        "odd_w_in": nrm(ks[6], (N_ODD, D_MODEL, ODD_IN), D_MODEL ** -0.5),
        "odd_w_out": nrm(ks[7], (N_ODD, MIX_WIDTH, D_MODEL), MIX_WIDTH ** -0.5),
        "ch_rel_bias": nrm(ks[8], (N_ODD, H_CH, 2 * REL_CLIP + 1), 0.5),
        "mla_q_norm_g": 1.0 + nrm(ks[9], (N_ODD, Q_LORA), 0.05),
        "mla_w_uq": nrm(ks[10], (N_ODD, Q_LORA, H_MLA * (QK_NOPE + QK_ROPE)), Q_LORA ** -0.5),
        "mla_kv_norm_g": 1.0 + nrm(ks[11], (N_ODD, KV_LORA), 0.05),
        "mla_w_ukv": nrm(ks[12], (N_ODD, KV_LORA, H_MLA * (QK_NOPE + V_MLA)), KV_LORA ** -0.5),
        "ffn_w_in": nrm(ks[13], (DEPTH, D_MODEL, 2 * D_FF), D_MODEL ** -0.5),
        "ffn_conv_w": nrm(ks[14], (DEPTH, CONV_W, 2 * D_FF), CONV_W ** -0.5),
        "ffn_conv_b": nrm(ks[15], (DEPTH, 2 * D_FF), 0.01),
        "ffn_w_out": nrm(ks[16], (DEPTH, D_FF, D_MODEL), D_FF ** -0.5),
    }


def reference(x, norm_g, even_w_in, even_w_out, diff_lambda, diff_subln_g,
              odd_w_in, odd_w_out, ch_rel_bias, mla_q_norm_g, mla_w_uq,
              mla_kv_norm_g, mla_w_ukv, ffn_w_in, ffn_conv_w, ffn_conv_b, ffn_w_out):
    for layer in range(DEPTH):
        g = norm_g[layer]
        hn = rmsnorm(x, g[0])
        if layer % 2 == 0:
            i = layer // 2
            mix = even_mixer(hn, even_w_in[i], even_w_out[i], diff_lambda[i],
                             diff_subln_g[i], layer)
        else:
            i = layer // 2
            mix = odd_mixer(hn, odd_w_in[i], odd_w_out[i], ch_rel_bias[i], mla_q_norm_g[i],
                            mla_w_uq[i], mla_kv_norm_g[i], mla_w_ukv[i])
        x = x + rmsnorm(mix, g[1])
        f = conv_ffn(rmsnorm(x, g[2]), ffn_w_in[layer], ffn_conv_w[layer],
                     ffn_conv_b[layer], ffn_w_out[layer])
        x = x + rmsnorm(f, g[3])
    return x
```

```python
import functools
import math

import jax
import jax.numpy as jnp
from jax import lax
from jax.experimental import pallas as pl
from jax.experimental.pallas import tpu as pltpu

F32 = jnp.float32
BF16 = jnp.bfloat16

CHUNK = 64
HEAD_DIM = 128
LEFT_CHUNKS = 8
REL_CLIP = 128
QK_NOPE = 128
QK_ROPE = 64
V_MLA = 128
ROPE_THETA = 10000.0
CONV_W = 3
EPS = 1e-6
NEG = -1e30

LANES = 128
BF16_SUBLANES = 16
VMEM_LIMIT = 52 * 1024 * 1024

ATTN_TILE = 256


def _params(*sem):
    return pltpu.CompilerParams(dimension_semantics=sem, vmem_limit_bytes=VMEM_LIMIT)


def _rms(x, g):
    ms = jnp.mean(x * x, axis=-1, keepdims=True)
    return x * lax.rsqrt(ms + EPS) * g


def _dot(a, b):
    return jnp.dot(a, b, preferred_element_type=F32)


def _dot_nt(a, b):
    return lax.dot_general(a, b, (((1,), (1,)), ((), ())), preferred_element_type=F32)


def _norm_cast_kernel(x_ref, g_ref, h_ref):
    h_ref[...] = _rms(x_ref[...], g_ref[...]).astype(h_ref.dtype)


def _resid_norm_kernel(x_ref, y_ref, g1_ref, g2_ref, xo_ref, h_ref):
    xn = x_ref[...] + _rms(y_ref[...], g1_ref[...])
    xo_ref[...] = xn
    h_ref[...] = _rms(xn, g2_ref[...]).astype(h_ref.dtype)


def _resid_kernel(x_ref, y_ref, g_ref, xo_ref):
    xo_ref[...] = x_ref[...] + _rms(y_ref[...], g_ref[...])


def _row_spec(tr, d):
    return pl.BlockSpec((tr, d), lambda i: (i, 0))


def _vec_spec(d):
    return pl.BlockSpec((1, d), lambda i: (0, 0))


def norm_cast(x, g, tr=256):
    t, d = x.shape
    return pl.pallas_call(
        _norm_cast_kernel,
        grid=(t // tr,),
        in_specs=[_row_spec(tr, d), _vec_spec(d)],
        out_specs=_row_spec(tr, d),
        out_shape=jax.ShapeDtypeStruct((t, d), BF16),
        compiler_params=_params("parallel"),
        name="norm_cast",
    )(x, g.reshape(1, d))


def resid_norm(x, y, g1, g2, tr=256):
    t, d = x.shape
    return pl.pallas_call(
        _resid_norm_kernel,
        grid=(t // tr,),
        in_specs=[_row_spec(tr, d), _row_spec(tr, d), _vec_spec(d), _vec_spec(d)],
        out_specs=[_row_spec(tr, d), _row_spec(tr, d)],
        out_shape=[jax.ShapeDtypeStruct((t, d), F32), jax.ShapeDtypeStruct((t, d), BF16)],
        compiler_params=_params("parallel"),
        name="resid_norm",
    )(x, y, g1.reshape(1, d), g2.reshape(1, d))


def resid(x, y, g, tr=256):
    t, d = x.shape
    return pl.pallas_call(
        _resid_kernel,
        grid=(t // tr,),
        in_specs=[_row_spec(tr, d), _row_spec(tr, d), _vec_spec(d)],
        out_specs=_row_spec(tr, d),
        out_shape=jax.ShapeDtypeStruct((t, d), F32),
        compiler_params=_params("parallel"),
        name="resid",
    )(x, y, g.reshape(1, d))


def _matmul_kernel(*refs, n_pairs):
    o_ref = refs[2 * n_pairs]
    acc = _dot(refs[0][...], refs[n_pairs][...])
    for p in range(1, n_pairs):
        acc = acc + _dot(refs[p][...], refs[n_pairs + p][...])
    o_ref[...] = acc.astype(o_ref.dtype)


def matmul(pairs, n, out_dtype, tm, tn, name):
    m = pairs[0][0].shape[0]
    n_pairs = len(pairs)
    a_specs, w_specs, args_a, args_w = [], [], [], []
    for a, w, rb in pairs:
        k = a.shape[1]
        a_specs.append(pl.BlockSpec((tm, k), lambda i, j: (i, 0)))
        w_specs.append(pl.BlockSpec((k, tn), lambda i, j, rb=rb: (rb, j)))
        args_a.append(a)
        args_w.append(w)
    return pl.pallas_call(
        functools.partial(_matmul_kernel, n_pairs=n_pairs),
        grid=(m // tm, n // tn),
        in_specs=a_specs + w_specs,
        out_specs=pl.BlockSpec((tm, tn), lambda i, j: (i, j)),
        out_shape=jax.ShapeDtypeStruct((m, n), out_dtype),
        compiler_params=_params("parallel", "arbitrary"),
        name=name,
    )(*args_a, *args_w)


def _gelu_tanh(x):
    c = math.sqrt(2.0 / math.pi)
    return x * (0.5 * (1.0 + jnp.tanh(c * (x + 0.044715 * (x * x * x)))))


def _ffn_in_kernel(a_ref, ah_ref, wg_ref, wv_ref, cwg_ref, cwv_ref, cbg_ref, cbv_ref,
                   o_ref, *, tm, seq):
    i = pl.program_id(0)
    a = a_ref[...]
    ah = ah_ref[...]
    seq_start = (i * tm) % seq == 0
    row = lax.broadcasted_iota(jnp.int32, o_ref.shape, 0)

    def conv(w_ref, cw_ref, cb_ref):
        w = w_ref[...]
        u = _dot(a, w)
        uh = _dot(ah, w)
        uh = jnp.where(seq_start, 0.0, uh)
        p1 = uh[BF16_SUBLANES - 1:BF16_SUBLANES, :]
        p2 = uh[BF16_SUBLANES - 2:BF16_SUBLANES - 1, :]
        u1 = jnp.where(row == 0, p1, pltpu.roll(u, 1, 0))
        u2 = jnp.where(row == 0, p2, jnp.where(row == 1, p1, pltpu.roll(u, 2, 0)))
        cw = cw_ref[...]
        return cb_ref[...] + (cw[0:1, :] * u2 + cw[1:2, :] * u1 + cw[2:3, :] * u)

    gate = conv(wg_ref, cwg_ref, cbg_ref)
    val = conv(wv_ref, cwv_ref, cbv_ref)
    o_ref[...] = (_gelu_tanh(gate) * val).astype(o_ref.dtype)


def ffn_in(h, w_in, conv_w, conv_b, seq, tm=1024, tn=256):
    t, k = h.shape
    f = w_in.shape[1] // 2
    nf = f // tn
    halo = BF16_SUBLANES
    hb = tm // halo
    return pl.pallas_call(
        functools.partial(_ffn_in_kernel, tm=tm, seq=seq),
        grid=(t // tm, nf),
        in_specs=[
            pl.BlockSpec((tm, k), lambda i, j: (i, 0)),
            pl.BlockSpec((halo, k), lambda i, j: (jnp.maximum(i * hb - 1, 0), 0)),
            pl.BlockSpec((k, tn), lambda i, j: (0, j)),
            pl.BlockSpec((k, tn), lambda i, j: (0, j + nf)),
            pl.BlockSpec((CONV_W, tn), lambda i, j: (0, j)),
            pl.BlockSpec((CONV_W, tn), lambda i, j: (0, j + nf)),
            pl.BlockSpec((1, tn), lambda i, j: (0, j)),
            pl.BlockSpec((1, tn), lambda i, j: (0, j + nf)),
        ],
        out_specs=pl.BlockSpec((tm, tn), lambda i, j: (i, j)),
        out_shape=jax.ShapeDtypeStruct((t, f), BF16),
        compiler_params=_params("parallel", "arbitrary"),
        name="ffn_in",
    )(h, h, w_in, w_in, conv_w, conv_w, conv_b.reshape(1, 2 * f), conv_b.reshape(1, 2 * f))


def _sb_kernel(q_ref, k_ref, v_ref, o_ref, *, tq, scale):
    qi = pl.program_id(2)
    q = q_ref[...]
    d = q.shape[1]
    row = lax.broadcasted_iota(jnp.int32, (tq, tq), 0)
    col = lax.broadcasted_iota(jnp.int32, (tq, tq), 1)
    strict = col < row
    r2 = lax.broadcasted_iota(jnp.int32, (2 * tq, tq), 0)
    c2 = lax.broadcasted_iota(jnp.int32, (2 * tq, tq), 1)
    tri = (jnp.where(r2 >= tq, r2 - tq, r2) >= c2).astype(BF16)

    def block(kb, rsum, acc, diag):
        start = pl.multiple_of(kb * tq, tq)
        k = k_ref[pl.ds(start, tq), :]
        v = v_ref[pl.ds(start, tq), :]
        z = _dot_nt(q, k) * scale
        lg = -(jnp.maximum(z, 0.0) + jnp.log1p(jnp.exp(-jnp.abs(z))))
        if diag:
            lg = jnp.where(strict, lg, 0.0)
        hi = lg.astype(BF16)
        lo = (lg - hi.astype(F32)).astype(BF16)
        csum = _dot(jnp.concatenate([hi, lo], axis=1), tri)
        w = jnp.exp(z + csum + rsum)
        if diag:
            w = jnp.where(strict, w, 0.0)
        acc = acc + _dot(w.astype(BF16), v)
        rsum = rsum + jnp.sum(lg, axis=-1, keepdims=True)
        return rsum, acc

    rsum, acc = block(qi, jnp.zeros((tq, 1), F32), jnp.zeros((tq, d), F32), True)

    def body(it, carry):
        return block(qi - 1 - it, carry[0], carry[1], False)

    rsum, acc = lax.fori_loop(0, qi, body, (rsum, acc))
    o_ref[...] = acc.astype(o_ref.dtype)


def sb_attention(qkv, batch, seq, n_heads, q_col, k_col, v_col):
    tq = ATTN_TILE
    nq = seq // tq
    d = HEAD_DIM
    return pl.pallas_call(
        functools.partial(_sb_kernel, tq=tq, scale=1.0 / math.sqrt(d)),
        grid=(batch, n_heads, nq),
        in_specs=[
            pl.BlockSpec((tq, d), lambda b, h, i: (b * nq + i, q_col + h)),
            pl.BlockSpec((seq, d), lambda b, h, i: (b, k_col + h)),
            pl.BlockSpec((seq, d), lambda b, h, i: (b, v_col + h)),
        ],
        out_specs=pl.BlockSpec((tq, d), lambda b, h, i: (b * nq + i, h)),
        out_shape=jax.ShapeDtypeStruct((batch * seq, n_heads * d), BF16),
        compiler_params=_params("parallel", "parallel", "arbitrary"),
        name="sb_attention",
    )(qkv, qkv, qkv)


def _online_step(s, v, m, l, acc):
    m_new = jnp.maximum(m, jnp.max(s, axis=-1, keepdims=True))
    alpha = jnp.exp(m - m_new)
    p = jnp.exp(s - m_new)
    l = alpha * l + jnp.sum(p, axis=-1, keepdims=True)
    acc = alpha * acc + _dot(p.astype(BF16), v)
    return m_new, l, acc


def _chunk_causal(tq):
    row = lax.broadcasted_iota(jnp.int32, (tq, tq), 0)
    col = lax.broadcasted_iota(jnp.int32, (tq, tq), 1)
    return row, col, (col // CHUNK) <= (row // CHUNK)


def _diff_kernel(q1_ref, q2_ref, k1_ref, k2_ref, v_ref, lam_ref, g_ref, o_ref, *,
                 tq, scale, lam_init):
    h = pl.program_id(1)
    qi = pl.program_id(2)
    q1 = q1_ref[...]
    q2 = q2_ref[...]
    dv = v_ref.shape[1]
    row, col, visible = _chunk_causal(tq)
    dist = (row - col).astype(F32)
    slope = lax.bitcast_convert_type(
        jnp.full((1, 1), (126 - h) * (1 << 23), jnp.int32), F32)
    nbias_diag = -slope * jnp.abs(dist)
    nbias_off = -slope * dist

    def block(kb, carry, diag):
        m1, l1, a1, m2, l2, a2 = carry
        start = pl.multiple_of(kb * tq, tq)
        k1 = k1_ref[pl.ds(start, tq), :]
        k2 = k2_ref[pl.ds(start, tq), :]
        v = v_ref[pl.ds(start, tq), :]
        s1 = _dot_nt(q1, k1) * scale
        s2 = _dot_nt(q2, k2) * scale
        if diag:
            s1 = jnp.where(visible, s1 + nbias_diag, NEG)
            s2 = jnp.where(visible, s2 + nbias_diag, NEG)
        else:
            bias = nbias_off - slope * ((qi - kb) * tq).astype(F32)
            s1 = s1 + bias
            s2 = s2 + bias
        m1, l1, a1 = _online_step(s1, v, m1, l1, a1)
        m2, l2, a2 = _online_step(s2, v, m2, l2, a2)
        return m1, l1, a1, m2, l2, a2

    def body(kb, carry):
        return block(kb, carry, False)

    init = (jnp.full((tq, 1), NEG, F32), jnp.zeros((tq, 1), F32), jnp.zeros((tq, dv), F32)) * 2
    carry = lax.fori_loop(0, qi, body, init)
    m1, l1, a1, m2, l2, a2 = block(qi, carry, True)

    lf = lam_ref[...]
    lam = (jnp.exp(jnp.sum(lf[0:1, :] * lf[1:2, :], axis=-1, keepdims=True))
           - jnp.exp(jnp.sum(lf[2:3, :] * lf[3:4, :], axis=-1, keepdims=True)) + lam_init)
    o = a1 / l1 - lam * (a2 / l2)
    o_ref[...] = (_rms(o, g_ref[...]) * (1.0 - lam_init)).astype(o_ref.dtype)


def diff_attention(qkv, diff_lambda, subln_g, batch, seq, n_heads, q_col, k_col, v_col, lam_init):
    tq = ATTN_TILE
    nq = seq // tq
    d = HEAD_DIM
    dv = 2 * HEAD_DIM
    assert n_heads == 8, "ALiBi slopes are built as exact powers of two"
    return pl.pallas_call(
        functools.partial(_diff_kernel, tq=tq, scale=1.0 / math.sqrt(d), lam_init=lam_init),
        grid=(batch, n_heads, nq),
        in_specs=[
            pl.BlockSpec((tq, d), lambda b, h, i: (b * nq + i, q_col + 2 * h)),
            pl.BlockSpec((tq, d), lambda b, h, i: (b * nq + i, q_col + 2 * h + 1)),
            pl.BlockSpec((seq, d), lambda b, h, i: (b, k_col + 2 * h)),
            pl.BlockSpec((seq, d), lambda b, h, i: (b, k_col + 2 * h + 1)),
            pl.BlockSpec((seq, dv), lambda b, h, i: (b, v_col + h)),
            pl.BlockSpec((4, d), lambda b, h, i: (0, 0)),
            pl.BlockSpec((1, dv), lambda b, h, i: (0, 0)),
        ],
        out_specs=pl.BlockSpec((tq, dv), lambda b, h, i: (b * nq + i, h)),
        out_shape=jax.ShapeDtypeStruct((batch * seq, n_heads * dv), BF16),
        compiler_params=_params("parallel", "parallel", "arbitrary"),
        name="diff_attention",
    )(qkv, qkv, qkv, qkv, qkv, diff_lambda, subln_g.reshape(1, dv))


def _chunk_kernel(q_ref, k_ref, v_ref, tab_ref, o_ref, *, tq, scale, n_win):
    qi = pl.program_id(2)
    q = q_ref[...]
    s_blocks, v_blocks = [], []
    for w in range(n_win):
        kb = qi - (n_win - 1) + w
        start = pl.multiple_of(jnp.maximum(kb, 0) * tq, tq)
        k = k_ref[pl.ds(start, tq), :]
        v_blocks.append(v_ref[pl.ds(start, tq), :])
        s = _dot_nt(q, k) * scale + tab_ref[0, :, w * tq:(w + 1) * tq]
        if w < n_win - 1:
            s = jnp.where(kb >= 0, s, NEG)
        s_blocks.append(s)
    m = s_blocks[0].max(axis=-1, keepdims=True)
    for s in s_blocks[1:]:
        m = jnp.maximum(m, s.max(axis=-1, keepdims=True))
    l = jnp.zeros_like(m)
    acc = jnp.zeros(o_ref.shape, F32)
    for s, v in zip(s_blocks, v_blocks):
        p = jnp.exp(s - m)
        l = l + jnp.sum(p, axis=-1, keepdims=True)
        acc = acc + _dot(p.astype(BF16), v)
    o_ref[...] = (acc / l).astype(o_ref.dtype)


def chunk_bias_table(rel_bias, tq):
    left = LEFT_CHUNKS * CHUNK
    i = jnp.arange(tq)[:, None]
    j = jnp.arange(tq + left)[None, :]
    rel = i + left - j
    qc, kc = i // CHUNK, j // CHUNK
    allowed = (kc >= qc) & (kc <= qc + LEFT_CHUNKS)
    bias = rel_bias[:, jnp.clip(rel, -REL_CLIP, REL_CLIP) + REL_CLIP].astype(F32)
    return jnp.where(allowed[None], bias, NEG)


def chunk_attention(qkv, rel_bias, batch, seq, n_heads, q_col, k_col, v_col):
    tq = ATTN_TILE
    nq = seq // tq
    d = HEAD_DIM
    left = LEFT_CHUNKS * CHUNK
    assert left % tq == 0
    n_win = left // tq + 1
    table = chunk_bias_table(rel_bias, tq)
    return pl.pallas_call(
        functools.partial(_chunk_kernel, tq=tq, scale=1.0 / math.sqrt(d), n_win=n_win),
        grid=(n_heads, batch, nq),
        in_specs=[
            pl.BlockSpec((tq, d), lambda h, b, i: (b * nq + i, q_col + h)),
            pl.BlockSpec((seq, d), lambda h, b, i: (b, k_col + h)),
            pl.BlockSpec((seq, d), lambda h, b, i: (b, v_col + h)),
            pl.BlockSpec((1, tq, tq + left), lambda h, b, i: (h, 0, 0)),
        ],
        out_specs=pl.BlockSpec((tq, d), lambda h, b, i: (b * nq + i, h)),
        out_shape=jax.ShapeDtypeStruct((batch * seq, n_heads * d), BF16),
        compiler_params=_params("parallel", "parallel", "arbitrary"),
        name="chunk_attention",
    )(qkv, qkv, qkv, table)


def rope_tables(seq):
    half = QK_ROPE // 2
    pos = jnp.arange(seq, dtype=F32)
    inv_freq = ROPE_THETA ** (-jnp.arange(0, QK_ROPE, 2, dtype=F32) / QK_ROPE)
    ang = pos[:, None] * inv_freq[None, :]
    cos, sin = jnp.cos(ang), jnp.sin(ang)
    z = lambda n: jnp.zeros((seq, n), F32)
    cos_t = jnp.concatenate([cos, cos, z(LANES - 2 * half)], axis=1)
    sin_a = jnp.concatenate([-sin, z(LANES - half)], axis=1)
    sin_b = jnp.concatenate([z(half), sin, z(LANES - 2 * half)], axis=1)
    return cos_t, sin_a, sin_b


def _rope(x, cos_t, sin_a, sin_b):
    half = QK_ROPE // 2
    return x * cos_t + pltpu.roll(x, LANES - half, 1) * sin_a + pltpu.roll(x, half, 1) * sin_b


def _mla_q_kernel(c_ref, g_ref, w_ref, cos_ref, sa_ref, sb_ref, o_ref, an_ref):
    @pl.when(pl.program_id(1) == 0)
    def _():
        an_ref[...] = _rms(c_ref[...], g_ref[...]).astype(an_ref.dtype)

    res = _dot(an_ref[...], w_ref[...])
    o_ref[:, :QK_NOPE] = res[:, :QK_NOPE].astype(o_ref.dtype)
    o_ref[:, QK_NOPE:] = _rope(res[:, QK_NOPE:], cos_ref[...], sa_ref[...], sb_ref[...]).astype(o_ref.dtype)


def mla_q(lat, g, w_q, tables, seq, n_heads, tm=512):
    t = lat.shape[0]
    kq = g.shape[0]
    hw = QK_NOPE + LANES
    npos = seq // tm
    tab_spec = pl.BlockSpec((tm, LANES), lambda i, j: (i % npos, 0))
    return pl.pallas_call(
        _mla_q_kernel,
        grid=(t // tm, n_heads),
        in_specs=[
            pl.BlockSpec((tm, kq), lambda i, j: (i, 0)),
            pl.BlockSpec((1, kq), lambda i, j: (0, 0)),
            pl.BlockSpec((kq, hw), lambda i, j: (0, j)),
            tab_spec, tab_spec, tab_spec,
        ],
        out_specs=pl.BlockSpec((tm, hw), lambda i, j: (i, j)),
        out_shape=jax.ShapeDtypeStruct((t, n_heads * hw), BF16),
        scratch_shapes=[pltpu.VMEM((tm, kq), BF16)],
        compiler_params=_params("parallel", "arbitrary"),
        name="mla_q",
    )(lat, g.reshape(1, kq), w_q, *tables)


def _mla_kv_kernel(c_ref, kr_ref, g_ref, w_ref, cos_ref, sa_ref, sb_ref, k_ref, v_ref,
                   an_ref, kr_scr):
    @pl.when(pl.program_id(1) == 0)
    def _():
        an_ref[...] = _rms(c_ref[...], g_ref[...]).astype(an_ref.dtype)
        kr_scr[...] = _rope(kr_ref[...], cos_ref[...], sa_ref[...], sb_ref[...]).astype(kr_scr.dtype)

    res = _dot(an_ref[...], w_ref[...])
    k_ref[:, :QK_NOPE] = res[:, :QK_NOPE].astype(k_ref.dtype)
    k_ref[:, QK_NOPE:] = kr_scr[...]
    v_ref[...] = res[:, QK_NOPE:].astype(v_ref.dtype)


def mla_kv(lat, g, w_kv, tables, seq, n_heads, ckv_col, kr_col, tm=512):
    t = lat.shape[0]
    kkv = g.shape[0]
    hw = QK_NOPE + LANES
    npos = seq // tm
    tab_spec = pl.BlockSpec((tm, LANES), lambda i, j: (i % npos, 0))
    return pl.pallas_call(
        _mla_kv_kernel,
        grid=(t // tm, n_heads),
        in_specs=[
            pl.BlockSpec((tm, kkv), lambda i, j: (i, ckv_col)),
            pl.BlockSpec((tm, LANES), lambda i, j: (i, kr_col)),
            pl.BlockSpec((1, kkv), lambda i, j: (0, 0)),
            pl.BlockSpec((kkv, QK_NOPE + V_MLA), lambda i, j: (0, j)),
            tab_spec, tab_spec, tab_spec,
        ],
        out_specs=[pl.BlockSpec((tm, hw), lambda i, j: (i, j)),
                   pl.BlockSpec((tm, V_MLA), lambda i, j: (i, j))],
        out_shape=[jax.ShapeDtypeStruct((t, n_heads * hw), BF16),
                   jax.ShapeDtypeStruct((t, n_heads * V_MLA), BF16)],
        scratch_shapes=[pltpu.VMEM((tm, kkv), BF16), pltpu.VMEM((tm, LANES), BF16)],
        compiler_params=_params("parallel", "arbitrary"),
        name="mla_kv",
    )(lat, lat, g.reshape(1, kkv), w_kv, *tables)


def _mla_attn_kernel(q_ref, k_ref, v_ref, o_ref, *, tq, scale):
    qi = pl.program_id(2)
    q = q_ref[...]
    dv = v_ref.shape[1]
    _, _, visible = _chunk_causal(tq)

    def block(kb, carry, diag):
        start = pl.multiple_of(kb * tq, tq)
        k = k_ref[pl.ds(start, tq), :]
        v = v_ref[pl.ds(start, tq), :]
        s = _dot_nt(q, k) * scale
        if diag:
            s = jnp.where(visible, s, NEG)
        return _online_step(s, v, *carry)

    def body(kb, carry):
        return block(kb, carry, False)

    init = (jnp.full((tq, 1), NEG, F32), jnp.zeros((tq, 1), F32), jnp.zeros((tq, dv), F32))
    carry = lax.fori_loop(0, qi, body, init)
    m, l, acc = block(qi, carry, True)
    o_ref[...] = (acc / l).astype(o_ref.dtype)


def mla_attention(q, k, v, batch, seq, n_heads):
    tq = ATTN_TILE
    nq = seq // tq
    hw = QK_NOPE + LANES
    return pl.pallas_call(
        functools.partial(_mla_attn_kernel, tq=tq, scale=1.0 / math.sqrt(QK_NOPE + QK_ROPE)),
        grid=(batch, n_heads, nq),
        in_specs=[
            pl.BlockSpec((tq, hw), lambda b, h, i: (b * nq + i, h)),
            pl.BlockSpec((seq, hw), lambda b, h, i: (b, h)),
            pl.BlockSpec((seq, V_MLA), lambda b, h, i: (b, h)),
        ],
        out_specs=pl.BlockSpec((tq, V_MLA), lambda b, h, i: (b * nq + i, h)),
        out_shape=jax.ShapeDtypeStruct((batch * seq, n_heads * V_MLA), BF16),
        compiler_params=_params("parallel", "parallel", "arbitrary"),
        name="mla_attention",
    )(q, k, v)


def _even_mixer(hn, w_in, w_out, diff_lambda, subln_g, batch, seq, layer):
    d_model = hn.shape[1]
    w_sb = d_model // 2
    n_sb = w_sb // HEAD_DIM
    n_diff = w_sb // (2 * HEAD_DIM)
    cb = w_sb // LANES
    qkv = matmul([(hn, w_in, 0)], w_in.shape[1], BF16, tm=1024, tn=512, name="even_in_proj")
    a = sb_attention(qkv, batch, seq, n_sb, 0, cb, 2 * cb)
    lam_init = 0.8 - 0.6 * math.exp(-0.3 * layer)
    bo = diff_attention(qkv, diff_lambda, subln_g, batch, seq, n_diff,
                        3 * cb, 4 * cb, 5 * cb // 2, lam_init)
    return matmul([(a, w_out, 0), (bo, w_out, 1)], d_model, F32, tm=1024, tn=512, name="mix_out_proj")


def _odd_mixer(hn, w_in, w_out, rel_bias, q_norm_g, w_uq, kv_norm_g, w_ukv, batch, seq):
    d_model = hn.shape[1]
    w_ch = d_model // 2
    n_ch = w_ch // HEAD_DIM
    n_mla = w_ch // HEAD_DIM
    cb = w_ch // LANES
    q_lora, kv_lora = q_norm_g.shape[0], kv_norm_g.shape[0]
    n_attn = 3 * w_ch
    n_lat = q_lora + kv_lora + QK_ROPE
    lat_pad = -n_lat % LANES
    w_attn = w_in[:, :n_attn].astype(BF16)
    w_lat = jnp.pad(w_in[:, n_attn:], ((0, 0), (0, lat_pad))).astype(BF16)
    qkv = matmul([(hn, w_attn, 0)], n_attn, BF16, tm=1024, tn=512, name="odd_in_proj")
    lat = matmul([(hn, w_lat, 0)], n_lat + lat_pad, F32, tm=512, tn=n_lat + lat_pad, name="odd_lat_proj")
    c = chunk_attention(qkv, rel_bias, batch, seq, n_ch, 0, cb, 2 * cb)

    tables = rope_tables(seq)
    hw = QK_NOPE + LANES
    wq = w_uq.reshape(q_lora, n_mla, QK_NOPE + QK_ROPE)
    wq = jnp.pad(wq, ((0, 0), (0, 0), (0, hw - QK_NOPE - QK_ROPE))).reshape(q_lora, n_mla * hw).astype(BF16)
    q = mla_q(lat, q_norm_g, wq, tables, seq, n_mla)
    assert q_lora % kv_lora == 0 and (q_lora + kv_lora) % LANES == 0
    k, v = mla_kv(lat, kv_norm_g, w_ukv.astype(BF16), tables, seq, n_mla,
                  q_lora // kv_lora, (q_lora + kv_lora) // LANES)
    dm = mla_attention(q, k, v, batch, seq, n_mla)
    return matmul([(c, w_out, 0), (dm, w_out, 1)], d_model, F32, tm=1024, tn=512, name="mix_out_proj")


def _ffn(h, w_in, conv_w, conv_b, w_out, seq):
    g = ffn_in(h, w_in.astype(BF16), conv_w, conv_b, seq)
    return matmul([(g, w_out.astype(BF16), 0)], w_out.shape[1], F32, tm=512, tn=256, name="ffn_out_proj")


def kernel(x, norm_g, even_w_in, even_w_out, diff_lambda, diff_subln_g, odd_w_in, odd_w_out,
           ch_rel_bias, mla_q_norm_g, mla_w_uq, mla_kv_norm_g, mla_w_ukv, ffn_w_in, ffn_conv_w,
           ffn_conv_b, ffn_w_out):
    batch, seq, d_model = x.shape
    depth = norm_g.shape[0]
    xf = x.reshape(batch * seq, d_model)
    hn = norm_cast(xf, norm_g[0, 0])
    for layer in range(depth):
        g = norm_g[layer]
        i = layer // 2
        if layer % 2 == 0:
            mix = _even_mixer(hn, even_w_in[i].astype(BF16), even_w_out[i].astype(BF16),
                              diff_lambda[i], diff_subln_g[i], batch, seq, layer)
        else:
            mix = _odd_mixer(hn, odd_w_in[i], odd_w_out[i].astype(BF16), ch_rel_bias[i],
                             mla_q_norm_g[i], mla_w_uq[i], mla_kv_norm_g[i], mla_w_ukv[i],
                             batch, seq)
        xf, h2 = resid_norm(xf, mix, g[1], g[2])
        f = _ffn(h2, ffn_w_in[layer], ffn_conv_w[layer], ffn_conv_b[layer], ffn_w_out[layer], seq)
        if layer + 1 < depth:
            xf, hn = resid_norm(xf, f, g[3], norm_g[layer + 1, 0])
        else:
            xf = resid(xf, f, g[3])
    return xf.reshape(batch, seq, d_model)
```

```python
import functools
import math

import jax
import jax.numpy as jnp
from jax import lax
from jax.experimental import pallas as pl
from jax.experimental.pallas import tpu as pltpu

F32 = jnp.float32
BF16 = jnp.bfloat16

CHUNK = 64
HEAD_DIM = 128
LEFT_CHUNKS = 8
REL_CLIP = 128
QK_NOPE = 128
QK_ROPE = 64
V_MLA = 128
ROPE_THETA = 10000.0
CONV_W = 3
EPS = 1e-6
NEG = -1e30

LANES = 128
BF16_SUBLANES = 16
VMEM_LIMIT = 52 * 1024 * 1024

ATTN_TILE = 256
SB_HEADS_PER_STEP = 4
MLA_HEADS_PER_STEP = 4
DIFF_HEADS_PER_STEP = 2


def _params(*sem):
    return pltpu.CompilerParams(dimension_semantics=sem, vmem_limit_bytes=VMEM_LIMIT)


def _rms(x, g):
    ms = jnp.mean(x * x, axis=-1, keepdims=True)
    return x * lax.rsqrt(ms + EPS) * g


def _dot(a, b):
    return jnp.dot(a, b, preferred_element_type=F32)


def _dot_nt(a, b):
    return lax.dot_general(a, b, (((1,), (1,)), ((), ())), preferred_element_type=F32)


def _norm_cast_kernel(x_ref, g_ref, h_ref):
    h_ref[...] = _rms(x_ref[...], g_ref[...]).astype(h_ref.dtype)


def _resid_norm_kernel(x_ref, y_ref, g1_ref, g2_ref, xo_ref, h_ref):
    xn = x_ref[...] + _rms(y_ref[...], g1_ref[...])
    xo_ref[...] = xn
    h_ref[...] = _rms(xn, g2_ref[...]).astype(h_ref.dtype)


def _resid_kernel(x_ref, y_ref, g_ref, xo_ref):
    xo_ref[...] = x_ref[...] + _rms(y_ref[...], g_ref[...])


def _row_spec(tr, d):
    return pl.BlockSpec((tr, d), lambda i: (i, 0))


def _vec_spec(d):
    return pl.BlockSpec((1, d), lambda i: (0, 0))


def norm_cast(x, g, tr=256):
    t, d = x.shape
    return pl.pallas_call(
        _norm_cast_kernel,
        grid=(t // tr,),
        in_specs=[_row_spec(tr, d), _vec_spec(d)],
        out_specs=_row_spec(tr, d),
        out_shape=jax.ShapeDtypeStruct((t, d), BF16),
        compiler_params=_params("parallel"),
        name="norm_cast",
    )(x, g.reshape(1, d))


def resid_norm(x, y, g1, g2, tr=256):
    t, d = x.shape
    return pl.pallas_call(
        _resid_norm_kernel,
        grid=(t // tr,),
        in_specs=[_row_spec(tr, d), _row_spec(tr, d), _vec_spec(d), _vec_spec(d)],
        out_specs=[_row_spec(tr, d), _row_spec(tr, d)],
        out_shape=[jax.ShapeDtypeStruct((t, d), F32), jax.ShapeDtypeStruct((t, d), BF16)],
        compiler_params=_params("parallel"),
        name="resid_norm",
    )(x, y, g1.reshape(1, d), g2.reshape(1, d))


def resid(x, y, g, tr=256):
    t, d = x.shape
    return pl.pallas_call(
        _resid_kernel,
        grid=(t // tr,),
        in_specs=[_row_spec(tr, d), _row_spec(tr, d), _vec_spec(d)],
        out_specs=_row_spec(tr, d),
        out_shape=jax.ShapeDtypeStruct((t, d), F32),
        compiler_params=_params("parallel"),
        name="resid",
    )(x, y, g.reshape(1, d))


def _matmul_kernel(*refs, n_pairs):
    o_ref = refs[2 * n_pairs]
    acc = _dot(refs[0][...], refs[n_pairs][...])
    for p in range(1, n_pairs):
        acc = acc + _dot(refs[p][...], refs[n_pairs + p][...])
    o_ref[...] = acc.astype(o_ref.dtype)


def matmul(pairs, n, out_dtype, tm, tn, name):
    m = pairs[0][0].shape[0]
    n_pairs = len(pairs)
    a_specs, w_specs, args_a, args_w = [], [], [], []
    for a, w, rb in pairs:
        k = a.shape[1]
        a_specs.append(pl.BlockSpec((tm, k), lambda i, j: (i, 0)))
        w_specs.append(pl.BlockSpec((k, tn), lambda i, j, rb=rb: (rb, j)))
        args_a.append(a)
        args_w.append(w)
    return pl.pallas_call(
        functools.partial(_matmul_kernel, n_pairs=n_pairs),
        grid=(m // tm, n // tn),
        in_specs=a_specs + w_specs,
        out_specs=pl.BlockSpec((tm, tn), lambda i, j: (i, j)),
        out_shape=jax.ShapeDtypeStruct((m, n), out_dtype),
        compiler_params=_params("parallel", "arbitrary"),
        name=name,
    )(*args_a, *args_w)


def _gelu_tanh(x):
    c = math.sqrt(2.0 / math.pi)
    return x * (0.5 * (1.0 + jnp.tanh(c * (x + 0.044715 * (x * x * x)))))


def _ffn_in_kernel(a_ref, ah_ref, wg_ref, wv_ref, cwg_ref, cwv_ref, cbg_ref, cbv_ref,
                   o_ref, *, tm, seq):
    i = pl.program_id(0)
    a = a_ref[...]
    ah = ah_ref[...]
    seq_start = (i * tm) % seq == 0
    row = lax.broadcasted_iota(jnp.int32, o_ref.shape, 0)

    def conv(w_ref, cw_ref, cb_ref):
        w = w_ref[...]
        u = _dot(a, w)
        uh = _dot(ah, w)
        uh = jnp.where(seq_start, 0.0, uh)
        p1 = uh[BF16_SUBLANES - 1:BF16_SUBLANES, :]
        p2 = uh[BF16_SUBLANES - 2:BF16_SUBLANES - 1, :]
        u1 = jnp.where(row == 0, p1, pltpu.roll(u, 1, 0))
        u2 = jnp.where(row == 0, p2, jnp.where(row == 1, p1, pltpu.roll(u, 2, 0)))
        cw = cw_ref[...]
        return cb_ref[...] + (cw[0:1, :] * u2 + cw[1:2, :] * u1 + cw[2:3, :] * u)

    gate = conv(wg_ref, cwg_ref, cbg_ref)
    val = conv(wv_ref, cwv_ref, cbv_ref)
    o_ref[...] = (_gelu_tanh(gate) * val).astype(o_ref.dtype)


def ffn_in(h, w_in, conv_w, conv_b, seq, tm=1024, tn=256):
    t, k = h.shape
    f = w_in.shape[1] // 2
    nf = f // tn
    halo = BF16_SUBLANES
    hb = tm // halo
    return pl.pallas_call(
        functools.partial(_ffn_in_kernel, tm=tm, seq=seq),
        grid=(t // tm, nf),
        in_specs=[
            pl.BlockSpec((tm, k), lambda i, j: (i, 0)),
            pl.BlockSpec((halo, k), lambda i, j: (jnp.maximum(i * hb - 1, 0), 0)),
            pl.BlockSpec((k, tn), lambda i, j: (0, j)),
            pl.BlockSpec((k, tn), lambda i, j: (0, j + nf)),
            pl.BlockSpec((CONV_W, tn), lambda i, j: (0, j)),
            pl.BlockSpec((CONV_W, tn), lambda i, j: (0, j + nf)),
            pl.BlockSpec((1, tn), lambda i, j: (0, j)),
            pl.BlockSpec((1, tn), lambda i, j: (0, j + nf)),
        ],
        out_specs=pl.BlockSpec((tm, tn), lambda i, j: (i, j)),
        out_shape=jax.ShapeDtypeStruct((t, f), BF16),
        compiler_params=_params("parallel", "arbitrary"),
        name="ffn_in",
    )(h, h, w_in, w_in, conv_w, conv_w, conv_b.reshape(1, 2 * f), conv_b.reshape(1, 2 * f))


def _sb_kernel(q_ref, k_ref, v_ref, o_ref, *, tq, scale, g):
    qi = pl.program_id(2)
    d = HEAD_DIM
    heads = [slice(hh * d, (hh + 1) * d) for hh in range(g)]
    qs = [q_ref[:, cs] for cs in heads]
    row = lax.broadcasted_iota(jnp.int32, (tq, tq), 0)
    col = lax.broadcasted_iota(jnp.int32, (tq, tq), 1)
    strict = col < row
    r2 = lax.broadcasted_iota(jnp.int32, (2 * tq, tq), 0)
    c2 = lax.broadcasted_iota(jnp.int32, (2 * tq, tq), 1)
    tri = (jnp.where(r2 >= tq, r2 - tq, r2) >= c2).astype(BF16)

    def block(kb, carry, diag):
        start = pl.multiple_of(kb * tq, tq)
        zs = [_dot_nt(q, k_ref[pl.ds(start, tq), cs]) * scale for q, cs in zip(qs, heads)]
        lgs, csums = [], []
        for z in zs:
            lg = -(jnp.maximum(z, 0.0) + jnp.log1p(jnp.exp(-jnp.abs(z))))
            if diag:
                lg = jnp.where(strict, lg, 0.0)
            hi = lg.astype(BF16)
            lo = (lg - hi.astype(F32)).astype(BF16)
            lgs.append(lg)
            csums.append(_dot(jnp.concatenate([hi, lo], axis=1), tri))
        out = []
        for z, lg, csum, cs, (rsum, acc) in zip(zs, lgs, csums, heads, carry):
            w = jnp.exp(z + csum + rsum)
            if diag:
                w = jnp.where(strict, w, 0.0)
            acc = acc + _dot(w.astype(BF16), v_ref[pl.ds(start, tq), cs])
            rsum = rsum + jnp.sum(lg, axis=-1, keepdims=True)
            out.append((rsum, acc))
        return tuple(out)

    init = tuple((jnp.zeros((tq, 1), F32), jnp.zeros((tq, d), F32)) for _ in range(g))
    carry = block(qi, init, True)

    def body(it, carry):
        return block(qi - 1 - it, carry, False)

    carry = lax.fori_loop(0, qi, body, carry)
    for cs, (_, acc) in zip(heads, carry):
        o_ref[:, cs] = acc.astype(o_ref.dtype)


def sb_attention(qkv, batch, seq, n_heads, q_col, k_col, v_col, g=SB_HEADS_PER_STEP):
    tq = ATTN_TILE
    nq = seq // tq
    d = HEAD_DIM
    assert n_heads % g == 0 and q_col % g == 0 and k_col % g == 0 and v_col % g == 0
    return pl.pallas_call(
        functools.partial(_sb_kernel, tq=tq, scale=1.0 / math.sqrt(d), g=g),
        grid=(batch, n_heads // g, nq),
        in_specs=[
            pl.BlockSpec((tq, g * d), lambda b, h, i: (b * nq + i, q_col // g + h)),
            pl.BlockSpec((seq, g * d), lambda b, h, i: (b, k_col // g + h)),
            pl.BlockSpec((seq, g * d), lambda b, h, i: (b, v_col // g + h)),
        ],
        out_specs=pl.BlockSpec((tq, g * d), lambda b, h, i: (b * nq + i, h)),
        out_shape=jax.ShapeDtypeStruct((batch * seq, n_heads * d), BF16),
        compiler_params=_params("parallel", "parallel", "arbitrary"),
        name="sb_attention",
    )(qkv, qkv, qkv)


def _online_step(s, v, m, l, acc):
    m_new = jnp.maximum(m, jnp.max(s, axis=-1, keepdims=True))
    alpha = jnp.exp(m - m_new)
    p = jnp.exp(s - m_new)
    l = alpha * l + jnp.sum(p, axis=-1, keepdims=True)
    acc = alpha * acc + _dot(p.astype(BF16), v)
    return m_new, l, acc


def _chunk_causal(tq):
    row = lax.broadcasted_iota(jnp.int32, (tq, tq), 0)
    col = lax.broadcasted_iota(jnp.int32, (tq, tq), 1)
    return row, col, (col // CHUNK) <= (row // CHUNK)


def _diff_kernel(q_ref, k_ref, v_ref, lam_ref, g_ref, o_ref, *, tq, scale, lam_init, g):
    hg = pl.program_id(1)
    qi = pl.program_id(2)
    d = HEAD_DIM
    dv = 2 * HEAD_DIM
    qk_cols = [slice(c * d, (c + 1) * d) for c in range(2 * g)]
    v_cols = [slice(hh * dv, (hh + 1) * dv) for hh in range(g)]
    qs = [q_ref[:, cs] for cs in qk_cols]
    row, col, visible = _chunk_causal(tq)
    dist = (row - col).astype(F32)
    nbias_diag, nbias_off, slopes = [], [], []
    for hh in range(g):
        slope = lax.bitcast_convert_type(
            jnp.full((1, 1), (126 - (hg * g + hh)) * (1 << 23), jnp.int32), F32)
        slopes.append(slope)
        nbias_diag.append(-slope * jnp.abs(dist))
        nbias_off.append(-slope * dist)

    def block(kb, carry, diag):
        start = pl.multiple_of(kb * tq, tq)
        ss = [_dot_nt(q, k_ref[pl.ds(start, tq), cs]) * scale for q, cs in zip(qs, qk_cols)]
        for c in range(2 * g):
            hh = c // 2
            if diag:
                ss[c] = jnp.where(visible, ss[c] + nbias_diag[hh], NEG)
            else:
                ss[c] = ss[c] + (nbias_off[hh] - slopes[hh] * ((qi - kb) * tq).astype(F32))
        return tuple(_online_step(ss[c], v_ref[pl.ds(start, tq), v_cols[c // 2]], *carry[c])
                     for c in range(2 * g))

    def body(kb, carry):
        return block(kb, carry, False)

    init = tuple((jnp.full((tq, 1), NEG, F32), jnp.zeros((tq, 1), F32), jnp.zeros((tq, dv), F32))
                 for _ in range(2 * g))
    carry = lax.fori_loop(0, qi, body, init)
    carry = block(qi, carry, True)

    lf = lam_ref[...]
    lam = (jnp.exp(jnp.sum(lf[0:1, :] * lf[1:2, :], axis=-1, keepdims=True))
           - jnp.exp(jnp.sum(lf[2:3, :] * lf[3:4, :], axis=-1, keepdims=True)) + lam_init)
    for hh in range(g):
        (_, l1, a1), (_, l2, a2) = carry[2 * hh], carry[2 * hh + 1]
        o = a1 / l1 - lam * (a2 / l2)
        o_ref[:, v_cols[hh]] = (_rms(o, g_ref[...]) * (1.0 - lam_init)).astype(o_ref.dtype)


def diff_attention(qkv, diff_lambda, subln_g, batch, seq, n_heads, q_col, k_col, v_col, lam_init,
                   g=DIFF_HEADS_PER_STEP):
    tq = ATTN_TILE
    nq = seq // tq
    dv = 2 * HEAD_DIM
    assert n_heads == 8, "ALiBi slopes are built as exact powers of two"
    assert n_heads % g == 0 and q_col % g == 0 and k_col % g == 0 and v_col % g == 0
    return pl.pallas_call(
        functools.partial(_diff_kernel, tq=tq, scale=1.0 / math.sqrt(HEAD_DIM), lam_init=lam_init, g=g),
        grid=(batch, n_heads // g, nq),
        in_specs=[
            pl.BlockSpec((tq, g * dv), lambda b, h, i: (b * nq + i, q_col // g + h)),
            pl.BlockSpec((seq, g * dv), lambda b, h, i: (b, k_col // g + h)),
            pl.BlockSpec((seq, g * dv), lambda b, h, i: (b, v_col // g + h)),
            pl.BlockSpec((4, HEAD_DIM), lambda b, h, i: (0, 0)),
            pl.BlockSpec((1, dv), lambda b, h, i: (0, 0)),
        ],
        out_specs=pl.BlockSpec((tq, g * dv), lambda b, h, i: (b * nq + i, h)),
        out_shape=jax.ShapeDtypeStruct((batch * seq, n_heads * dv), BF16),
        compiler_params=_params("parallel", "parallel", "arbitrary"),
        name="diff_attention",
    )(qkv, qkv, qkv, diff_lambda, subln_g.reshape(1, dv))


def _chunk_kernel(q_ref, k_ref, v_ref, tab_ref, o_ref, *, tq, scale, n_win):
    qi = pl.program_id(2)
    q = q_ref[...]
    s_blocks, v_blocks = [], []
    for w in range(n_win):
        kb = qi - (n_win - 1) + w
        start = pl.multiple_of(jnp.maximum(kb, 0) * tq, tq)
        k = k_ref[pl.ds(start, tq), :]
        v_blocks.append(v_ref[pl.ds(start, tq), :])
        s = _dot_nt(q, k) * scale + tab_ref[0, :, w * tq:(w + 1) * tq]
        if w < n_win - 1:
            s = jnp.where(kb >= 0, s, NEG)
        s_blocks.append(s)
    m = s_blocks[0].max(axis=-1, keepdims=True)
    for s in s_blocks[1:]:
        m = jnp.maximum(m, s.max(axis=-1, keepdims=True))
    l = jnp.zeros_like(m)
    acc = jnp.zeros(o_ref.shape, F32)
    for s, v in zip(s_blocks, v_blocks):
        p = jnp.exp(s - m)
        l = l + jnp.sum(p, axis=-1, keepdims=True)
        acc = acc + _dot(p.astype(BF16), v)
    o_ref[...] = (acc / l).astype(o_ref.dtype)


def chunk_bias_table(rel_bias, tq):
    left = LEFT_CHUNKS * CHUNK
    n_heads = rel_bias.shape[0]
    width = tq + left
    period = 2 * left
    assert left >= REL_CLIP and tq <= left
    rb = rel_bias.astype(F32)
    hi = jnp.broadcast_to(rb[:, -1:], (n_heads, left - REL_CLIP))
    mid = rb[:, ::-1]
    lo = jnp.broadcast_to(rb[:, :1], (n_heads, tq - REL_CLIP - 1))
    neg_d = jnp.broadcast_to(rb[:, -1:], (n_heads, period - width))
    diag = jnp.concatenate([hi, mid, lo, neg_d], axis=1)
    flat = jnp.tile(diag, (1, tq))[:, :tq * (period - 1)]
    bias = flat.reshape(n_heads, tq, period - 1)[:, :, :width]
    i = jnp.arange(tq)[:, None]
    j = jnp.arange(width)[None, :]
    qc, kc = i // CHUNK, j // CHUNK
    allowed = (kc >= qc) & (kc <= qc + LEFT_CHUNKS)
    return jnp.where(allowed[None], bias, NEG)


def chunk_attention(qkv, rel_bias, batch, seq, n_heads, q_col, k_col, v_col):
    tq = ATTN_TILE
    nq = seq // tq
    d = HEAD_DIM
    left = LEFT_CHUNKS * CHUNK
    assert left % tq == 0
    n_win = left // tq + 1
    table = chunk_bias_table(rel_bias, tq)
    return pl.pallas_call(
        functools.partial(_chunk_kernel, tq=tq, scale=1.0 / math.sqrt(d), n_win=n_win),
        grid=(n_heads, batch, nq),
        in_specs=[
            pl.BlockSpec((tq, d), lambda h, b, i: (b * nq + i, q_col + h)),
            pl.BlockSpec((seq, d), lambda h, b, i: (b, k_col + h)),
            pl.BlockSpec((seq, d), lambda h, b, i: (b, v_col + h)),
            pl.BlockSpec((1, tq, tq + left), lambda h, b, i: (h, 0, 0)),
        ],
        out_specs=pl.BlockSpec((tq, d), lambda h, b, i: (b * nq + i, h)),
        out_shape=jax.ShapeDtypeStruct((batch * seq, n_heads * d), BF16),
        compiler_params=_params("parallel", "parallel", "arbitrary"),
        name="chunk_attention",
    )(qkv, qkv, qkv, table)


def rope_tables(seq):
    half = QK_ROPE // 2
    pos = jnp.arange(seq, dtype=F32)
    inv_freq = ROPE_THETA ** (-jnp.arange(0, QK_ROPE, 2, dtype=F32) / QK_ROPE)
    ang = pos[:, None] * inv_freq[None, :]
    cos, sin = jnp.cos(ang), jnp.sin(ang)
    z = lambda n: jnp.zeros((seq, n), F32)
    cos_t = jnp.concatenate([cos, cos, z(LANES - 2 * half)], axis=1)
    sin_a = jnp.concatenate([-sin, z(LANES - half)], axis=1)
    sin_b = jnp.concatenate([z(half), sin, z(LANES - 2 * half)], axis=1)
    return cos_t, sin_a, sin_b


def _rope(x, cos_t, sin_a, sin_b):
    half = QK_ROPE // 2
    return x * cos_t + pltpu.roll(x, LANES - half, 1) * sin_a + pltpu.roll(x, half, 1) * sin_b


def _mla_q_kernel(c_ref, g_ref, w_ref, cos_ref, sa_ref, sb_ref, o_ref, an_ref):
    @pl.when(pl.program_id(1) == 0)
    def _():
        an_ref[...] = _rms(c_ref[...], g_ref[...]).astype(an_ref.dtype)

    res = _dot(an_ref[...], w_ref[...])
    o_ref[:, :QK_NOPE] = res[:, :QK_NOPE].astype(o_ref.dtype)
    o_ref[:, QK_NOPE:] = _rope(res[:, QK_NOPE:], cos_ref[...], sa_ref[...], sb_ref[...]).astype(o_ref.dtype)


def mla_q(lat, g, w_q, tables, seq, n_heads, tm=512):
    t = lat.shape[0]
    kq = g.shape[0]
    hw = QK_NOPE + LANES
    npos = seq // tm
    tab_spec = pl.BlockSpec((tm, LANES), lambda i, j: (i % npos, 0))
    return pl.pallas_call(
        _mla_q_kernel,
        grid=(t // tm, n_heads),
        in_specs=[
            pl.BlockSpec((tm, kq), lambda i, j: (i, 0)),
            pl.BlockSpec((1, kq), lambda i, j: (0, 0)),
            pl.BlockSpec((kq, hw), lambda i, j: (0, j)),
            tab_spec, tab_spec, tab_spec,
        ],
        out_specs=pl.BlockSpec((tm, hw), lambda i, j: (i, j)),
        out_shape=jax.ShapeDtypeStruct((t, n_heads * hw), BF16),
        scratch_shapes=[pltpu.VMEM((tm, kq), BF16)],
        compiler_params=_params("parallel", "arbitrary"),
        name="mla_q",
    )(lat, g.reshape(1, kq), w_q, *tables)


def _mla_kv_kernel(c_ref, kr_ref, g_ref, w_ref, cos_ref, sa_ref, sb_ref, k_ref, v_ref,
                   an_ref, kr_scr):
    @pl.when(pl.program_id(1) == 0)
    def _():
        an_ref[...] = _rms(c_ref[...], g_ref[...]).astype(an_ref.dtype)
        kr_scr[...] = _rope(kr_ref[...], cos_ref[...], sa_ref[...], sb_ref[...]).astype(kr_scr.dtype)

    res = _dot(an_ref[...], w_ref[...])
    k_ref[:, :QK_NOPE] = res[:, :QK_NOPE].astype(k_ref.dtype)
    k_ref[:, QK_NOPE:] = kr_scr[...]
    v_ref[...] = res[:, QK_NOPE:].astype(v_ref.dtype)


def mla_kv(lat, g, w_kv, tables, seq, n_heads, ckv_col, kr_col, tm=512):
    t = lat.shape[0]
    kkv = g.shape[0]
    hw = QK_NOPE + LANES
    npos = seq // tm
    tab_spec = pl.BlockSpec((tm, LANES), lambda i, j: (i % npos, 0))
    return pl.pallas_call(
        _mla_kv_kernel,
        grid=(t // tm, n_heads),
        in_specs=[
            pl.BlockSpec((tm, kkv), lambda i, j: (i, ckv_col)),
            pl.BlockSpec((tm, LANES), lambda i, j: (i, kr_col)),
            pl.BlockSpec((1, kkv), lambda i, j: (0, 0)),
            pl.BlockSpec((kkv, QK_NOPE + V_MLA), lambda i, j: (0, j)),
            tab_spec, tab_spec, tab_spec,
        ],
        out_specs=[pl.BlockSpec((tm, hw), lambda i, j: (i, j)),
                   pl.BlockSpec((tm, V_MLA), lambda i, j: (i, j))],
        out_shape=[jax.ShapeDtypeStruct((t, n_heads * hw), BF16),
                   jax.ShapeDtypeStruct((t, n_heads * V_MLA), BF16)],
        scratch_shapes=[pltpu.VMEM((tm, kkv), BF16), pltpu.VMEM((tm, LANES), BF16)],
        compiler_params=_params("parallel", "arbitrary"),
        name="mla_kv",
    )(lat, lat, g.reshape(1, kkv), w_kv, *tables)


def _mla_attn_kernel(q_ref, k_ref, v_ref, o_ref, *, tq, scale, g):
    qi = pl.program_id(2)
    hw = QK_NOPE + LANES
    dv = V_MLA
    qk_cols = [slice(hh * hw, (hh + 1) * hw) for hh in range(g)]
    v_cols = [slice(hh * dv, (hh + 1) * dv) for hh in range(g)]
    qs = [q_ref[:, cs] for cs in qk_cols]
    _, _, visible = _chunk_causal(tq)

    def block(kb, carry, diag):
        start = pl.multiple_of(kb * tq, tq)
        ss = [_dot_nt(q, k_ref[pl.ds(start, tq), cs]) * scale for q, cs in zip(qs, qk_cols)]
        if diag:
            ss = [jnp.where(visible, s, NEG) for s in ss]
        return tuple(_online_step(s, v_ref[pl.ds(start, tq), vs], *st)
                     for s, vs, st in zip(ss, v_cols, carry))

    def body(kb, carry):
        return block(kb, carry, False)

    init = tuple((jnp.full((tq, 1), NEG, F32), jnp.zeros((tq, 1), F32), jnp.zeros((tq, dv), F32))
                 for _ in range(g))
    carry = lax.fori_loop(0, qi, body, init)
    carry = block(qi, carry, True)
    for vs, (_, l, acc) in zip(v_cols, carry):
        o_ref[:, vs] = (acc / l).astype(o_ref.dtype)


def mla_attention(q, k, v, batch, seq, n_heads, g=MLA_HEADS_PER_STEP):
    tq = ATTN_TILE
    nq = seq // tq
    hw = QK_NOPE + LANES
    assert n_heads % g == 0
    return pl.pallas_call(
        functools.partial(_mla_attn_kernel, tq=tq, scale=1.0 / math.sqrt(QK_NOPE + QK_ROPE), g=g),
        grid=(batch, n_heads // g, nq),
        in_specs=[
            pl.BlockSpec((tq, g * hw), lambda b, h, i: (b * nq + i, h)),
            pl.BlockSpec((seq, g * hw), lambda b, h, i: (b, h)),
            pl.BlockSpec((seq, g * V_MLA), lambda b, h, i: (b, h)),
        ],
        out_specs=pl.BlockSpec((tq, g * V_MLA), lambda b, h, i: (b * nq + i, h)),
        out_shape=jax.ShapeDtypeStruct((batch * seq, n_heads * V_MLA), BF16),
        compiler_params=_params("parallel", "parallel", "arbitrary"),
        name="mla_attention",
    )(q, k, v)


def _even_mixer(hn, w_in, w_out, diff_lambda, subln_g, batch, seq, layer):
    d_model = hn.shape[1]
    w_sb = d_model // 2
    n_sb = w_sb // HEAD_DIM
    n_diff = w_sb // (2 * HEAD_DIM)
    cb = w_sb // LANES
    qkv = matmul([(hn, w_in, 0)], w_in.shape[1], BF16, tm=1024, tn=512, name="even_in_proj")
    a = sb_attention(qkv, batch, seq, n_sb, 0, cb, 2 * cb)
    lam_init = 0.8 - 0.6 * math.exp(-0.3 * layer)
    bo = diff_attention(qkv, diff_lambda, subln_g, batch, seq, n_diff,
                        3 * cb // 2, 4 * cb // 2, 5 * cb // 2, lam_init)
    return matmul([(a, w_out, 0), (bo, w_out, 1)], d_model, F32, tm=1024, tn=512, name="mix_out_proj")


def _odd_mixer(hn, w_in, w_out, rel_bias, q_norm_g, w_uq, kv_norm_g, w_ukv, batch, seq):
    d_model = hn.shape[1]
    w_ch = d_model // 2
    n_ch = w_ch // HEAD_DIM
    n_mla = w_ch // HEAD_DIM
    cb = w_ch // LANES
    q_lora, kv_lora = q_norm_g.shape[0], kv_norm_g.shape[0]
    n_attn = 3 * w_ch
    n_lat = q_lora + kv_lora + QK_ROPE
    lat_pad = -n_lat % LANES
    w_attn = w_in[:, :n_attn].astype(BF16)
    w_lat = jnp.pad(w_in[:, n_attn:], ((0, 0), (0, lat_pad))).astype(BF16)
    qkv = matmul([(hn, w_attn, 0)], n_attn, BF16, tm=1024, tn=512, name="odd_in_proj")
    lat = matmul([(hn, w_lat, 0)], n_lat + lat_pad, F32, tm=512, tn=n_lat + lat_pad, name="odd_lat_proj")
    c = chunk_attention(qkv, rel_bias, batch, seq, n_ch, 0, cb, 2 * cb)

    tables = rope_tables(seq)
    hw = QK_NOPE + LANES
    wq = w_uq.reshape(q_lora, n_mla, QK_NOPE + QK_ROPE)
    wq = jnp.pad(wq, ((0, 0), (0, 0), (0, hw - QK_NOPE - QK_ROPE))).reshape(q_lora, n_mla * hw).astype(BF16)
    q = mla_q(lat, q_norm_g, wq, tables, seq, n_mla)
    assert q_lora % kv_lora == 0 and (q_lora + kv_lora) % LANES == 0
    k, v = mla_kv(lat, kv_norm_g, w_ukv.astype(BF16), tables, seq, n_mla,
                  q_lora // kv_lora, (q_lora + kv_lora) // LANES)
    dm = mla_attention(q, k, v, batch, seq, n_mla)
    return matmul([(c, w_out, 0), (dm, w_out, 1)], d_model, F32, tm=1024, tn=512, name="mix_out_proj")


def _ffn(h, w_in, conv_w, conv_b, w_out, seq):
    g = ffn_in(h, w_in.astype(BF16), conv_w, conv_b, seq)
    return matmul([(g, w_out.astype(BF16), 0)], w_out.shape[1], F32, tm=512, tn=256, name="ffn_out_proj")


def kernel(x, norm_g, even_w_in, even_w_out, diff_lambda, diff_subln_g, odd_w_in, odd_w_out,
           ch_rel_bias, mla_q_norm_g, mla_w_uq, mla_kv_norm_g, mla_w_ukv, ffn_w_in, ffn_conv_w,
           ffn_conv_b, ffn_w_out):
    batch, seq, d_model = x.shape
    depth = norm_g.shape[0]
    xf = x.reshape(batch * seq, d_model)
    hn = norm_cast(xf, norm_g[0, 0])
    for layer in range(depth):
        g = norm_g[layer]
        i = layer // 2
        if layer % 2 == 0:
            mix = _even_mixer(hn, even_w_in[i].astype(BF16), even_w_out[i].astype(BF16),
                              diff_lambda[i], diff_subln_g[i], batch, seq, layer)
        else:
            mix = _odd_mixer(hn, odd_w_in[i], odd_w_out[i].astype(BF16), ch_rel_bias[i],
                             mla_q_norm_g[i], mla_w_uq[i], mla_kv_norm_g[i], mla_w_ukv[i],
                             batch, seq)
        xf, h2 = resid_norm(xf, mix, g[1], g[2])
        f = _ffn(h2, ffn_w_in[layer], ffn_conv_w[layer], ffn_conv_b[layer], ffn_w_out[layer], seq)
        if layer + 1 < depth:
            xf, hn = resid_norm(xf, f, g[3], norm_g[layer + 1, 0])
        else:
            xf = resid(xf, f, g[3])
    return xf.reshape(batch, seq, d_model)
```

```python
import functools
import math

import jax
import jax.numpy as jnp
from jax import lax
from jax.experimental import pallas as pl
from jax.experimental.pallas import tpu as pltpu

F32 = jnp.float32
BF16 = jnp.bfloat16

CHUNK = 64
HEAD_DIM = 128
LEFT_CHUNKS = 8
REL_CLIP = 128
QK_NOPE = 128
QK_ROPE = 64
V_MLA = 128
ROPE_THETA = 10000.0
CONV_W = 3
EPS = 1e-6
NEG = -1e30

LANES = 128
BF16_SUBLANES = 16
VMEM_LIMIT = 52 * 1024 * 1024

ATTN_TILE = 256
SB_HEADS_PER_STEP = 4
MLA_HEADS_PER_STEP = 4
DIFF_HEADS_PER_STEP = 2
CHUNK_HEADS_PER_STEP = 4
MLA_PROJ_HEADS_PER_STEP = 4


def _params(*sem):
    return pltpu.CompilerParams(dimension_semantics=sem, vmem_limit_bytes=VMEM_LIMIT)


def _rms(x, g):
    ms = jnp.mean(x * x, axis=-1, keepdims=True)
    return x * lax.rsqrt(ms + EPS) * g


def _dot(a, b):
    return jnp.dot(a, b, preferred_element_type=F32)


def _dot_nt(a, b):
    return lax.dot_general(a, b, (((1,), (1,)), ((), ())), preferred_element_type=F32)


def _norm_cast_kernel(x_ref, g_ref, h_ref):
    h_ref[...] = _rms(x_ref[...], g_ref[...]).astype(h_ref.dtype)


def _resid_norm_kernel(x_ref, y_ref, g1_ref, g2_ref, xo_ref, h_ref):
    xn = x_ref[...] + _rms(y_ref[...], g1_ref[...])
    xo_ref[...] = xn
    h_ref[...] = _rms(xn, g2_ref[...]).astype(h_ref.dtype)


def _resid_kernel(x_ref, y_ref, g_ref, xo_ref):
    xo_ref[...] = x_ref[...] + _rms(y_ref[...], g_ref[...])


def _row_spec(tr, d):
    return pl.BlockSpec((tr, d), lambda i: (i, 0))


def _vec_spec(d):
    return pl.BlockSpec((1, d), lambda i: (0, 0))


def norm_cast(x, g, tr=256):
    t, d = x.shape
    return pl.pallas_call(
        _norm_cast_kernel,
        grid=(t // tr,),
        in_specs=[_row_spec(tr, d), _vec_spec(d)],
        out_specs=_row_spec(tr, d),
        out_shape=jax.ShapeDtypeStruct((t, d), BF16),
        compiler_params=_params("parallel"),
        name="norm_cast",
    )(x, g.reshape(1, d))


def resid_norm(x, y, g1, g2, tr=256):
    t, d = x.shape
    return pl.pallas_call(
        _resid_norm_kernel,
        grid=(t // tr,),
        in_specs=[_row_spec(tr, d), _row_spec(tr, d), _vec_spec(d), _vec_spec(d)],
        out_specs=[_row_spec(tr, d), _row_spec(tr, d)],
        out_shape=[jax.ShapeDtypeStruct((t, d), F32), jax.ShapeDtypeStruct((t, d), BF16)],
        compiler_params=_params("parallel"),
        name="resid_norm",
    )(x, y, g1.reshape(1, d), g2.reshape(1, d))


def resid(x, y, g, tr=256):
    t, d = x.shape
    return pl.pallas_call(
        _resid_kernel,
        grid=(t // tr,),
        in_specs=[_row_spec(tr, d), _row_spec(tr, d), _vec_spec(d)],
        out_specs=_row_spec(tr, d),
        out_shape=jax.ShapeDtypeStruct((t, d), F32),
        compiler_params=_params("parallel"),
        name="resid",
    )(x, y, g.reshape(1, d))


def _matmul_kernel(*refs, n_pairs):
    o_ref = refs[2 * n_pairs]
    acc = _dot(refs[0][...], refs[n_pairs][...].astype(BF16))
    for p in range(1, n_pairs):
        acc = acc + _dot(refs[p][...], refs[n_pairs + p][...].astype(BF16))
    o_ref[...] = acc.astype(o_ref.dtype)


def matmul(pairs, n, out_dtype, tm, tn, name):
    m = pairs[0][0].shape[0]
    n_pairs = len(pairs)
    a_specs, w_specs, args_a, args_w = [], [], [], []
    for a, w, layer, rb in pairs:
        k = a.shape[1]
        a_specs.append(pl.BlockSpec((tm, k), lambda i, j: (i, 0)))
        w_specs.append(pl.BlockSpec((None, k, tn), lambda i, j, layer=layer, rb=rb: (layer, rb, j)))
        args_a.append(a)
        args_w.append(w)
    return pl.pallas_call(
        functools.partial(_matmul_kernel, n_pairs=n_pairs),
        grid=(m // tm, n // tn),
        in_specs=a_specs + w_specs,
        out_specs=pl.BlockSpec((tm, tn), lambda i, j: (i, j)),
        out_shape=jax.ShapeDtypeStruct((m, n), out_dtype),
        compiler_params=_params("parallel", "arbitrary"),
        name=name,
    )(*args_a, *args_w)


def _gelu_tanh(x):
    c = math.sqrt(2.0 / math.pi)
    return x * (0.5 * (1.0 + jnp.tanh(c * (x + 0.044715 * (x * x * x)))))


def _ffn_in_kernel(a_ref, ah_ref, wg_ref, wv_ref, cwg_ref, cwv_ref, cbg_ref, cbv_ref,
                   o_ref, *, tm, seq):
    i = pl.program_id(0)
    tn = o_ref.shape[1]
    a = a_ref[...]
    ah = ah_ref[...]
    seq_start = (i * tm) % seq == 0
    row8 = lax.broadcasted_iota(jnp.int32, (8, tn), 0)

    def conv(w_ref, cw_ref, cb_ref):
        w = w_ref[...].astype(BF16)
        u = _dot(a, w)
        uh = _dot(ah, w)
        uh = jnp.where(seq_start, 0.0, uh)
        p1 = uh[BF16_SUBLANES - 1:BF16_SUBLANES, :]
        p2 = uh[BF16_SUBLANES - 2:BF16_SUBLANES - 1, :]
        r1 = pltpu.roll(u, 1, 0)
        r2 = pltpu.roll(u, 2, 0)
        h1 = jnp.where(row8 == 0, p1, r1[:8, :])
        h2 = jnp.where(row8 == 0, p2, jnp.where(row8 == 1, p1, r2[:8, :]))
        u1 = jnp.concatenate([h1, r1[8:, :]], axis=0)
        u2 = jnp.concatenate([h2, r2[8:, :]], axis=0)
        cw = cw_ref[...]
        return cb_ref[...] + (cw[0:1, :] * u2 + cw[1:2, :] * u1 + cw[2:3, :] * u)

    gate = conv(wg_ref, cwg_ref, cbg_ref)
    val = conv(wv_ref, cwv_ref, cbv_ref)
    o_ref[...] = (_gelu_tanh(gate) * val).astype(o_ref.dtype)


def ffn_in(h, w_in, conv_w, conv_b, layer, seq, tm=1024, tn=256):
    t, k = h.shape
    f = w_in.shape[2] // 2
    nf = f // tn
    halo = BF16_SUBLANES
    hb = tm // halo
    conv_b = conv_b.reshape(conv_b.shape[0], 1, 2 * f)
    return pl.pallas_call(
        functools.partial(_ffn_in_kernel, tm=tm, seq=seq),
        grid=(t // tm, nf),
        in_specs=[
            pl.BlockSpec((tm, k), lambda i, j: (i, 0)),
            pl.BlockSpec((halo, k), lambda i, j: (jnp.maximum(i * hb - 1, 0), 0)),
            pl.BlockSpec((None, k, tn), lambda i, j: (layer, 0, j)),
            pl.BlockSpec((None, k, tn), lambda i, j: (layer, 0, j + nf)),
            pl.BlockSpec((None, CONV_W, tn), lambda i, j: (layer, 0, j)),
            pl.BlockSpec((None, CONV_W, tn), lambda i, j: (layer, 0, j + nf)),
            pl.BlockSpec((None, 1, tn), lambda i, j: (layer, 0, j)),
            pl.BlockSpec((None, 1, tn), lambda i, j: (layer, 0, j + nf)),
        ],
        out_specs=pl.BlockSpec((tm, tn), lambda i, j: (i, j)),
        out_shape=jax.ShapeDtypeStruct((t, f), BF16),
        compiler_params=_params("parallel", "arbitrary"),
        name="ffn_in",
    )(h, h, w_in, w_in, conv_w, conv_w, conv_b, conv_b)


def _sb_kernel(q_ref, k_ref, v_ref, o_ref, *, tq, scale, g):
    qi = pl.program_id(2)
    d = HEAD_DIM
    heads = [slice(hh * d, (hh + 1) * d) for hh in range(g)]
    qs = [q_ref[:, cs] for cs in heads]
    row = lax.broadcasted_iota(jnp.int32, (tq, tq), 0)
    col = lax.broadcasted_iota(jnp.int32, (tq, tq), 1)
    strict = col < row
    r2 = lax.broadcasted_iota(jnp.int32, (2 * tq, tq), 0)
    c2 = lax.broadcasted_iota(jnp.int32, (2 * tq, tq), 1)
    tri = (jnp.where(r2 >= tq, r2 - tq, r2) >= c2).astype(BF16)

    def block(kb, carry, diag):
        start = pl.multiple_of(kb * tq, tq)
        zs = [_dot_nt(q, k_ref[pl.ds(start, tq), cs]) * scale for q, cs in zip(qs, heads)]
        lgs, csums = [], []
        for z in zs:
            lg = -(jnp.maximum(z, 0.0) + jnp.log1p(jnp.exp(-jnp.abs(z))))
            if diag:
                lg = jnp.where(strict, lg, 0.0)
            hi = lg.astype(BF16)
            lo = (lg - hi.astype(F32)).astype(BF16)
            lgs.append(lg)
            csums.append(_dot(jnp.concatenate([hi, lo], axis=1), tri))
        out = []
        for z, lg, csum, cs, (rsum, acc) in zip(zs, lgs, csums, heads, carry):
            w = jnp.exp(z + csum + rsum)
            if diag:
                w = jnp.where(strict, w, 0.0)
            acc = acc + _dot(w.astype(BF16), v_ref[pl.ds(start, tq), cs])
            rsum = rsum + jnp.sum(lg, axis=-1, keepdims=True)
            out.append((rsum, acc))
        return tuple(out)

    init = tuple((jnp.zeros((tq, 1), F32), jnp.zeros((tq, d), F32)) for _ in range(g))
    carry = block(qi, init, True)

    def body(it, carry):
        return block(qi - 1 - it, carry, False)

    carry = lax.fori_loop(0, qi, body, carry)
    for cs, (_, acc) in zip(heads, carry):
        o_ref[:, cs] = acc.astype(o_ref.dtype)


def sb_attention(qkv, batch, seq, n_heads, q_col, k_col, v_col, g=SB_HEADS_PER_STEP):
    tq = ATTN_TILE
    nq = seq // tq
    d = HEAD_DIM
    assert n_heads % g == 0 and q_col % g == 0 and k_col % g == 0 and v_col % g == 0
    return pl.pallas_call(
        functools.partial(_sb_kernel, tq=tq, scale=1.0 / math.sqrt(d), g=g),
        grid=(batch, n_heads // g, nq),
        in_specs=[
            pl.BlockSpec((tq, g * d), lambda b, h, i: (b * nq + i, q_col // g + h)),
            pl.BlockSpec((seq, g * d), lambda b, h, i: (b, k_col // g + h)),
            pl.BlockSpec((seq, g * d), lambda b, h, i: (b, v_col // g + h)),
        ],
        out_specs=pl.BlockSpec((tq, g * d), lambda b, h, i: (b * nq + i, h)),
        out_shape=jax.ShapeDtypeStruct((batch * seq, n_heads * d), BF16),
        compiler_params=_params("parallel", "parallel", "arbitrary"),
        name="sb_attention",
    )(qkv, qkv, qkv)


def _online_step(s, v, m, l, acc):
    m_new = jnp.maximum(m, jnp.max(s, axis=-1, keepdims=True))
    alpha = jnp.exp(m - m_new)
    p = jnp.exp(s - m_new)
    l = alpha * l + jnp.sum(p, axis=-1, keepdims=True)
    acc = alpha * acc + _dot(p.astype(BF16), v)
    return m_new, l, acc


def _chunk_causal(tq):
    row = lax.broadcasted_iota(jnp.int32, (tq, tq), 0)
    col = lax.broadcasted_iota(jnp.int32, (tq, tq), 1)
    return row, col, (col // CHUNK) <= (row // CHUNK)


def _diff_kernel(q_ref, k_ref, v_ref, lam_ref, g_ref, o_ref, *, tq, scale, lam_init, g):
    hg = pl.program_id(1)
    qi = pl.program_id(2)
    d = HEAD_DIM
    dv = 2 * HEAD_DIM
    qk_cols = [slice(c * d, (c + 1) * d) for c in range(2 * g)]
    v_cols = [slice(hh * dv, (hh + 1) * dv) for hh in range(g)]
    qs = [q_ref[:, cs] for cs in qk_cols]
    row, col, visible = _chunk_causal(tq)
    dist = (row - col).astype(F32)
    nbias_diag, nbias_off, slopes = [], [], []
    for hh in range(g):
        slope = lax.bitcast_convert_type(
            jnp.full((1, 1), (126 - (hg * g + hh)) * (1 << 23), jnp.int32), F32)
        slopes.append(slope)
        nbias_diag.append(-slope * jnp.abs(dist))
        nbias_off.append(-slope * dist)

    def block(kb, carry, diag):
        start = pl.multiple_of(kb * tq, tq)
        ss = [_dot_nt(q, k_ref[pl.ds(start, tq), cs]) * scale for q, cs in zip(qs, qk_cols)]
        for c in range(2 * g):
            hh = c // 2
            if diag:
                ss[c] = jnp.where(visible, ss[c] + nbias_diag[hh], NEG)
            else:
                ss[c] = ss[c] + (nbias_off[hh] - slopes[hh] * ((qi - kb) * tq).astype(F32))
        return tuple(_online_step(ss[c], v_ref[pl.ds(start, tq), v_cols[c // 2]], *carry[c])
                     for c in range(2 * g))

    def body(kb, carry):
        return block(kb, carry, False)

    init = tuple((jnp.full((tq, 1), NEG, F32), jnp.zeros((tq, 1), F32), jnp.zeros((tq, dv), F32))
                 for _ in range(2 * g))
    carry = lax.fori_loop(0, qi, body, init)
    carry = block(qi, carry, True)

    lf = lam_ref[...]
    lam = (jnp.exp(jnp.sum(lf[0:1, :] * lf[1:2, :], axis=-1, keepdims=True))
           - jnp.exp(jnp.sum(lf[2:3, :] * lf[3:4, :], axis=-1, keepdims=True)) + lam_init)
    for hh in range(g):
        (_, l1, a1), (_, l2, a2) = carry[2 * hh], carry[2 * hh + 1]
        o = a1 / l1 - lam * (a2 / l2)
        o_ref[:, v_cols[hh]] = (_rms(o, g_ref[...]) * (1.0 - lam_init)).astype(o_ref.dtype)


def diff_attention(qkv, diff_lambda, subln_g, batch, seq, n_heads, q_col, k_col, v_col, lam_init,
                   g=DIFF_HEADS_PER_STEP):
    tq = ATTN_TILE
    nq = seq // tq
    dv = 2 * HEAD_DIM
    assert n_heads == 8, "ALiBi slopes are built as exact powers of two"
    assert n_heads % g == 0 and q_col % g == 0 and k_col % g == 0 and v_col % g == 0
    return pl.pallas_call(
        functools.partial(_diff_kernel, tq=tq, scale=1.0 / math.sqrt(HEAD_DIM), lam_init=lam_init, g=g),
        grid=(batch, n_heads // g, nq),
        in_specs=[
            pl.BlockSpec((tq, g * dv), lambda b, h, i: (b * nq + i, q_col // g + h)),
            pl.BlockSpec((seq, g * dv), lambda b, h, i: (b, k_col // g + h)),
            pl.BlockSpec((seq, g * dv), lambda b, h, i: (b, v_col // g + h)),
            pl.BlockSpec((4, HEAD_DIM), lambda b, h, i: (0, 0)),
            pl.BlockSpec((1, dv), lambda b, h, i: (0, 0)),
        ],
        out_specs=pl.BlockSpec((tq, g * dv), lambda b, h, i: (b * nq + i, h)),
        out_shape=jax.ShapeDtypeStruct((batch * seq, n_heads * dv), BF16),
        compiler_params=_params("parallel", "parallel", "arbitrary"),
        name="diff_attention",
    )(qkv, qkv, qkv, diff_lambda, subln_g.reshape(1, dv))


def _chunk_kernel(q_ref, k_ref, v_ref, tab_ref, o_ref, *, tq, scale, n_win, g):
    qi = pl.program_id(2)
    d = HEAD_DIM
    heads = [slice(hh * d, (hh + 1) * d) for hh in range(g)]
    kbs = [qi - (n_win - 1) + w for w in range(n_win)]
    starts = [pl.multiple_of(jnp.maximum(kb, 0) * tq, tq) for kb in kbs]
    scores = []
    for hh, cs in enumerate(heads):
        q = q_ref[:, cs]
        s_blocks = []
        for w in range(n_win):
            s = _dot_nt(q, k_ref[pl.ds(starts[w], tq), cs]) * scale + tab_ref[hh, :, w * tq:(w + 1) * tq]
            if w < n_win - 1:
                s = jnp.where(kbs[w] >= 0, s, NEG)
            s_blocks.append(s)
        scores.append(s_blocks)
    for cs, s_blocks in zip(heads, scores):
        m = s_blocks[0].max(axis=-1, keepdims=True)
        for s in s_blocks[1:]:
            m = jnp.maximum(m, s.max(axis=-1, keepdims=True))
        l = jnp.zeros_like(m)
        acc = jnp.zeros((tq, d), F32)
        for w, s in enumerate(s_blocks):
            p = jnp.exp(s - m)
            l = l + jnp.sum(p, axis=-1, keepdims=True)
            acc = acc + _dot(p.astype(BF16), v_ref[pl.ds(starts[w], tq), cs])
        o_ref[:, cs] = (acc / l).astype(o_ref.dtype)


def chunk_bias_table(rel_bias, tq):
    left = LEFT_CHUNKS * CHUNK
    n_heads = rel_bias.shape[0]
    width = tq + left
    period = 2 * left
    assert left >= REL_CLIP and tq <= left
    rb = rel_bias.astype(F32)
    hi = jnp.broadcast_to(rb[:, -1:], (n_heads, left - REL_CLIP))
    mid = rb[:, ::-1]
    lo = jnp.broadcast_to(rb[:, :1], (n_heads, tq - REL_CLIP - 1))
    neg_d = jnp.broadcast_to(rb[:, -1:], (n_heads, period - width))
    diag = jnp.concatenate([hi, mid, lo, neg_d], axis=1)
    flat = jnp.tile(diag, (1, tq))[:, :tq * (period - 1)]
    bias = flat.reshape(n_heads, tq, period - 1)[:, :, :width]
    i = jnp.arange(tq)[:, None]
    j = jnp.arange(width)[None, :]
    qc, kc = i // CHUNK, j // CHUNK
    allowed = (kc >= qc) & (kc <= qc + LEFT_CHUNKS)
    return jnp.where(allowed[None], bias, NEG)


def chunk_attention(qkv, rel_bias, batch, seq, n_heads, q_col, k_col, v_col, g=CHUNK_HEADS_PER_STEP):
    tq = ATTN_TILE
    nq = seq // tq
    d = HEAD_DIM
    left = LEFT_CHUNKS * CHUNK
    assert left % tq == 0
    assert n_heads % g == 0 and q_col % g == 0 and k_col % g == 0 and v_col % g == 0
    n_win = left // tq + 1
    table = chunk_bias_table(rel_bias, tq)
    return pl.pallas_call(
        functools.partial(_chunk_kernel, tq=tq, scale=1.0 / math.sqrt(d), n_win=n_win, g=g),
        grid=(n_heads // g, batch, nq),
        in_specs=[
            pl.BlockSpec((tq, g * d), lambda h, b, i: (b * nq + i, q_col // g + h)),
            pl.BlockSpec((seq, g * d), lambda h, b, i: (b, k_col // g + h)),
            pl.BlockSpec((seq, g * d), lambda h, b, i: (b, v_col // g + h)),
            pl.BlockSpec((g, tq, tq + left), lambda h, b, i: (h, 0, 0)),
        ],
        out_specs=pl.BlockSpec((tq, g * d), lambda h, b, i: (b * nq + i, h)),
        out_shape=jax.ShapeDtypeStruct((batch * seq, n_heads * d), BF16),
        compiler_params=_params("parallel", "parallel", "arbitrary"),
        name="chunk_attention",
    )(qkv, qkv, qkv, table)


def rope_tables(seq):
    half = QK_ROPE // 2
    pos = jnp.arange(seq, dtype=F32)
    inv_freq = ROPE_THETA ** (-jnp.arange(0, QK_ROPE, 2, dtype=F32) / QK_ROPE)
    ang = pos[:, None] * inv_freq[None, :]
    cos, sin = jnp.cos(ang), jnp.sin(ang)
    z = lambda n: jnp.zeros((seq, n), F32)
    cos_t = jnp.concatenate([cos, cos, z(LANES - 2 * half)], axis=1)
    sin_a = jnp.concatenate([-sin, z(LANES - half)], axis=1)
    sin_b = jnp.concatenate([z(half), sin, z(LANES - 2 * half)], axis=1)
    return cos_t, sin_a, sin_b


def _rope(x, cos_t, sin_a, sin_b):
    half = QK_ROPE // 2
    return x * cos_t + pltpu.roll(x, LANES - half, 1) * sin_a + pltpu.roll(x, half, 1) * sin_b


def _mla_q_kernel(c_ref, g_ref, w_ref, cos_ref, sa_ref, sb_ref, o_ref, an_ref, *, hp):
    @pl.when(pl.program_id(1) == 0)
    def _():
        an_ref[...] = _rms(c_ref[...], g_ref[...]).astype(an_ref.dtype)

    hw = QK_NOPE + LANES
    res = _dot(an_ref[...], w_ref[...])
    cos_t, sin_a, sin_b = cos_ref[...], sa_ref[...], sb_ref[...]
    for hh in range(hp):
        c0 = hh * hw
        o_ref[:, c0:c0 + QK_NOPE] = res[:, c0:c0 + QK_NOPE].astype(o_ref.dtype)
        o_ref[:, c0 + QK_NOPE:c0 + hw] = _rope(res[:, c0 + QK_NOPE:c0 + hw],
                                               cos_t, sin_a, sin_b).astype(o_ref.dtype)


def mla_q(lat, g, w_q, tables, seq, n_heads, tm=512, hp=MLA_PROJ_HEADS_PER_STEP):
    t = lat.shape[0]
    kq = g.shape[0]
    hw = QK_NOPE + LANES
    npos = seq // tm
    tab_spec = pl.BlockSpec((tm, LANES), lambda i, j: (i % npos, 0))
    return pl.pallas_call(
        functools.partial(_mla_q_kernel, hp=hp),
        grid=(t // tm, n_heads // hp),
        in_specs=[
            pl.BlockSpec((tm, kq), lambda i, j: (i, 0)),
            pl.BlockSpec((1, kq), lambda i, j: (0, 0)),
            pl.BlockSpec((kq, hp * hw), lambda i, j: (0, j)),
            tab_spec, tab_spec, tab_spec,
        ],
        out_specs=pl.BlockSpec((tm, hp * hw), lambda i, j: (i, j)),
        out_shape=jax.ShapeDtypeStruct((t, n_heads * hw), BF16),
        scratch_shapes=[pltpu.VMEM((tm, kq), BF16)],
        compiler_params=_params("parallel", "arbitrary"),
        name="mla_q",
    )(lat, g.reshape(1, kq), w_q, *tables)


def _mla_kv_kernel(c_ref, kr_ref, g_ref, w_ref, cos_ref, sa_ref, sb_ref, k_ref, v_ref,
                   an_ref, kr_scr, *, hp):
    @pl.when(pl.program_id(1) == 0)
    def _():
        an_ref[...] = _rms(c_ref[...], g_ref[...]).astype(an_ref.dtype)
        kr_scr[...] = _rope(kr_ref[...], cos_ref[...], sa_ref[...], sb_ref[...]).astype(kr_scr.dtype)

    hw = QK_NOPE + LANES
    wv = QK_NOPE + V_MLA
    res = _dot(an_ref[...], w_ref[...])
    kr = kr_scr[...]
    for hh in range(hp):
        k_ref[:, hh * hw:hh * hw + QK_NOPE] = res[:, hh * wv:hh * wv + QK_NOPE].astype(k_ref.dtype)
        k_ref[:, hh * hw + QK_NOPE:(hh + 1) * hw] = kr
        v_ref[:, hh * V_MLA:(hh + 1) * V_MLA] = res[:, hh * wv + QK_NOPE:(hh + 1) * wv].astype(v_ref.dtype)


def mla_kv(lat, g, w_kv, tables, seq, n_heads, ckv_col, kr_col, tm=512, hp=MLA_PROJ_HEADS_PER_STEP):
    t = lat.shape[0]
    kkv = g.shape[0]
    hw = QK_NOPE + LANES
    npos = seq // tm
    tab_spec = pl.BlockSpec((tm, LANES), lambda i, j: (i % npos, 0))
    return pl.pallas_call(
        functools.partial(_mla_kv_kernel, hp=hp),
        grid=(t // tm, n_heads // hp),
        in_specs=[
            pl.BlockSpec((tm, kkv), lambda i, j: (i, ckv_col)),
            pl.BlockSpec((tm, LANES), lambda i, j: (i, kr_col)),
            pl.BlockSpec((1, kkv), lambda i, j: (0, 0)),
            pl.BlockSpec((kkv, hp * (QK_NOPE + V_MLA)), lambda i, j: (0, j)),
            tab_spec, tab_spec, tab_spec,
        ],
        out_specs=[pl.BlockSpec((tm, hp * hw), lambda i, j: (i, j)),
                   pl.BlockSpec((tm, hp * V_MLA), lambda i, j: (i, j))],
        out_shape=[jax.ShapeDtypeStruct((t, n_heads * hw), BF16),
                   jax.ShapeDtypeStruct((t, n_heads * V_MLA), BF16)],
        scratch_shapes=[pltpu.VMEM((tm, kkv), BF16), pltpu.VMEM((tm, LANES), BF16)],
        compiler_params=_params("parallel", "arbitrary"),
        name="mla_kv",
    )(lat, lat, g.reshape(1, kkv), w_kv, *tables)


def _mla_attn_kernel(q_ref, k_ref, v_ref, o_ref, *, tq, scale, g):
    qi = pl.program_id(2)
    hw = QK_NOPE + LANES
    dv = V_MLA
    qk_cols = [slice(hh * hw, (hh + 1) * hw) for hh in range(g)]
    v_cols = [slice(hh * dv, (hh + 1) * dv) for hh in range(g)]
    qs = [q_ref[:, cs] for cs in qk_cols]
    _, _, visible = _chunk_causal(tq)

    def block(kb, carry, diag):
        start = pl.multiple_of(kb * tq, tq)
        ss = [_dot_nt(q, k_ref[pl.ds(start, tq), cs]) * scale for q, cs in zip(qs, qk_cols)]
        if diag:
            ss = [jnp.where(visible, s, NEG) for s in ss]
        return tuple(_online_step(s, v_ref[pl.ds(start, tq), vs], *st)
                     for s, vs, st in zip(ss, v_cols, carry))

    def body(kb, carry):
        return block(kb, carry, False)

    init = tuple((jnp.full((tq, 1), NEG, F32), jnp.zeros((tq, 1), F32), jnp.zeros((tq, dv), F32))
                 for _ in range(g))
    carry = lax.fori_loop(0, qi, body, init)
    carry = block(qi, carry, True)
    for vs, (_, l, acc) in zip(v_cols, carry):
        o_ref[:, vs] = (acc / l).astype(o_ref.dtype)


def mla_attention(q, k, v, batch, seq, n_heads, g=MLA_HEADS_PER_STEP):
    tq = ATTN_TILE
    nq = seq // tq
    hw = QK_NOPE + LANES
    assert n_heads % g == 0
    return pl.pallas_call(
        functools.partial(_mla_attn_kernel, tq=tq, scale=1.0 / math.sqrt(QK_NOPE + QK_ROPE), g=g),
        grid=(batch, n_heads // g, nq),
        in_specs=[
            pl.BlockSpec((tq, g * hw), lambda b, h, i: (b * nq + i, h)),
            pl.BlockSpec((seq, g * hw), lambda b, h, i: (b, h)),
            pl.BlockSpec((seq, g * V_MLA), lambda b, h, i: (b, h)),
        ],
        out_specs=pl.BlockSpec((tq, g * V_MLA), lambda b, h, i: (b * nq + i, h)),
        out_shape=jax.ShapeDtypeStruct((batch * seq, n_heads * V_MLA), BF16),
        compiler_params=_params("parallel", "parallel", "arbitrary"),
        name="mla_attention",
    )(q, k, v)


def _even_mixer(hn, w_in, w_out, i, diff_lambda, subln_g, batch, seq, layer):
    d_model = hn.shape[1]
    w_sb = d_model // 2
    n_sb = w_sb // HEAD_DIM
    n_diff = w_sb // (2 * HEAD_DIM)
    cb = w_sb // LANES
    qkv = matmul([(hn, w_in, i, 0)], w_in.shape[2], BF16, tm=1024, tn=512, name="even_in_proj")
    a = sb_attention(qkv, batch, seq, n_sb, 0, cb, 2 * cb)
    lam_init = 0.8 - 0.6 * math.exp(-0.3 * layer)
    bo = diff_attention(qkv, diff_lambda, subln_g, batch, seq, n_diff,
                        3 * cb // 2, 4 * cb // 2, 5 * cb // 2, lam_init)
    return matmul([(a, w_out, i, 0), (bo, w_out, i, 1)], d_model, F32, tm=1024, tn=512,
                  name="mix_out_proj")


def _odd_mixer(hn, w_in, w_out, i, rel_bias, q_norm_g, w_uq, kv_norm_g, w_ukv, batch, seq):
    d_model = hn.shape[1]
    w_ch = d_model // 2
    n_ch = w_ch // HEAD_DIM
    n_mla = w_ch // HEAD_DIM
    cb = w_ch // LANES
    q_lora, kv_lora = q_norm_g.shape[0], kv_norm_g.shape[0]
    n_attn = 3 * w_ch
    n_lat = q_lora + kv_lora + QK_ROPE
    lat_pad = -n_lat % LANES
    w_lat = jnp.pad(w_in[i:i + 1, :, n_attn:], ((0, 0), (0, 0), (0, lat_pad))).astype(BF16)
    qkv = matmul([(hn, w_in, i, 0)], n_attn, BF16, tm=1024, tn=512, name="odd_in_proj")
    lat = matmul([(hn, w_lat, 0, 0)], n_lat + lat_pad, F32, tm=512, tn=n_lat + lat_pad,
                 name="odd_lat_proj")
    c = chunk_attention(qkv, rel_bias, batch, seq, n_ch, 0, cb, 2 * cb)

    tables = rope_tables(seq)
    hw = QK_NOPE + LANES
    wq = w_uq.reshape(q_lora, n_mla, QK_NOPE + QK_ROPE)
    wq = jnp.pad(wq, ((0, 0), (0, 0), (0, hw - QK_NOPE - QK_ROPE))).reshape(q_lora, n_mla * hw).astype(BF16)
    q = mla_q(lat, q_norm_g, wq, tables, seq, n_mla)
    assert q_lora % kv_lora == 0 and (q_lora + kv_lora) % LANES == 0
    k, v = mla_kv(lat, kv_norm_g, w_ukv.astype(BF16), tables, seq, n_mla,
                  q_lora // kv_lora, (q_lora + kv_lora) // LANES)
    dm = mla_attention(q, k, v, batch, seq, n_mla)
    return matmul([(c, w_out, i, 0), (dm, w_out, i, 1)], d_model, F32, tm=1024, tn=512,
                  name="mix_out_proj")


def _ffn(h, w_in, conv_w, conv_b, w_out, layer, seq):
    g = ffn_in(h, w_in, conv_w, conv_b, layer, seq)
    return matmul([(g, w_out.astype(BF16), layer, 0)], w_out.shape[2], F32, tm=512, tn=256,
                  name="ffn_out_proj")


def kernel(x, norm_g, even_w_in, even_w_out, diff_lambda, diff_subln_g, odd_w_in, odd_w_out,
           ch_rel_bias, mla_q_norm_g, mla_w_uq, mla_kv_norm_g, mla_w_ukv, ffn_w_in, ffn_conv_w,
           ffn_conv_b, ffn_w_out):
    batch, seq, d_model = x.shape
    depth = norm_g.shape[0]
    xf = x.reshape(batch * seq, d_model)
    hn = norm_cast(xf, norm_g[0, 0])
    for layer in range(depth):
        g = norm_g[layer]
        i = layer // 2
        if layer % 2 == 0:
            mix = _even_mixer(hn, even_w_in, even_w_out, i, diff_lambda[i], diff_subln_g[i],
                              batch, seq, layer)
        else:
            mix = _odd_mixer(hn, odd_w_in, odd_w_out, i, ch_rel_bias[i], mla_q_norm_g[i],
                             mla_w_uq[i], mla_kv_norm_g[i], mla_w_ukv[i], batch, seq)
        xf, h2 = resid_norm(xf, mix, g[1], g[2])
        f = _ffn(h2, ffn_w_in, ffn_conv_w, ffn_conv_b, ffn_w_out, layer, seq)
        if layer + 1 < depth:
            xf, hn = resid_norm(xf, f, g[3], norm_g[layer + 1, 0])
        else:
            xf = resid(xf, f, g[3])
    return xf.reshape(batch, seq, d_model)
```

```python
import functools
import math

import jax
import jax.numpy as jnp
from jax import lax
from jax.experimental import pallas as pl
from jax.experimental.pallas import tpu as pltpu

F32 = jnp.float32
BF16 = jnp.bfloat16

CHUNK = 64
HEAD_DIM = 128
LEFT_CHUNKS = 8
REL_CLIP = 128
QK_NOPE = 128
QK_ROPE = 64
V_MLA = 128
ROPE_THETA = 10000.0
CONV_W = 3
EPS = 1e-6
NEG = -1e30
LOG2E = math.log2(math.e)

LANES = 128
BF16_SUBLANES = 16
VMEM_LIMIT = 52 * 1024 * 1024

ATTN_TILE = 256
SB_HEADS_PER_STEP = 8
MLA_HEADS_PER_STEP = 4
DIFF_HEADS_PER_STEP = 2
CHUNK_HEADS_PER_STEP = 4
MLA_PROJ_HEADS_PER_STEP = 4


def _params(*sem):
    return pltpu.CompilerParams(dimension_semantics=sem, vmem_limit_bytes=VMEM_LIMIT)


def _rms(x, g):
    ms = jnp.mean(x * x, axis=-1, keepdims=True)
    return x * lax.rsqrt(ms + EPS) * g


def _dot(a, b):
    return jnp.dot(a, b, preferred_element_type=F32)


def _dot_nt(a, b):
    return lax.dot_general(a, b, (((1,), (1,)), ((), ())), preferred_element_type=F32)


def _norm_cast_kernel(x_ref, g_ref, h_ref):
    h_ref[...] = _rms(x_ref[...], g_ref[...]).astype(h_ref.dtype)


def _resid_norm_kernel(x_ref, y_ref, g1_ref, g2_ref, xo_ref, h_ref):
    xn = x_ref[...] + _rms(y_ref[...], g1_ref[...])
    xo_ref[...] = xn
    h_ref[...] = _rms(xn, g2_ref[...]).astype(h_ref.dtype)


def _resid_kernel(x_ref, y_ref, g_ref, xo_ref):
    xo_ref[...] = x_ref[...] + _rms(y_ref[...], g_ref[...])


def _row_spec(tr, d):
    return pl.BlockSpec((tr, d), lambda i: (i, 0))


def _vec_spec(d):
    return pl.BlockSpec((1, d), lambda i: (0, 0))


def norm_cast(x, g, tr=256):
    t, d = x.shape
    return pl.pallas_call(
        _norm_cast_kernel,
        grid=(t // tr,),
        in_specs=[_row_spec(tr, d), _vec_spec(d)],
        out_specs=_row_spec(tr, d),
        out_shape=jax.ShapeDtypeStruct((t, d), BF16),
        compiler_params=_params("parallel"),
        name="norm_cast",
    )(x, g.reshape(1, d))


def resid_norm(x, y, g1, g2, tr=256):
    t, d = x.shape
    return pl.pallas_call(
        _resid_norm_kernel,
        grid=(t // tr,),
        in_specs=[_row_spec(tr, d), _row_spec(tr, d), _vec_spec(d), _vec_spec(d)],
        out_specs=[_row_spec(tr, d), _row_spec(tr, d)],
        out_shape=[jax.ShapeDtypeStruct((t, d), F32), jax.ShapeDtypeStruct((t, d), BF16)],
        compiler_params=_params("parallel"),
        name="resid_norm",
    )(x, y, g1.reshape(1, d), g2.reshape(1, d))


def resid(x, y, g, tr=256):
    t, d = x.shape
    return pl.pallas_call(
        _resid_kernel,
        grid=(t // tr,),
        in_specs=[_row_spec(tr, d), _row_spec(tr, d), _vec_spec(d)],
        out_specs=_row_spec(tr, d),
        out_shape=jax.ShapeDtypeStruct((t, d), F32),
        compiler_params=_params("parallel"),
        name="resid",
    )(x, y, g.reshape(1, d))


def _matmul_kernel(*refs, n_pairs):
    o_ref = refs[2 * n_pairs]
    acc = _dot(refs[0][...], refs[n_pairs][...].astype(BF16))
    for p in range(1, n_pairs):
        acc = acc + _dot(refs[p][...], refs[n_pairs + p][...].astype(BF16))
    o_ref[...] = acc.astype(o_ref.dtype)


def matmul(pairs, n, out_dtype, tm, tn, name):
    m = pairs[0][0].shape[0]
    n_pairs = len(pairs)
    a_specs, w_specs, args_a, args_w = [], [], [], []
    for a, w, layer, rb in pairs:
        k = a.shape[1]
        a_specs.append(pl.BlockSpec((tm, k), lambda i, j: (i, 0)))
        w_specs.append(pl.BlockSpec((None, k, tn), lambda i, j, layer=layer, rb=rb: (layer, rb, j)))
        args_a.append(a)
        args_w.append(w)
    return pl.pallas_call(
        functools.partial(_matmul_kernel, n_pairs=n_pairs),
        grid=(m // tm, n // tn),
        in_specs=a_specs + w_specs,
        out_specs=pl.BlockSpec((tm, tn), lambda i, j: (i, j)),
        out_shape=jax.ShapeDtypeStruct((m, n), out_dtype),
        compiler_params=_params("parallel", "arbitrary"),
        name=name,
    )(*args_a, *args_w)


def _gelu_tanh(x):
    c = math.sqrt(2.0 / math.pi)
    return x * (0.5 * (1.0 + jnp.tanh(c * (x + 0.044715 * (x * x * x)))))


def _ffn_in_kernel(a_ref, ah_ref, wg_ref, wv_ref, cwg_ref, cwv_ref, cbg_ref, cbv_ref,
                   o_ref, *, tm, seq):
    i = pl.program_id(0)
    tn = o_ref.shape[1]
    a = a_ref[...]
    ah = ah_ref[...]
    seq_start = (i * tm) % seq == 0
    row8 = lax.broadcasted_iota(jnp.int32, (8, tn), 0)

    def conv(w_ref, cw_ref, cb_ref):
        w = w_ref[...].astype(BF16)
        u = _dot(a, w)
        uh = _dot(ah, w)
        uh = jnp.where(seq_start, 0.0, uh)
        p1 = uh[BF16_SUBLANES - 1:BF16_SUBLANES, :]
        p2 = uh[BF16_SUBLANES - 2:BF16_SUBLANES - 1, :]
        r1 = pltpu.roll(u, 1, 0)
        r2 = pltpu.roll(u, 2, 0)
        h1 = jnp.where(row8 == 0, p1, r1[:8, :])
        h2 = jnp.where(row8 == 0, p2, jnp.where(row8 == 1, p1, r2[:8, :]))
        u1 = jnp.concatenate([h1, r1[8:, :]], axis=0)
        u2 = jnp.concatenate([h2, r2[8:, :]], axis=0)
        cw = cw_ref[...]
        return cb_ref[...] + (cw[0:1, :] * u2 + cw[1:2, :] * u1 + cw[2:3, :] * u)

    gate = conv(wg_ref, cwg_ref, cbg_ref)
    val = conv(wv_ref, cwv_ref, cbv_ref)
    o_ref[...] = (_gelu_tanh(gate) * val).astype(o_ref.dtype)


def ffn_in(h, w_in, conv_w, conv_b, layer, seq, tm=1024, tn=256):
    t, k = h.shape
    f = w_in.shape[2] // 2
    nf = f // tn
    halo = BF16_SUBLANES
    hb = tm // halo
    conv_b = conv_b.reshape(conv_b.shape[0], 1, 2 * f)
    return pl.pallas_call(
        functools.partial(_ffn_in_kernel, tm=tm, seq=seq),
        grid=(t // tm, nf),
        in_specs=[
            pl.BlockSpec((tm, k), lambda i, j: (i, 0)),
            pl.BlockSpec((halo, k), lambda i, j: (jnp.maximum(i * hb - 1, 0), 0)),
            pl.BlockSpec((None, k, tn), lambda i, j: (layer, 0, j)),
            pl.BlockSpec((None, k, tn), lambda i, j: (layer, 0, j + nf)),
            pl.BlockSpec((None, CONV_W, tn), lambda i, j: (layer, 0, j)),
            pl.BlockSpec((None, CONV_W, tn), lambda i, j: (layer, 0, j + nf)),
            pl.BlockSpec((None, 1, tn), lambda i, j: (layer, 0, j)),
            pl.BlockSpec((None, 1, tn), lambda i, j: (layer, 0, j + nf)),
        ],
        out_specs=pl.BlockSpec((tm, tn), lambda i, j: (i, j)),
        out_shape=jax.ShapeDtypeStruct((t, f), BF16),
        compiler_params=_params("parallel", "arbitrary"),
        name="ffn_in",
    )(h, h, w_in, w_in, conv_w, conv_w, conv_b, conv_b)


def _sb_kernel(q_ref, k_ref, v_ref, o_ref, *, tq, scale, g):
    qi = pl.program_id(2)
    d = HEAD_DIM
    heads = [slice(hh * d, (hh + 1) * d) for hh in range(g)]
    qs = [q_ref[:, cs] for cs in heads]
    row = lax.broadcasted_iota(jnp.int32, (tq, tq), 0)
    col = lax.broadcasted_iota(jnp.int32, (tq, tq), 1)
    strict = col < row
    r2 = lax.broadcasted_iota(jnp.int32, (2 * tq, tq), 0)
    c2 = lax.broadcasted_iota(jnp.int32, (2 * tq, tq), 1)
    tri = (jnp.where(r2 >= tq, r2 - tq, r2) >= c2).astype(BF16)

    def block(kb, carry, diag):
        start = pl.multiple_of(kb * tq, tq)
        zs = [_dot_nt(q, k_ref[pl.ds(start, tq), cs]) * (scale * LOG2E) for q, cs in zip(qs, heads)]
        csums = []
        for z in zs:
            neg_abs = lax.bitcast_convert_type(
                lax.bitcast_convert_type(z, jnp.uint32) | jnp.uint32(0x80000000), F32)
            sp = jnp.maximum(z, 0.0) + jnp.log2(1.0 + jnp.exp2(neg_abs))
            if diag:
                sp = jnp.where(strict, sp, 0.0)
            hi = sp.astype(BF16)
            lo = (sp - hi.astype(F32)).astype(BF16)
            csums.append(_dot(jnp.concatenate([hi, lo], axis=1), tri))
        out = []
        for z, csum, cs, (rsum, acc) in zip(zs, csums, heads, carry):
            w = jnp.exp2(z - csum - rsum)
            if diag:
                w = jnp.where(strict, w, 0.0)
            acc = acc + _dot(w.astype(BF16), v_ref[pl.ds(start, tq), cs])
            rsum = rsum + csum[:, 0:1]
            out.append((rsum, acc))
        return tuple(out)

    init = tuple((jnp.zeros((tq, 1), F32), jnp.zeros((tq, d), F32)) for _ in range(g))
    carry = block(qi, init, True)

    def body(it, carry):
        return block(qi - 1 - it, carry, False)

    carry = lax.fori_loop(0, qi, body, carry)
    for cs, (_, acc) in zip(heads, carry):
        o_ref[:, cs] = acc.astype(o_ref.dtype)


def sb_attention(qkv, batch, seq, n_heads, q_col, k_col, v_col, g=SB_HEADS_PER_STEP):
    tq = ATTN_TILE
    nq = seq // tq
    d = HEAD_DIM
    assert n_heads % g == 0 and q_col % g == 0 and k_col % g == 0 and v_col % g == 0
    return pl.pallas_call(
        functools.partial(_sb_kernel, tq=tq, scale=1.0 / math.sqrt(d), g=g),
        grid=(batch, n_heads // g, nq),
        in_specs=[
            pl.BlockSpec((tq, g * d), lambda b, h, i: (b * nq + i, q_col // g + h)),
            pl.BlockSpec((seq, g * d), lambda b, h, i: (b, k_col // g + h)),
            pl.BlockSpec((seq, g * d), lambda b, h, i: (b, v_col // g + h)),
        ],
        out_specs=pl.BlockSpec((tq, g * d), lambda b, h, i: (b * nq + i, h)),
        out_shape=jax.ShapeDtypeStruct((batch * seq, n_heads * d), BF16),
        compiler_params=_params("parallel", "parallel", "arbitrary"),
        name="sb_attention",
    )(qkv, qkv, qkv)


def _online_step(s, v, m, l, acc):
    m_new = jnp.maximum(m, jnp.max(s, axis=-1, keepdims=True))
    alpha = jnp.exp2(m - m_new)
    p = jnp.exp2(s - m_new)
    l = alpha * l + jnp.sum(p, axis=-1, keepdims=True)
    acc = alpha * acc + _dot(p.astype(BF16), v)
    return m_new, l, acc


def _chunk_causal(tq):
    row = lax.broadcasted_iota(jnp.int32, (tq, tq), 0)
    col = lax.broadcasted_iota(jnp.int32, (tq, tq), 1)
    return row, col, (col // CHUNK) <= (row // CHUNK)


def _diff_kernel(q_ref, k_ref, v_ref, lam_ref, g_ref, o_ref, *, tq, scale, lam_init, g):
    hg = pl.program_id(1)
    qi = pl.program_id(2)
    d = HEAD_DIM
    dv = 2 * HEAD_DIM
    qk_cols = [slice(c * d, (c + 1) * d) for c in range(2 * g)]
    v_cols = [slice(hh * dv, (hh + 1) * dv) for hh in range(g)]
    qs = [q_ref[:, cs] for cs in qk_cols]
    row, col, visible = _chunk_causal(tq)
    dist = (row - col).astype(F32)
    nbias_diag, nbias_off, slopes = [], [], []
    for hh in range(g):
        slope = lax.bitcast_convert_type(
            jnp.full((1, 1), (126 - (hg * g + hh)) * (1 << 23), jnp.int32), F32)
        slope = slope * LOG2E
        slopes.append(slope)
        nbias_diag.append(-slope * jnp.abs(dist))
        nbias_off.append(-slope * dist)

    def block(kb, carry, diag):
        start = pl.multiple_of(kb * tq, tq)
        ss = [_dot_nt(q, k_ref[pl.ds(start, tq), cs]) * (scale * LOG2E) for q, cs in zip(qs, qk_cols)]
        for c in range(2 * g):
            hh = c // 2
            if diag:
                ss[c] = jnp.where(visible, ss[c] + nbias_diag[hh], NEG)
            else:
                ss[c] = ss[c] + (nbias_off[hh] - slopes[hh] * ((qi - kb) * tq).astype(F32))
        return tuple(_online_step(ss[c], v_ref[pl.ds(start, tq), v_cols[c // 2]], *carry[c])
                     for c in range(2 * g))

    def body(kb, carry):
        return block(kb, carry, False)

    init = tuple((jnp.full((tq, 1), NEG, F32), jnp.zeros((tq, 1), F32), jnp.zeros((tq, dv), F32))
                 for _ in range(2 * g))
    carry = lax.fori_loop(0, qi, body, init)
    carry = block(qi, carry, True)

    lf = lam_ref[...]
    lam = (jnp.exp(jnp.sum(lf[0:1, :] * lf[1:2, :], axis=-1, keepdims=True))
           - jnp.exp(jnp.sum(lf[2:3, :] * lf[3:4, :], axis=-1, keepdims=True)) + lam_init)
    for hh in range(g):
        (_, l1, a1), (_, l2, a2) = carry[2 * hh], carry[2 * hh + 1]
        o = a1 / l1 - lam * (a2 / l2)
        o_ref[:, v_cols[hh]] = (_rms(o, g_ref[...]) * (1.0 - lam_init)).astype(o_ref.dtype)


def diff_attention(qkv, diff_lambda, subln_g, batch, seq, n_heads, q_col, k_col, v_col, lam_init,
                   g=DIFF_HEADS_PER_STEP):
    tq = ATTN_TILE
    nq = seq // tq
    dv = 2 * HEAD_DIM
    assert n_heads == 8, "ALiBi slopes are built as exact powers of two"
    assert n_heads % g == 0 and q_col % g == 0 and k_col % g == 0 and v_col % g == 0
    return pl.pallas_call(
        functools.partial(_diff_kernel, tq=tq, scale=1.0 / math.sqrt(HEAD_DIM), lam_init=lam_init, g=g),
        grid=(batch, n_heads // g, nq),
        in_specs=[
            pl.BlockSpec((tq, g * dv), lambda b, h, i: (b * nq + i, q_col // g + h)),
            pl.BlockSpec((seq, g * dv), lambda b, h, i: (b, k_col // g + h)),
            pl.BlockSpec((seq, g * dv), lambda b, h, i: (b, v_col // g + h)),
            pl.BlockSpec((4, HEAD_DIM), lambda b, h, i: (0, 0)),
            pl.BlockSpec((1, dv), lambda b, h, i: (0, 0)),
        ],
        out_specs=pl.BlockSpec((tq, g * dv), lambda b, h, i: (b * nq + i, h)),
        out_shape=jax.ShapeDtypeStruct((batch * seq, n_heads * dv), BF16),
        compiler_params=_params("parallel", "parallel", "arbitrary"),
        name="diff_attention",
    )(qkv, qkv, qkv, diff_lambda, subln_g.reshape(1, dv))


def _chunk_kernel(q_ref, k_ref, v_ref, tab_ref, o_ref, *, tq, scale, n_win, g):
    qi = pl.program_id(2)
    d = HEAD_DIM
    heads = [slice(hh * d, (hh + 1) * d) for hh in range(g)]
    kbs = [qi - (n_win - 1) + w for w in range(n_win)]
    starts = [pl.multiple_of(jnp.maximum(kb, 0) * tq, tq) for kb in kbs]
    scores = []
    for hh, cs in enumerate(heads):
        q = q_ref[:, cs]
        s_blocks = []
        for w in range(n_win):
            s = (_dot_nt(q, k_ref[pl.ds(starts[w], tq), cs]) * (scale * LOG2E)
                 + tab_ref[hh, :, w * tq:(w + 1) * tq])
            if w < n_win - 1:
                s = jnp.where(kbs[w] >= 0, s, NEG)
            s_blocks.append(s)
        scores.append(s_blocks)
    for cs, s_blocks in zip(heads, scores):
        m = s_blocks[0].max(axis=-1, keepdims=True)
        for s in s_blocks[1:]:
            m = jnp.maximum(m, s.max(axis=-1, keepdims=True))
        l = jnp.zeros_like(m)
        acc = jnp.zeros((tq, d), F32)
        for w, s in enumerate(s_blocks):
            p = jnp.exp2(s - m)
            l = l + jnp.sum(p, axis=-1, keepdims=True)
            acc = acc + _dot(p.astype(BF16), v_ref[pl.ds(starts[w], tq), cs])
        o_ref[:, cs] = (acc / l).astype(o_ref.dtype)


def chunk_bias_table(rel_bias, tq):
    left = LEFT_CHUNKS * CHUNK
    n_heads = rel_bias.shape[0]
    width = tq + left
    period = 2 * left
    assert left >= REL_CLIP and tq <= left
    rb = rel_bias.astype(F32)
    hi = jnp.broadcast_to(rb[:, -1:], (n_heads, left - REL_CLIP))
    mid = rb[:, ::-1]
    lo = jnp.broadcast_to(rb[:, :1], (n_heads, tq - REL_CLIP - 1))
    neg_d = jnp.broadcast_to(rb[:, -1:], (n_heads, period - width))
    diag = jnp.concatenate([hi, mid, lo, neg_d], axis=1)
    flat = jnp.tile(diag, (1, tq))[:, :tq * (period - 1)]
    bias = flat.reshape(n_heads, tq, period - 1)[:, :, :width]
    i = jnp.arange(tq)[:, None]
    j = jnp.arange(width)[None, :]
    qc, kc = i // CHUNK, j // CHUNK
    allowed = (kc >= qc) & (kc <= qc + LEFT_CHUNKS)
    return jnp.where(allowed[None], bias * LOG2E, NEG)


def chunk_attention(qkv, rel_bias, batch, seq, n_heads, q_col, k_col, v_col, g=CHUNK_HEADS_PER_STEP):
    tq = ATTN_TILE
    nq = seq // tq
    d = HEAD_DIM
    left = LEFT_CHUNKS * CHUNK
    assert left % tq == 0
    assert n_heads % g == 0 and q_col % g == 0 and k_col % g == 0 and v_col % g == 0
    n_win = left // tq + 1
    table = chunk_bias_table(rel_bias, tq)
    return pl.pallas_call(
        functools.partial(_chunk_kernel, tq=tq, scale=1.0 / math.sqrt(d), n_win=n_win, g=g),
        grid=(n_heads // g, batch, nq),
        in_specs=[
            pl.BlockSpec((tq, g * d), lambda h, b, i: (b * nq + i, q_col // g + h)),
            pl.BlockSpec((seq, g * d), lambda h, b, i: (b, k_col // g + h)),
            pl.BlockSpec((seq, g * d), lambda h, b, i: (b, v_col // g + h)),
            pl.BlockSpec((g, tq, tq + left), lambda h, b, i: (h, 0, 0)),
        ],
        out_specs=pl.BlockSpec((tq, g * d), lambda h, b, i: (b * nq + i, h)),
        out_shape=jax.ShapeDtypeStruct((batch * seq, n_heads * d), BF16),
        compiler_params=_params("parallel", "parallel", "arbitrary"),
        name="chunk_attention",
    )(qkv, qkv, qkv, table)


def rope_tables(seq):
    half = QK_ROPE // 2
    pos = jnp.arange(seq, dtype=F32)
    inv_freq = ROPE_THETA ** (-jnp.arange(0, QK_ROPE, 2, dtype=F32) / QK_ROPE)
    ang = pos[:, None] * inv_freq[None, :]
    cos, sin = jnp.cos(ang), jnp.sin(ang)
    z = lambda n: jnp.zeros((seq, n), F32)
    cos_t = jnp.concatenate([cos, cos, z(LANES - 2 * half)], axis=1)
    sin_a = jnp.concatenate([-sin, z(LANES - half)], axis=1)
    sin_b = jnp.concatenate([z(half), sin, z(LANES - 2 * half)], axis=1)
    return cos_t, sin_a, sin_b


def _rope(x, cos_t, sin_a, sin_b):
    half = QK_ROPE // 2
    return x * cos_t + pltpu.roll(x, LANES - half, 1) * sin_a + pltpu.roll(x, half, 1) * sin_b


def _mla_q_kernel(c_ref, g_ref, w_ref, cos_ref, sa_ref, sb_ref, o_ref, an_ref, *, hp):
    @pl.when(pl.program_id(1) == 0)
    def _():
        an_ref[...] = _rms(c_ref[...], g_ref[...]).astype(an_ref.dtype)

    hw = QK_NOPE + LANES
    res = _dot(an_ref[...], w_ref[...])
    cos_t, sin_a, sin_b = cos_ref[...], sa_ref[...], sb_ref[...]
    for hh in range(hp):
        c0 = hh * hw
        o_ref[:, c0:c0 + QK_NOPE] = res[:, c0:c0 + QK_NOPE].astype(o_ref.dtype)
        o_ref[:, c0 + QK_NOPE:c0 + hw] = _rope(res[:, c0 + QK_NOPE:c0 + hw],
                                               cos_t, sin_a, sin_b).astype(o_ref.dtype)


def mla_q(lat, g, w_q, tables, seq, n_heads, tm=512, hp=MLA_PROJ_HEADS_PER_STEP):
    t = lat.shape[0]
    kq = g.shape[0]
    hw = QK_NOPE + LANES
    npos = seq // tm
    tab_spec = pl.BlockSpec((tm, LANES), lambda i, j: (i % npos, 0))
    return pl.pallas_call(
        functools.partial(_mla_q_kernel, hp=hp),
        grid=(t // tm, n_heads // hp),
        in_specs=[
            pl.BlockSpec((tm, kq), lambda i, j: (i, 0)),
            pl.BlockSpec((1, kq), lambda i, j: (0, 0)),
            pl.BlockSpec((kq, hp * hw), lambda i, j: (0, j)),
            tab_spec, tab_spec, tab_spec,
        ],
        out_specs=pl.BlockSpec((tm, hp * hw), lambda i, j: (i, j)),
        out_shape=jax.ShapeDtypeStruct((t, n_heads * hw), BF16),
        scratch_shapes=[pltpu.VMEM((tm, kq), BF16)],
        compiler_params=_params("parallel", "arbitrary"),
        name="mla_q",
    )(lat, g.reshape(1, kq), w_q, *tables)


def _mla_kv_kernel(c_ref, kr_ref, g_ref, w_ref, cos_ref, sa_ref, sb_ref, k_ref, v_ref,
                   an_ref, kr_scr, *, hp):
    @pl.when(pl.program_id(1) == 0)
    def _():
        an_ref[...] = _rms(c_ref[...], g_ref[...]).astype(an_ref.dtype)
        kr_scr[...] = _rope(kr_ref[...], cos_ref[...], sa_ref[...], sb_ref[...]).astype(kr_scr.dtype)

    hw = QK_NOPE + LANES
    wv = QK_NOPE + V_MLA
    res = _dot(an_ref[...], w_ref[...])
    kr = kr_scr[...]
    for hh in range(hp):
        k_ref[:, hh * hw:hh * hw + QK_NOPE] = res[:, hh * wv:hh * wv + QK_NOPE].astype(k_ref.dtype)
        k_ref[:, hh * hw + QK_NOPE:(hh + 1) * hw] = kr
        v_ref[:, hh * V_MLA:(hh + 1) * V_MLA] = res[:, hh * wv + QK_NOPE:(hh + 1) * wv].astype(v_ref.dtype)


def mla_kv(lat, g, w_kv, tables, seq, n_heads, ckv_col, kr_col, tm=512, hp=MLA_PROJ_HEADS_PER_STEP):
    t = lat.shape[0]
    kkv = g.shape[0]
    hw = QK_NOPE + LANES
    npos = seq // tm
    tab_spec = pl.BlockSpec((tm, LANES), lambda i, j: (i % npos, 0))
    return pl.pallas_call(
        functools.partial(_mla_kv_kernel, hp=hp),
        grid=(t // tm, n_heads // hp),
        in_specs=[
            pl.BlockSpec((tm, kkv), lambda i, j: (i, ckv_col)),
            pl.BlockSpec((tm, LANES), lambda i, j: (i, kr_col)),
            pl.BlockSpec((1, kkv), lambda i, j: (0, 0)),
            pl.BlockSpec((kkv, hp * (QK_NOPE + V_MLA)), lambda i, j: (0, j)),
            tab_spec, tab_spec, tab_spec,
        ],
        out_specs=[pl.BlockSpec((tm, hp * hw), lambda i, j: (i, j)),
                   pl.BlockSpec((tm, hp * V_MLA), lambda i, j: (i, j))],
        out_shape=[jax.ShapeDtypeStruct((t, n_heads * hw), BF16),
                   jax.ShapeDtypeStruct((t, n_heads * V_MLA), BF16)],
        scratch_shapes=[pltpu.VMEM((tm, kkv), BF16), pltpu.VMEM((tm, LANES), BF16)],
        compiler_params=_params("parallel", "arbitrary"),
        name="mla_kv",
    )(lat, lat, g.reshape(1, kkv), w_kv, *tables)


def _mla_attn_kernel(q_ref, k_ref, v_ref, o_ref, *, tq, scale, g):
    qi = pl.program_id(2)
    hw = QK_NOPE + LANES
    dv = V_MLA
    qk_cols = [slice(hh * hw, (hh + 1) * hw) for hh in range(g)]
    v_cols = [slice(hh * dv, (hh + 1) * dv) for hh in range(g)]
    qs = [q_ref[:, cs] for cs in qk_cols]
    _, _, visible = _chunk_causal(tq)

    def block(kb, carry, diag):
        start = pl.multiple_of(kb * tq, tq)
        ss = [_dot_nt(q, k_ref[pl.ds(start, tq), cs]) * (scale * LOG2E) for q, cs in zip(qs, qk_cols)]
        if diag:
            ss = [jnp.where(visible, s, NEG) for s in ss]
        return tuple(_online_step(s, v_ref[pl.ds(start, tq), vs], *st)
                     for s, vs, st in zip(ss, v_cols, carry))

    def body(kb, carry):
        return block(kb, carry, False)

    init = tuple((jnp.full((tq, 1), NEG, F32), jnp.zeros((tq, 1), F32), jnp.zeros((tq, dv), F32))
                 for _ in range(g))
    carry = lax.fori_loop(0, qi, body, init)
    carry = block(qi, carry, True)
    for vs, (_, l, acc) in zip(v_cols, carry):
        o_ref[:, vs] = (acc / l).astype(o_ref.dtype)


def mla_attention(q, k, v, batch, seq, n_heads, g=MLA_HEADS_PER_STEP):
    tq = ATTN_TILE
    nq = seq // tq
    hw = QK_NOPE + LANES
    assert n_heads % g == 0
    return pl.pallas_call(
        functools.partial(_mla_attn_kernel, tq=tq, scale=1.0 / math.sqrt(QK_NOPE + QK_ROPE), g=g),
        grid=(batch, n_heads // g, nq),
        in_specs=[
            pl.BlockSpec((tq, g * hw), lambda b, h, i: (b * nq + i, h)),
            pl.BlockSpec((seq, g * hw), lambda b, h, i: (b, h)),
            pl.BlockSpec((seq, g * V_MLA), lambda b, h, i: (b, h)),
        ],
        out_specs=pl.BlockSpec((tq, g * V_MLA), lambda b, h, i: (b * nq + i, h)),
        out_shape=jax.ShapeDtypeStruct((batch * seq, n_heads * V_MLA), BF16),
        compiler_params=_params("parallel", "parallel", "arbitrary"),
        name="mla_attention",
    )(q, k, v)


def _even_mixer(hn, w_in, w_out, i, diff_lambda, subln_g, batch, seq, layer):
    d_model = hn.shape[1]
    w_sb = d_model // 2
    n_sb = w_sb // HEAD_DIM
    n_diff = w_sb // (2 * HEAD_DIM)
    cb = w_sb // LANES
    qkv = matmul([(hn, w_in, i, 0)], w_in.shape[2], BF16, tm=1024, tn=512, name="even_in_proj")
    a = sb_attention(qkv, batch, seq, n_sb, 0, cb, 2 * cb)
    lam_init = 0.8 - 0.6 * math.exp(-0.3 * layer)
    bo = diff_attention(qkv, diff_lambda, subln_g, batch, seq, n_diff,
                        3 * cb // 2, 4 * cb // 2, 5 * cb // 2, lam_init)
    return matmul([(a, w_out, i, 0), (bo, w_out, i, 1)], d_model, F32, tm=1024, tn=512,
                  name="mix_out_proj")


def _odd_mixer(hn, w_in, w_out, i, rel_bias, q_norm_g, w_uq, kv_norm_g, w_ukv, batch, seq):
    d_model = hn.shape[1]
    w_ch = d_model // 2
    n_ch = w_ch // HEAD_DIM
    n_mla = w_ch // HEAD_DIM
    cb = w_ch // LANES
    q_lora, kv_lora = q_norm_g.shape[0], kv_norm_g.shape[0]
    n_attn = 3 * w_ch
    n_lat = q_lora + kv_lora + QK_ROPE
    lat_pad = -n_lat % LANES
    w_lat = jnp.pad(w_in[i:i + 1, :, n_attn:], ((0, 0), (0, 0), (0, lat_pad))).astype(BF16)
    qkv = matmul([(hn, w_in, i, 0)], n_attn, BF16, tm=1024, tn=512, name="odd_in_proj")
    lat = matmul([(hn, w_lat, 0, 0)], n_lat + lat_pad, F32, tm=512, tn=n_lat + lat_pad,
                 name="odd_lat_proj")
    c = chunk_attention(qkv, rel_bias, batch, seq, n_ch, 0, cb, 2 * cb)

    tables = rope_tables(seq)
    hw = QK_NOPE + LANES
    wq = w_uq.reshape(q_lora, n_mla, QK_NOPE + QK_ROPE)
    wq = jnp.pad(wq, ((0, 0), (0, 0), (0, hw - QK_NOPE - QK_ROPE))).reshape(q_lora, n_mla * hw).astype(BF16)
    q = mla_q(lat, q_norm_g, wq, tables, seq, n_mla)
    assert q_lora % kv_lora == 0 and (q_lora + kv_lora) % LANES == 0
    k, v = mla_kv(lat, kv_norm_g, w_ukv.astype(BF16), tables, seq, n_mla,
                  q_lora // kv_lora, (q_lora + kv_lora) // LANES)
    dm = mla_attention(q, k, v, batch, seq, n_mla)
    return matmul([(c, w_out, i, 0), (dm, w_out, i, 1)], d_model, F32, tm=1024, tn=512,
                  name="mix_out_proj")


def _ffn(h, w_in, conv_w, conv_b, w_out, layer, seq):
    g = ffn_in(h, w_in, conv_w, conv_b, layer, seq)
    return matmul([(g, w_out.astype(BF16), layer, 0)], w_out.shape[2], F32, tm=512, tn=256,
                  name="ffn_out_proj")


def kernel(x, norm_g, even_w_in, even_w_out, diff_lambda, diff_subln_g, odd_w_in, odd_w_out,
           ch_rel_bias, mla_q_norm_g, mla_w_uq, mla_kv_norm_g, mla_w_ukv, ffn_w_in, ffn_conv_w,
           ffn_conv_b, ffn_w_out):
    batch, seq, d_model = x.shape
    depth = norm_g.shape[0]
    xf = x.reshape(batch * seq, d_model)
    hn = norm_cast(xf, norm_g[0, 0])
    for layer in range(depth):
        g = norm_g[layer]
        i = layer // 2
        if layer % 2 == 0:
            mix = _even_mixer(hn, even_w_in, even_w_out, i, diff_lambda[i], diff_subln_g[i],
                              batch, seq, layer)
        else:
            mix = _odd_mixer(hn, odd_w_in, odd_w_out, i, ch_rel_bias[i], mla_q_norm_g[i],
                             mla_w_uq[i], mla_kv_norm_g[i], mla_w_ukv[i], batch, seq)
        xf, h2 = resid_norm(xf, mix, g[1], g[2])
        f = _ffn(h2, ffn_w_in, ffn_conv_w, ffn_conv_b, ffn_w_out, layer, seq)
        if layer + 1 < depth:
            xf, hn = resid_norm(xf, f, g[3], norm_g[layer + 1, 0])
        else:
            xf = resid(xf, f, g[3])
    return xf.reshape(batch, seq, d_model)
```

```python
import functools
import math

import jax
import jax.numpy as jnp
from jax import lax
from jax.experimental import pallas as pl
from jax.experimental.pallas import tpu as pltpu

F32 = jnp.float32
BF16 = jnp.bfloat16

CHUNK = 64
HEAD_DIM = 128
LEFT_CHUNKS = 8
REL_CLIP = 128
QK_NOPE = 128
QK_ROPE = 64
V_MLA = 128
ROPE_THETA = 10000.0
CONV_W = 3
EPS = 1e-6
NEG = -1e30
LOG2E = math.log2(math.e)

LANES = 128
BF16_SUBLANES = 16
VMEM_LIMIT = 52 * 1024 * 1024

ATTN_TILE = 256
SB_HEADS_PER_STEP = 8
MLA_HEADS_PER_STEP = 8
DIFF_HEADS_PER_STEP = 4
CHUNK_HEADS_PER_STEP = 4
MLA_PROJ_HEADS_PER_STEP = 4


def _params(*sem):
    return pltpu.CompilerParams(dimension_semantics=sem, vmem_limit_bytes=VMEM_LIMIT)


def _rms(x, g):
    ms = jnp.mean(x * x, axis=-1, keepdims=True)
    return x * lax.rsqrt(ms + EPS) * g


def _dot(a, b):
    return jnp.dot(a, b, preferred_element_type=F32)


def _dot_nt(a, b):
    return lax.dot_general(a, b, (((1,), (1,)), ((), ())), preferred_element_type=F32)


def _norm_cast_kernel(x_ref, g_ref, h_ref):
    h_ref[...] = _rms(x_ref[...], g_ref[...]).astype(h_ref.dtype)


def _resid_norm_kernel(x_ref, y_ref, g1_ref, g2_ref, xo_ref, h_ref):
    xn = x_ref[...] + _rms(y_ref[...], g1_ref[...])
    xo_ref[...] = xn
    h_ref[...] = _rms(xn, g2_ref[...]).astype(h_ref.dtype)


def _resid_kernel(x_ref, y_ref, g_ref, xo_ref):
    xo_ref[...] = x_ref[...] + _rms(y_ref[...], g_ref[...])


def _row_spec(tr, d):
    return pl.BlockSpec((tr, d), lambda i: (i, 0))


def _vec_spec(d):
    return pl.BlockSpec((1, d), lambda i: (0, 0))


def norm_cast(x, g, tr=256):
    t, d = x.shape
    return pl.pallas_call(
        _norm_cast_kernel,
        grid=(t // tr,),
        in_specs=[_row_spec(tr, d), _vec_spec(d)],
        out_specs=_row_spec(tr, d),
        out_shape=jax.ShapeDtypeStruct((t, d), BF16),
        compiler_params=_params("parallel"),
        name="norm_cast",
    )(x, g.reshape(1, d))


def resid_norm(x, y, g1, g2, tr=256):
    t, d = x.shape
    return pl.pallas_call(
        _resid_norm_kernel,
        grid=(t // tr,),
        in_specs=[_row_spec(tr, d), _row_spec(tr, d), _vec_spec(d), _vec_spec(d)],
        out_specs=[_row_spec(tr, d), _row_spec(tr, d)],
        out_shape=[jax.ShapeDtypeStruct((t, d), F32), jax.ShapeDtypeStruct((t, d), BF16)],
        compiler_params=_params("parallel"),
        name="resid_norm",
    )(x, y, g1.reshape(1, d), g2.reshape(1, d))


def resid(x, y, g, tr=256):
    t, d = x.shape
    return pl.pallas_call(
        _resid_kernel,
        grid=(t // tr,),
        in_specs=[_row_spec(tr, d), _row_spec(tr, d), _vec_spec(d)],
        out_specs=_row_spec(tr, d),
        out_shape=jax.ShapeDtypeStruct((t, d), F32),
        compiler_params=_params("parallel"),
        name="resid",
    )(x, y, g.reshape(1, d))


def _matmul_kernel(*refs, n_pairs):
    o_ref = refs[2 * n_pairs]
    acc = _dot(refs[0][...], refs[n_pairs][...].astype(BF16))
    for p in range(1, n_pairs):
        acc = acc + _dot(refs[p][...], refs[n_pairs + p][...].astype(BF16))
    o_ref[...] = acc.astype(o_ref.dtype)


def matmul(pairs, n, out_dtype, tm, tn, name):
    m = pairs[0][0].shape[0]
    n_pairs = len(pairs)
    a_specs, w_specs, args_a, args_w = [], [], [], []
    for a, w, layer, rb in pairs:
        k = a.shape[1]
        a_specs.append(pl.BlockSpec((tm, k), lambda i, j: (i, 0)))
        w_specs.append(pl.BlockSpec((None, k, tn), lambda i, j, layer=layer, rb=rb: (layer, rb, j)))
        args_a.append(a)
        args_w.append(w)
    return pl.pallas_call(
        functools.partial(_matmul_kernel, n_pairs=n_pairs),
        grid=(m // tm, n // tn),
        in_specs=a_specs + w_specs,
        out_specs=pl.BlockSpec((tm, tn), lambda i, j: (i, j)),
        out_shape=jax.ShapeDtypeStruct((m, n), out_dtype),
        compiler_params=_params("parallel", "arbitrary"),
        name=name,
    )(*args_a, *args_w)


def _gelu_tanh(x):
    c = math.sqrt(2.0 / math.pi)
    return x * (0.5 * (1.0 + jnp.tanh(c * (x + 0.044715 * (x * x * x)))))


def _ffn_in_kernel(a_ref, ah_ref, wg_ref, wv_ref, cwg_ref, cwv_ref, cbg_ref, cbv_ref,
                   o_ref, *, tm, seq):
    i = pl.program_id(0)
    tn = o_ref.shape[1]
    a = a_ref[...]
    ah = ah_ref[...]
    seq_start = (i * tm) % seq == 0
    row8 = lax.broadcasted_iota(jnp.int32, (8, tn), 0)

    def conv(w_ref, cw_ref, cb_ref):
        w = w_ref[...].astype(BF16)
        u = _dot(a, w)
        uh = _dot(ah, w)
        uh = jnp.where(seq_start, 0.0, uh)
        p1 = uh[BF16_SUBLANES - 1:BF16_SUBLANES, :]
        p2 = uh[BF16_SUBLANES - 2:BF16_SUBLANES - 1, :]
        r1 = pltpu.roll(u, 1, 0)
        r2 = pltpu.roll(u, 2, 0)
        h1 = jnp.where(row8 == 0, p1, r1[:8, :])
        h2 = jnp.where(row8 == 0, p2, jnp.where(row8 == 1, p1, r2[:8, :]))
        u1 = jnp.concatenate([h1, r1[8:, :]], axis=0)
        u2 = jnp.concatenate([h2, r2[8:, :]], axis=0)
        cw = cw_ref[...]
        return cb_ref[...] + (cw[0:1, :] * u2 + cw[1:2, :] * u1 + cw[2:3, :] * u)

    gate = conv(wg_ref, cwg_ref, cbg_ref)
    val = conv(wv_ref, cwv_ref, cbv_ref)
    o_ref[...] = (_gelu_tanh(gate) * val).astype(o_ref.dtype)


def ffn_in(h, w_in, conv_w, conv_b, layer, seq, tm=1024, tn=256):
    t, k = h.shape
    f = w_in.shape[2] // 2
    nf = f // tn
    halo = BF16_SUBLANES
    hb = tm // halo
    conv_b = conv_b.reshape(conv_b.shape[0], 1, 2 * f)
    return pl.pallas_call(
        functools.partial(_ffn_in_kernel, tm=tm, seq=seq),
        grid=(t // tm, nf),
        in_specs=[
            pl.BlockSpec((tm, k), lambda i, j: (i, 0)),
            pl.BlockSpec((halo, k), lambda i, j: (jnp.maximum(i * hb - 1, 0), 0)),
            pl.BlockSpec((None, k, tn), lambda i, j: (layer, 0, j)),
            pl.BlockSpec((None, k, tn), lambda i, j: (layer, 0, j + nf)),
            pl.BlockSpec((None, CONV_W, tn), lambda i, j: (layer, 0, j)),
            pl.BlockSpec((None, CONV_W, tn), lambda i, j: (layer, 0, j + nf)),
            pl.BlockSpec((None, 1, tn), lambda i, j: (layer, 0, j)),
            pl.BlockSpec((None, 1, tn), lambda i, j: (layer, 0, j + nf)),
        ],
        out_specs=pl.BlockSpec((tm, tn), lambda i, j: (i, j)),
        out_shape=jax.ShapeDtypeStruct((t, f), BF16),
        compiler_params=_params("parallel", "arbitrary"),
        name="ffn_in",
    )(h, h, w_in, w_in, conv_w, conv_w, conv_b, conv_b)


def _sb_kernel(q_ref, k_ref, v_ref, o_ref, *, tq, scale, g):
    qi = pl.program_id(2)
    d = HEAD_DIM
    heads = [slice(hh * d, (hh + 1) * d) for hh in range(g)]
    qs = [q_ref[:, cs] for cs in heads]
    row = lax.broadcasted_iota(jnp.int32, (tq, tq), 0)
    col = lax.broadcasted_iota(jnp.int32, (tq, tq), 1)
    strict = col < row
    r2 = lax.broadcasted_iota(jnp.int32, (2 * tq, tq), 0)
    c2 = lax.broadcasted_iota(jnp.int32, (2 * tq, tq), 1)
    tri = (jnp.where(r2 >= tq, r2 - tq, r2) >= c2).astype(BF16)

    def block(kb, carry, diag):
        start = pl.multiple_of(kb * tq, tq)
        zs = [_dot_nt(q, k_ref[pl.ds(start, tq), cs]) * (scale * LOG2E) for q, cs in zip(qs, heads)]
        csums = []
        for z in zs:
            neg_abs = lax.bitcast_convert_type(
                lax.bitcast_convert_type(z, jnp.uint32) | jnp.uint32(0x80000000), F32)
            sp = jnp.maximum(z, 0.0) + jnp.log2(1.0 + jnp.exp2(neg_abs))
            if diag:
                sp = jnp.where(strict, sp, 0.0)
            hi = sp.astype(BF16)
            lo = (sp - hi.astype(F32)).astype(BF16)
            csums.append(_dot(jnp.concatenate([hi, lo], axis=1), tri))
        out = []
        for z, csum, cs, (rsum, acc) in zip(zs, csums, heads, carry):
            w = jnp.exp2(z - csum - rsum)
            if diag:
                w = jnp.where(strict, w, 0.0)
            acc = acc + _dot(w.astype(BF16), v_ref[pl.ds(start, tq), cs])
            rsum = rsum + csum[:, 0:1]
            out.append((rsum, acc))
        return tuple(out)

    init = tuple((jnp.zeros((tq, 1), F32), jnp.zeros((tq, d), F32)) for _ in range(g))
    carry = block(qi, init, True)

    def body(it, carry):
        return block(qi - 1 - it, carry, False)

    carry = lax.fori_loop(0, qi, body, carry)
    for cs, (_, acc) in zip(heads, carry):
        o_ref[:, cs] = acc.astype(o_ref.dtype)


def sb_attention(qkv, batch, seq, n_heads, q_col, k_col, v_col, g=SB_HEADS_PER_STEP):
    tq = ATTN_TILE
    nq = seq // tq
    d = HEAD_DIM
    assert n_heads % g == 0 and q_col % g == 0 and k_col % g == 0 and v_col % g == 0
    return pl.pallas_call(
        functools.partial(_sb_kernel, tq=tq, scale=1.0 / math.sqrt(d), g=g),
        grid=(batch, n_heads // g, nq),
        in_specs=[
            pl.BlockSpec((tq, g * d), lambda b, h, i: (b * nq + i, q_col // g + h)),
            pl.BlockSpec((seq, g * d), lambda b, h, i: (b, k_col // g + h)),
            pl.BlockSpec((seq, g * d), lambda b, h, i: (b, v_col // g + h)),
        ],
        out_specs=pl.BlockSpec((tq, g * d), lambda b, h, i: (b * nq + i, h)),
        out_shape=jax.ShapeDtypeStruct((batch * seq, n_heads * d), BF16),
        compiler_params=_params("parallel", "parallel", "arbitrary"),
        name="sb_attention",
    )(qkv, qkv, qkv)


def _online_step_t(s, vt, m, l, acc):
    m_new = jnp.maximum(m, jnp.max(s, axis=0, keepdims=True))
    alpha = jnp.exp2(m - m_new)
    p = jnp.exp2(s - m_new)
    l = alpha * l + jnp.sum(p, axis=0, keepdims=True)
    acc = alpha * acc + _dot(vt, p.astype(BF16))
    return m_new, l, acc


def _chunk_causal(tq):
    row = lax.broadcasted_iota(jnp.int32, (tq, tq), 0)
    col = lax.broadcasted_iota(jnp.int32, (tq, tq), 1)
    return row, col, (col // CHUNK) <= (row // CHUNK)


def _diff_kernel(q_ref, k_ref, v_ref, lam_ref, g_ref, o_ref, vt_scr, *, tq, scale, lam_init, g):
    hg = pl.program_id(1)
    qi = pl.program_id(2)
    d = HEAD_DIM
    dv = 2 * HEAD_DIM
    nq = vt_scr.shape[0]
    qk_cols = [slice(c * d, (c + 1) * d) for c in range(2 * g)]
    v_cols = [slice(hh * dv, (hh + 1) * dv) for hh in range(g)]
    qs = [q_ref[:, cs] for cs in qk_cols]

    @pl.when(qi == 0)
    def _():
        def transpose_tile(kb, c):
            start = pl.multiple_of(kb * tq, tq)
            vt_scr[kb] = v_ref[pl.ds(start, tq), :].astype(F32).T.astype(vt_scr.dtype)
            return c
        lax.fori_loop(0, nq, transpose_tile, 0)

    key, qry, _ = _chunk_causal(tq)
    visible = (key // CHUNK) <= (qry // CHUNK)
    dist = (qry - key).astype(F32)
    nbias_diag, nbias_off, slopes = [], [], []
    for hh in range(g):
        slope = lax.bitcast_convert_type(
            jnp.full((1, 1), (126 - (hg * g + hh)) * (1 << 23), jnp.int32), F32)
        slope = slope * LOG2E
        slopes.append(slope)
        nbias_diag.append(-slope * jnp.abs(dist))
        nbias_off.append(-slope * dist)

    def block(kb, carry, diag):
        start = pl.multiple_of(kb * tq, tq)
        ss = [_dot_nt(k_ref[pl.ds(start, tq), cs], q) * (scale * LOG2E) for q, cs in zip(qs, qk_cols)]
        for c in range(2 * g):
            hh = c // 2
            if diag:
                ss[c] = jnp.where(visible, ss[c] + nbias_diag[hh], NEG)
            else:
                ss[c] = ss[c] + (nbias_off[hh] - slopes[hh] * ((qi - kb) * tq).astype(F32))
        return tuple(_online_step_t(ss[c], vt_scr[kb, v_cols[c // 2], :], *carry[c])
                     for c in range(2 * g))

    def body(kb, carry):
        return block(kb, carry, False)

    init = tuple((jnp.full((1, tq), NEG, F32), jnp.zeros((1, tq), F32), jnp.zeros((dv, tq), F32))
                 for _ in range(2 * g))
    carry = lax.fori_loop(0, qi, body, init)
    carry = block(qi, carry, True)

    lf = lam_ref[...]
    lam = (jnp.exp(jnp.sum(lf[0:1, :] * lf[1:2, :], axis=-1, keepdims=True))
           - jnp.exp(jnp.sum(lf[2:3, :] * lf[3:4, :], axis=-1, keepdims=True)) + lam_init)
    for hh in range(g):
        (_, l1, a1), (_, l2, a2) = carry[2 * hh], carry[2 * hh + 1]
        o = (a1 / l1 - lam * (a2 / l2)).T
        o_ref[:, v_cols[hh]] = (_rms(o, g_ref[...]) * (1.0 - lam_init)).astype(o_ref.dtype)


def diff_attention(qkv, diff_lambda, subln_g, batch, seq, n_heads, q_col, k_col, v_col, lam_init,
                   g=DIFF_HEADS_PER_STEP):
    tq = ATTN_TILE
    nq = seq // tq
    dv = 2 * HEAD_DIM
    assert n_heads == 8, "ALiBi slopes are built as exact powers of two"
    assert n_heads % g == 0 and q_col % g == 0 and k_col % g == 0 and v_col % g == 0
    return pl.pallas_call(
        functools.partial(_diff_kernel, tq=tq, scale=1.0 / math.sqrt(HEAD_DIM), lam_init=lam_init, g=g),
        grid=(batch, n_heads // g, nq),
        in_specs=[
            pl.BlockSpec((tq, g * dv), lambda b, h, i: (b * nq + i, q_col // g + h)),
            pl.BlockSpec((seq, g * dv), lambda b, h, i: (b, k_col // g + h)),
            pl.BlockSpec((seq, g * dv), lambda b, h, i: (b, v_col // g + h)),
            pl.BlockSpec((4, HEAD_DIM), lambda b, h, i: (0, 0)),
            pl.BlockSpec((1, dv), lambda b, h, i: (0, 0)),
        ],
        out_specs=pl.BlockSpec((tq, g * dv), lambda b, h, i: (b * nq + i, h)),
        out_shape=jax.ShapeDtypeStruct((batch * seq, n_heads * dv), BF16),
        scratch_shapes=[pltpu.VMEM((nq, g * dv, tq), BF16)],
        compiler_params=_params("parallel", "parallel", "arbitrary"),
        name="diff_attention",
    )(qkv, qkv, qkv, diff_lambda, subln_g.reshape(1, dv))


def _chunk_kernel(q_ref, k_ref, v_ref, tab_ref, o_ref, *, tq, scale, n_win, g):
    qi = pl.program_id(2)
    d = HEAD_DIM
    heads = [slice(hh * d, (hh + 1) * d) for hh in range(g)]
    kbs = [qi - (n_win - 1) + w for w in range(n_win)]
    starts = [pl.multiple_of(jnp.maximum(kb, 0) * tq, tq) for kb in kbs]
    scores = []
    for hh, cs in enumerate(heads):
        q = q_ref[:, cs]
        s_blocks = []
        for w in range(n_win):
            s = (_dot_nt(q, k_ref[pl.ds(starts[w], tq), cs]) * (scale * LOG2E)
                 + tab_ref[hh, :, w * tq:(w + 1) * tq])
            if w < n_win - 1:
                s = jnp.where(kbs[w] >= 0, s, NEG)
            s_blocks.append(s)
        scores.append(s_blocks)
    for cs, s_blocks in zip(heads, scores):
        m = s_blocks[0].max(axis=-1, keepdims=True)
        for s in s_blocks[1:]:
            m = jnp.maximum(m, s.max(axis=-1, keepdims=True))
        l = jnp.zeros_like(m)
        acc = jnp.zeros((tq, d), F32)
        for w, s in enumerate(s_blocks):
            p = jnp.exp2(s - m)
            l = l + jnp.sum(p, axis=-1, keepdims=True)
            acc = acc + _dot(p.astype(BF16), v_ref[pl.ds(starts[w], tq), cs])
        o_ref[:, cs] = (acc / l).astype(o_ref.dtype)


def chunk_bias_table(rel_bias, tq):
    left = LEFT_CHUNKS * CHUNK
    n_heads = rel_bias.shape[0]
    width = tq + left
    period = 2 * left
    assert left >= REL_CLIP and tq <= left
    rb = rel_bias.astype(F32)
    hi = jnp.broadcast_to(rb[:, -1:], (n_heads, left - REL_CLIP))
    mid = rb[:, ::-1]
    lo = jnp.broadcast_to(rb[:, :1], (n_heads, tq - REL_CLIP - 1))
    neg_d = jnp.broadcast_to(rb[:, -1:], (n_heads, period - width))
    diag = jnp.concatenate([hi, mid, lo, neg_d], axis=1)
    flat = jnp.tile(diag, (1, tq))[:, :tq * (period - 1)]
    bias = flat.reshape(n_heads, tq, period - 1)[:, :, :width]
    i = jnp.arange(tq)[:, None]
    j = jnp.arange(width)[None, :]
    qc, kc = i // CHUNK, j // CHUNK
    allowed = (kc >= qc) & (kc <= qc + LEFT_CHUNKS)
    return jnp.where(allowed[None], bias * LOG2E, NEG)


def chunk_attention(qkv, rel_bias, batch, seq, n_heads, q_col, k_col, v_col, g=CHUNK_HEADS_PER_STEP):
    tq = ATTN_TILE
    nq = seq // tq
    d = HEAD_DIM
    left = LEFT_CHUNKS * CHUNK
    assert left % tq == 0
    assert n_heads % g == 0 and q_col % g == 0 and k_col % g == 0 and v_col % g == 0
    n_win = left // tq + 1
    table = chunk_bias_table(rel_bias, tq)
    return pl.pallas_call(
        functools.partial(_chunk_kernel, tq=tq, scale=1.0 / math.sqrt(d), n_win=n_win, g=g),
        grid=(n_heads // g, batch, nq),
        in_specs=[
            pl.BlockSpec((tq, g * d), lambda h, b, i: (b * nq + i, q_col // g + h)),
            pl.BlockSpec((seq, g * d), lambda h, b, i: (b, k_col // g + h)),
            pl.BlockSpec((seq, g * d), lambda h, b, i: (b, v_col // g + h)),
            pl.BlockSpec((g, tq, tq + left), lambda h, b, i: (h, 0, 0)),
        ],
        out_specs=pl.BlockSpec((tq, g * d), lambda h, b, i: (b * nq + i, h)),
        out_shape=jax.ShapeDtypeStruct((batch * seq, n_heads * d), BF16),
        compiler_params=_params("parallel", "parallel", "arbitrary"),
        name="chunk_attention",
    )(qkv, qkv, qkv, table)


def rope_tables(seq):
    half = QK_ROPE // 2
    pos = jnp.arange(seq, dtype=F32)
    inv_freq = ROPE_THETA ** (-jnp.arange(0, QK_ROPE, 2, dtype=F32) / QK_ROPE)
    ang = pos[:, None] * inv_freq[None, :]
    cos, sin = jnp.cos(ang), jnp.sin(ang)
    z = lambda n: jnp.zeros((seq, n), F32)
    cos_t = jnp.concatenate([cos, cos, z(LANES - 2 * half)], axis=1)
    sin_a = jnp.concatenate([-sin, z(LANES - half)], axis=1)
    sin_b = jnp.concatenate([z(half), sin, z(LANES - 2 * half)], axis=1)
    return cos_t, sin_a, sin_b


def _rope(x, cos_t, sin_a, sin_b):
    half = QK_ROPE // 2
    return x * cos_t + pltpu.roll(x, LANES - half, 1) * sin_a + pltpu.roll(x, half, 1) * sin_b


def _mla_q_kernel(c_ref, g_ref, w_ref, cos_ref, sa_ref, sb_ref, o_ref, an_ref, *, hp):
    @pl.when(pl.program_id(1) == 0)
    def _():
        an_ref[...] = _rms(c_ref[...], g_ref[...]).astype(an_ref.dtype)

    hw = QK_NOPE + LANES
    res = _dot(an_ref[...], w_ref[...])
    cos_t, sin_a, sin_b = cos_ref[...], sa_ref[...], sb_ref[...]
    for hh in range(hp):
        c0 = hh * hw
        o_ref[:, c0:c0 + QK_NOPE] = res[:, c0:c0 + QK_NOPE].astype(o_ref.dtype)
        o_ref[:, c0 + QK_NOPE:c0 + hw] = _rope(res[:, c0 + QK_NOPE:c0 + hw],
                                               cos_t, sin_a, sin_b).astype(o_ref.dtype)


def mla_q(lat, g, w_q, tables, seq, n_heads, tm=512, hp=MLA_PROJ_HEADS_PER_STEP):
    t = lat.shape[0]
    kq = g.shape[0]
    hw = QK_NOPE + LANES
    npos = seq // tm
    tab_spec = pl.BlockSpec((tm, LANES), lambda i, j: (i % npos, 0))
    return pl.pallas_call(
        functools.partial(_mla_q_kernel, hp=hp),
        grid=(t // tm, n_heads // hp),
        in_specs=[
            pl.BlockSpec((tm, kq), lambda i, j: (i, 0)),
            pl.BlockSpec((1, kq), lambda i, j: (0, 0)),
            pl.BlockSpec((kq, hp * hw), lambda i, j: (0, j)),
            tab_spec, tab_spec, tab_spec,
        ],
        out_specs=pl.BlockSpec((tm, hp * hw), lambda i, j: (i, j)),
        out_shape=jax.ShapeDtypeStruct((t, n_heads * hw), BF16),
        scratch_shapes=[pltpu.VMEM((tm, kq), BF16)],
        compiler_params=_params("parallel", "arbitrary"),
        name="mla_q",
    )(lat, g.reshape(1, kq), w_q, *tables)


def _mla_kv_kernel(c_ref, kr_ref, g_ref, w_ref, cos_ref, sa_ref, sb_ref, k_ref, vt_ref,
                   an_ref, kr_scr, *, hp, tm, tk):
    @pl.when(pl.program_id(1) == 0)
    def _():
        an_ref[...] = _rms(c_ref[...], g_ref[...]).astype(an_ref.dtype)
        kr_scr[...] = _rope(kr_ref[...], cos_ref[...], sa_ref[...], sb_ref[...]).astype(kr_scr.dtype)

    hw = QK_NOPE + LANES
    wv = QK_NOPE + V_MLA
    res = _dot(an_ref[...], w_ref[...])
    kr = kr_scr[...]
    for hh in range(hp):
        k_ref[:, hh * hw:hh * hw + QK_NOPE] = res[:, hh * wv:hh * wv + QK_NOPE].astype(k_ref.dtype)
        k_ref[:, hh * hw + QK_NOPE:(hh + 1) * hw] = kr
        v = res[:, hh * wv + QK_NOPE:(hh + 1) * wv]
        for kt in range(tm // tk):
            vt_ref[kt, hh * V_MLA:(hh + 1) * V_MLA, :] = v[kt * tk:(kt + 1) * tk, :].T.astype(vt_ref.dtype)


def mla_kv(lat, g, w_kv, tables, seq, n_heads, ckv_col, kr_col, tm=512, hp=MLA_PROJ_HEADS_PER_STEP):
    t = lat.shape[0]
    kkv = g.shape[0]
    hw = QK_NOPE + LANES
    tk = ATTN_TILE
    assert tm % tk == 0
    npos = seq // tm
    tab_spec = pl.BlockSpec((tm, LANES), lambda i, j: (i % npos, 0))
    return pl.pallas_call(
        functools.partial(_mla_kv_kernel, hp=hp, tm=tm, tk=tk),
        grid=(t // tm, n_heads // hp),
        in_specs=[
            pl.BlockSpec((tm, kkv), lambda i, j: (i, ckv_col)),
            pl.BlockSpec((tm, LANES), lambda i, j: (i, kr_col)),
            pl.BlockSpec((1, kkv), lambda i, j: (0, 0)),
            pl.BlockSpec((kkv, hp * (QK_NOPE + V_MLA)), lambda i, j: (0, j)),
            tab_spec, tab_spec, tab_spec,
        ],
        out_specs=[pl.BlockSpec((tm, hp * hw), lambda i, j: (i, j)),
                   pl.BlockSpec((tm // tk, hp * V_MLA, tk), lambda i, j: (i, j, 0))],
        out_shape=[jax.ShapeDtypeStruct((t, n_heads * hw), BF16),
                   jax.ShapeDtypeStruct((t // tk, n_heads * V_MLA, tk), BF16)],
        scratch_shapes=[pltpu.VMEM((tm, kkv), BF16), pltpu.VMEM((tm, LANES), BF16)],
        compiler_params=_params("parallel", "arbitrary"),
        name="mla_kv",
    )(lat, lat, g.reshape(1, kkv), w_kv, *tables)


def _mla_attn_kernel(q_ref, k_ref, vt_ref, o_ref, *, tq, scale, g):
    qi = pl.program_id(2)
    hw = QK_NOPE + LANES
    dv = V_MLA
    qk_cols = [slice(hh * hw, (hh + 1) * hw) for hh in range(g)]
    v_cols = [slice(hh * dv, (hh + 1) * dv) for hh in range(g)]
    qs = [q_ref[:, cs] for cs in qk_cols]
    key, qry, _ = _chunk_causal(tq)
    visible = (key // CHUNK) <= (qry // CHUNK)

    def block(kb, carry, diag):
        start = pl.multiple_of(kb * tq, tq)
        ss = [_dot_nt(k_ref[pl.ds(start, tq), cs], q) * (scale * LOG2E) for q, cs in zip(qs, qk_cols)]
        if diag:
            ss = [jnp.where(visible, s, NEG) for s in ss]
        return tuple(_online_step_t(s, vt_ref[kb, vs, :], *st)
                     for s, vs, st in zip(ss, v_cols, carry))

    def body(kb, carry):
        return block(kb, carry, False)

    init = tuple((jnp.full((1, tq), NEG, F32), jnp.zeros((1, tq), F32), jnp.zeros((dv, tq), F32))
                 for _ in range(g))
    carry = lax.fori_loop(0, qi, body, init)
    carry = block(qi, carry, True)
    for vs, (_, l, acc) in zip(v_cols, carry):
        o_ref[:, vs] = (acc / l).T.astype(o_ref.dtype)


def mla_attention(q, k, vt, batch, seq, n_heads, g=MLA_HEADS_PER_STEP):
    tq = ATTN_TILE
    nq = seq // tq
    hw = QK_NOPE + LANES
    assert n_heads % g == 0
    return pl.pallas_call(
        functools.partial(_mla_attn_kernel, tq=tq, scale=1.0 / math.sqrt(QK_NOPE + QK_ROPE), g=g),
        grid=(batch, n_heads // g, nq),
        in_specs=[
            pl.BlockSpec((tq, g * hw), lambda b, h, i: (b * nq + i, h)),
            pl.BlockSpec((seq, g * hw), lambda b, h, i: (b, h), pipeline_mode=pl.Buffered(1)),
            pl.BlockSpec((nq, g * V_MLA, tq), lambda b, h, i: (b, h, 0), pipeline_mode=pl.Buffered(1)),
        ],
        out_specs=pl.BlockSpec((tq, g * V_MLA), lambda b, h, i: (b * nq + i, h)),
        out_shape=jax.ShapeDtypeStruct((batch * seq, n_heads * V_MLA), BF16),
        compiler_params=_params("parallel", "parallel", "arbitrary"),
        name="mla_attention",
    )(q, k, vt)


def _even_mixer(hn, w_in, w_out, i, diff_lambda, subln_g, batch, seq, layer):
    d_model = hn.shape[1]
    w_sb = d_model // 2
    n_sb = w_sb // HEAD_DIM
    n_diff = w_sb // (2 * HEAD_DIM)
    cb = w_sb // LANES
    qkv = matmul([(hn, w_in, i, 0)], w_in.shape[2], BF16, tm=1024, tn=512, name="even_in_proj")
    a = sb_attention(qkv, batch, seq, n_sb, 0, cb, 2 * cb)
    lam_init = 0.8 - 0.6 * math.exp(-0.3 * layer)
    bo = diff_attention(qkv, diff_lambda, subln_g, batch, seq, n_diff,
                        3 * cb // 2, 4 * cb // 2, 5 * cb // 2, lam_init)
    return matmul([(a, w_out, i, 0), (bo, w_out, i, 1)], d_model, F32, tm=1024, tn=512,
                  name="mix_out_proj")


def _odd_mixer(hn, w_in, w_out, i, rel_bias, q_norm_g, w_uq, kv_norm_g, w_ukv, batch, seq):
    d_model = hn.shape[1]
    w_ch = d_model // 2
    n_ch = w_ch // HEAD_DIM
    n_mla = w_ch // HEAD_DIM
    cb = w_ch // LANES
    q_lora, kv_lora = q_norm_g.shape[0], kv_norm_g.shape[0]
    n_attn = 3 * w_ch
    n_lat = q_lora + kv_lora + QK_ROPE
    lat_pad = -n_lat % LANES
    w_lat = jnp.pad(w_in[i:i + 1, :, n_attn:], ((0, 0), (0, 0), (0, lat_pad))).astype(BF16)
    qkv = matmul([(hn, w_in, i, 0)], n_attn, BF16, tm=1024, tn=512, name="odd_in_proj")
    lat = matmul([(hn, w_lat, 0, 0)], n_lat + lat_pad, F32, tm=512, tn=n_lat + lat_pad,
                 name="odd_lat_proj")
    c = chunk_attention(qkv, rel_bias, batch, seq, n_ch, 0, cb, 2 * cb)

    tables = rope_tables(seq)
    hw = QK_NOPE + LANES
    wq = w_uq.reshape(q_lora, n_mla, QK_NOPE + QK_ROPE)
    wq = jnp.pad(wq, ((0, 0), (0, 0), (0, hw - QK_NOPE - QK_ROPE))).reshape(q_lora, n_mla * hw).astype(BF16)
    q = mla_q(lat, q_norm_g, wq, tables, seq, n_mla)
    assert q_lora % kv_lora == 0 and (q_lora + kv_lora) % LANES == 0
    k, v = mla_kv(lat, kv_norm_g, w_ukv.astype(BF16), tables, seq, n_mla,
                  q_lora // kv_lora, (q_lora + kv_lora) // LANES)
    dm = mla_attention(q, k, v, batch, seq, n_mla)
    return matmul([(c, w_out, i, 0), (dm, w_out, i, 1)], d_model, F32, tm=1024, tn=512,
                  name="mix_out_proj")


def _ffn(h, w_in, conv_w, conv_b, w_out, layer, seq):
    g = ffn_in(h, w_in, conv_w, conv_b, layer, seq)
    return matmul([(g, w_out.astype(BF16), layer, 0)], w_out.shape[2], F32, tm=512, tn=256,
                  name="ffn_out_proj")


def kernel(x, norm_g, even_w_in, even_w_out, diff_lambda, diff_subln_g, odd_w_in, odd_w_out,
           ch_rel_bias, mla_q_norm_g, mla_w_uq, mla_kv_norm_g, mla_w_ukv, ffn_w_in, ffn_conv_w,
           ffn_conv_b, ffn_w_out):
    batch, seq, d_model = x.shape
    depth = norm_g.shape[0]
    xf = x.reshape(batch * seq, d_model)
    hn = norm_cast(xf, norm_g[0, 0])
    for layer in range(depth):
        g = norm_g[layer]
        i = layer // 2
        if layer % 2 == 0:
            mix = _even_mixer(hn, even_w_in, even_w_out, i, diff_lambda[i], diff_subln_g[i],
                              batch, seq, layer)
        else:
            mix = _odd_mixer(hn, odd_w_in, odd_w_out, i, ch_rel_bias[i], mla_q_norm_g[i],
                             mla_w_uq[i], mla_kv_norm_g[i], mla_w_ukv[i], batch, seq)
        xf, h2 = resid_norm(xf, mix, g[1], g[2])
        f = _ffn(h2, ffn_w_in, ffn_conv_w, ffn_conv_b, ffn_w_out, layer, seq)
        if layer + 1 < depth:
            xf, hn = resid_norm(xf, f, g[3], norm_g[layer + 1, 0])
        else:
            xf = resid(xf, f, g[3])
    return xf.reshape(batch, seq, d_model)
```

```python
import functools
import math

import jax
import jax.numpy as jnp
from jax import lax
from jax.experimental import pallas as pl
from jax.experimental.pallas import tpu as pltpu

F32 = jnp.float32
BF16 = jnp.bfloat16

CHUNK = 64
HEAD_DIM = 128
LEFT_CHUNKS = 8
REL_CLIP = 128
QK_NOPE = 128
QK_ROPE = 64
V_MLA = 128
ROPE_THETA = 10000.0
CONV_W = 3
EPS = 1e-6
NEG = -1e30
LOG2E = math.log2(math.e)

LANES = 128
BF16_SUBLANES = 16
VMEM_LIMIT = 52 * 1024 * 1024

ATTN_TILE = 256
SB_HEADS_PER_STEP = 8
MLA_HEADS_PER_STEP = 8
DIFF_HEADS_PER_STEP = 4
CHUNK_HEADS_PER_STEP = 4
MLA_PROJ_HEADS_PER_STEP = 4


def _params(*sem):
    return pltpu.CompilerParams(dimension_semantics=sem, vmem_limit_bytes=VMEM_LIMIT)


def _rms(x, g):
    ms = jnp.mean(x * x, axis=-1, keepdims=True)
    return x * lax.rsqrt(ms + EPS) * g


def _dot(a, b):
    return jnp.dot(a, b, preferred_element_type=F32)


def _dot_nt(a, b):
    return lax.dot_general(a, b, (((1,), (1,)), ((), ())), preferred_element_type=F32)


def _norm_cast_kernel(x_ref, g_ref, h_ref):
    h_ref[...] = _rms(x_ref[...], g_ref[...]).astype(h_ref.dtype)


def _resid_norm_kernel(x_ref, y_ref, g1_ref, g2_ref, xo_ref, h_ref):
    xn = x_ref[...] + _rms(y_ref[...], g1_ref[...])
    xo_ref[...] = xn
    h_ref[...] = _rms(xn, g2_ref[...]).astype(h_ref.dtype)


def _resid_kernel(x_ref, y_ref, g_ref, xo_ref):
    xo_ref[...] = x_ref[...] + _rms(y_ref[...], g_ref[...])


def _row_spec(tr, d):
    return pl.BlockSpec((tr, d), lambda i: (i, 0))


def _vec_spec(d):
    return pl.BlockSpec((1, d), lambda i: (0, 0))


def norm_cast(x, g, tr=256):
    t, d = x.shape
    return pl.pallas_call(
        _norm_cast_kernel,
        grid=(t // tr,),
        in_specs=[_row_spec(tr, d), _vec_spec(d)],
        out_specs=_row_spec(tr, d),
        out_shape=jax.ShapeDtypeStruct((t, d), BF16),
        compiler_params=_params("parallel"),
        name="norm_cast",
    )(x, g.reshape(1, d))


def resid_norm(x, y, g1, g2, tr=256):
    t, d = x.shape
    return pl.pallas_call(
        _resid_norm_kernel,
        grid=(t // tr,),
        in_specs=[_row_spec(tr, d), _row_spec(tr, d), _vec_spec(d), _vec_spec(d)],
        out_specs=[_row_spec(tr, d), _row_spec(tr, d)],
        out_shape=[jax.ShapeDtypeStruct((t, d), F32), jax.ShapeDtypeStruct((t, d), BF16)],
        compiler_params=_params("parallel"),
        name="resid_norm",
    )(x, y, g1.reshape(1, d), g2.reshape(1, d))


def resid(x, y, g, tr=256):
    t, d = x.shape
    return pl.pallas_call(
        _resid_kernel,
        grid=(t // tr,),
        in_specs=[_row_spec(tr, d), _row_spec(tr, d), _vec_spec(d)],
        out_specs=_row_spec(tr, d),
        out_shape=jax.ShapeDtypeStruct((t, d), F32),
        compiler_params=_params("parallel"),
        name="resid",
    )(x, y, g.reshape(1, d))


def _matmul_kernel(*refs, n_pairs):
    o_ref = refs[2 * n_pairs]
    acc = _dot(refs[0][...], refs[n_pairs][...].astype(BF16))
    for p in range(1, n_pairs):
        acc = acc + _dot(refs[p][...], refs[n_pairs + p][...].astype(BF16))
    o_ref[...] = acc.astype(o_ref.dtype)


def matmul(pairs, n, out_dtype, tm, tn, name):
    m = pairs[0][0].shape[0]
    n_pairs = len(pairs)
    a_specs, w_specs, args_a, args_w = [], [], [], []
    for a, w, layer, rb in pairs:
        k = a.shape[1]
        a_specs.append(pl.BlockSpec((tm, k), lambda i, j: (i, 0)))
        w_specs.append(pl.BlockSpec((None, k, tn), lambda i, j, layer=layer, rb=rb: (layer, rb, j)))
        args_a.append(a)
        args_w.append(w)
    return pl.pallas_call(
        functools.partial(_matmul_kernel, n_pairs=n_pairs),
        grid=(m // tm, n // tn),
        in_specs=a_specs + w_specs,
        out_specs=pl.BlockSpec((tm, tn), lambda i, j: (i, j)),
        out_shape=jax.ShapeDtypeStruct((m, n), out_dtype),
        compiler_params=_params("parallel", "arbitrary"),
        name=name,
    )(*args_a, *args_w)


def _gelu_tanh(x):
    c = math.sqrt(2.0 / math.pi)
    return x * (0.5 * (1.0 + jnp.tanh(c * (x + 0.044715 * (x * x * x)))))


def _ffn_in_kernel(a_ref, ah_ref, wg_ref, wv_ref, cwg_ref, cwv_ref, cbg_ref, cbv_ref,
                   o_ref, *, tm, seq):
    i = pl.program_id(0)
    tn = o_ref.shape[1]
    a = a_ref[...]
    ah = ah_ref[...]
    seq_start = (i * tm) % seq == 0
    row8 = lax.broadcasted_iota(jnp.int32, (8, tn), 0)

    def conv(w_ref, cw_ref, cb_ref):
        w = w_ref[...].astype(BF16)
        u = _dot(a, w)
        uh = _dot(ah, w)
        uh = jnp.where(seq_start, 0.0, uh)
        p1 = uh[BF16_SUBLANES - 1:BF16_SUBLANES, :]
        p2 = uh[BF16_SUBLANES - 2:BF16_SUBLANES - 1, :]
        r1 = pltpu.roll(u, 1, 0)
        r2 = pltpu.roll(u, 2, 0)
        h1 = jnp.where(row8 == 0, p1, r1[:8, :])
        h2 = jnp.where(row8 == 0, p2, jnp.where(row8 == 1, p1, r2[:8, :]))
        u1 = jnp.concatenate([h1, r1[8:, :]], axis=0)
        u2 = jnp.concatenate([h2, r2[8:, :]], axis=0)
        cw = cw_ref[...]
        return cb_ref[...] + (cw[0:1, :] * u2 + cw[1:2, :] * u1 + cw[2:3, :] * u)

    gate = conv(wg_ref, cwg_ref, cbg_ref)
    val = conv(wv_ref, cwv_ref, cbv_ref)
    o_ref[...] = (_gelu_tanh(gate) * val).astype(o_ref.dtype)


def ffn_in(h, w_in, conv_w, conv_b, layer, seq, tm=1024, tn=256):
    t, k = h.shape
    f = w_in.shape[2] // 2
    nf = f // tn
    halo = BF16_SUBLANES
    hb = tm // halo
    conv_b = conv_b.reshape(conv_b.shape[0], 1, 2 * f)
    return pl.pallas_call(
        functools.partial(_ffn_in_kernel, tm=tm, seq=seq),
        grid=(t // tm, nf),
        in_specs=[
            pl.BlockSpec((tm, k), lambda i, j: (i, 0)),
            pl.BlockSpec((halo, k), lambda i, j: (jnp.maximum(i * hb - 1, 0), 0)),
            pl.BlockSpec((None, k, tn), lambda i, j: (layer, 0, j)),
            pl.BlockSpec((None, k, tn), lambda i, j: (layer, 0, j + nf)),
            pl.BlockSpec((None, CONV_W, tn), lambda i, j: (layer, 0, j)),
            pl.BlockSpec((None, CONV_W, tn), lambda i, j: (layer, 0, j + nf)),
            pl.BlockSpec((None, 1, tn), lambda i, j: (layer, 0, j)),
            pl.BlockSpec((None, 1, tn), lambda i, j: (layer, 0, j + nf)),
        ],
        out_specs=pl.BlockSpec((tm, tn), lambda i, j: (i, j)),
        out_shape=jax.ShapeDtypeStruct((t, f), BF16),
        compiler_params=_params("parallel", "arbitrary"),
        name="ffn_in",
    )(h, h, w_in, w_in, conv_w, conv_w, conv_b, conv_b)


def _transpose_value_tiles(v_ref, vt_scr, tq):
    def transpose_tile(kb, c):
        start = pl.multiple_of(kb * tq, tq)
        vt_scr[kb] = v_ref[pl.ds(start, tq), :].astype(F32).T.astype(vt_scr.dtype)
        return c
    lax.fori_loop(0, vt_scr.shape[0], transpose_tile, 0)


def _sb_kernel(q_ref, k_ref, v_ref, o_ref, vt_scr, *, tq, scale, g):
    qi = pl.program_id(2)
    d = HEAD_DIM
    heads = [slice(hh * d, (hh + 1) * d) for hh in range(g)]
    qs = [q_ref[:, cs] for cs in heads]

    @pl.when(qi == 0)
    def _():
        _transpose_value_tiles(v_ref, vt_scr, tq)

    key = lax.broadcasted_iota(jnp.int32, (tq, tq), 0)
    qry = lax.broadcasted_iota(jnp.int32, (tq, tq), 1)
    strict = key < qry
    r2 = lax.broadcasted_iota(jnp.int32, (tq, 2 * tq), 0)
    c2 = lax.broadcasted_iota(jnp.int32, (tq, 2 * tq), 1)
    tri = (jnp.where(c2 >= tq, c2 - tq, c2) >= r2).astype(BF16)

    def block(kb, carry, diag):
        start = pl.multiple_of(kb * tq, tq)
        zs = [_dot_nt(k_ref[pl.ds(start, tq), cs], q) * (scale * LOG2E) for q, cs in zip(qs, heads)]
        csums = []
        for z in zs:
            neg_abs = lax.bitcast_convert_type(
                lax.bitcast_convert_type(z, jnp.uint32) | jnp.uint32(0x80000000), F32)
            sp = jnp.maximum(z, 0.0) + jnp.log2(1.0 + jnp.exp2(neg_abs))
            if diag:
                sp = jnp.where(strict, sp, 0.0)
            hi = sp.astype(BF16)
            lo = (sp - hi.astype(F32)).astype(BF16)
            csums.append(_dot(tri, jnp.concatenate([hi, lo], axis=0)))
        out = []
        for z, csum, cs, (rsum, acc) in zip(zs, csums, heads, carry):
            w = jnp.exp2(z - csum - rsum)
            if diag:
                w = jnp.where(strict, w, 0.0)
            acc = acc + _dot(vt_scr[kb, cs, :], w.astype(BF16))
            rsum = rsum + csum[0:1, :]
            out.append((rsum, acc))
        return tuple(out)

    init = tuple((jnp.zeros((1, tq), F32), jnp.zeros((d, tq), F32)) for _ in range(g))
    carry = block(qi, init, True)

    def body(it, carry):
        return block(qi - 1 - it, carry, False)

    carry = lax.fori_loop(0, qi, body, carry)
    for cs, (_, acc) in zip(heads, carry):
        o_ref[:, cs] = acc.T.astype(o_ref.dtype)


def sb_attention(qkv, batch, seq, n_heads, q_col, k_col, v_col, g=SB_HEADS_PER_STEP):
    tq = ATTN_TILE
    nq = seq // tq
    d = HEAD_DIM
    assert n_heads % g == 0 and q_col % g == 0 and k_col % g == 0 and v_col % g == 0
    return pl.pallas_call(
        functools.partial(_sb_kernel, tq=tq, scale=1.0 / math.sqrt(d), g=g),
        grid=(batch, n_heads // g, nq),
        in_specs=[
            pl.BlockSpec((tq, g * d), lambda b, h, i: (b * nq + i, q_col // g + h)),
            pl.BlockSpec((seq, g * d), lambda b, h, i: (b, k_col // g + h)),
            pl.BlockSpec((seq, g * d), lambda b, h, i: (b, v_col // g + h)),
        ],
        out_specs=pl.BlockSpec((tq, g * d), lambda b, h, i: (b * nq + i, h)),
        out_shape=jax.ShapeDtypeStruct((batch * seq, n_heads * d), BF16),
        scratch_shapes=[pltpu.VMEM((nq, g * d, tq), BF16)],
        compiler_params=_params("parallel", "parallel", "arbitrary"),
        name="sb_attention",
    )(qkv, qkv, qkv)


def _online_step_t(s, vt, m, l, acc):
    m_new = jnp.maximum(m, jnp.max(s, axis=0, keepdims=True))
    alpha = jnp.exp2(m - m_new)
    p = jnp.exp2(s - m_new)
    l = alpha * l + jnp.sum(p, axis=0, keepdims=True)
    acc = alpha * acc + _dot(vt, p.astype(BF16))
    return m_new, l, acc


def _chunk_causal(tq):
    row = lax.broadcasted_iota(jnp.int32, (tq, tq), 0)
    col = lax.broadcasted_iota(jnp.int32, (tq, tq), 1)
    return row, col, (col // CHUNK) <= (row // CHUNK)


def _diff_kernel(q_ref, k_ref, v_ref, lam_ref, g_ref, o_ref, vt_scr, *, tq, scale, lam_init, g):
    hg = pl.program_id(1)
    qi = pl.program_id(2)
    d = HEAD_DIM
    dv = 2 * HEAD_DIM
    qk_cols = [slice(c * d, (c + 1) * d) for c in range(2 * g)]
    v_cols = [slice(hh * dv, (hh + 1) * dv) for hh in range(g)]
    qs = [q_ref[:, cs] for cs in qk_cols]

    @pl.when(qi == 0)
    def _():
        _transpose_value_tiles(v_ref, vt_scr, tq)

    key, qry, _ = _chunk_causal(tq)
    visible = (key // CHUNK) <= (qry // CHUNK)
    dist = (qry - key).astype(F32)
    nbias_diag, nbias_off, slopes = [], [], []
    for hh in range(g):
        slope = lax.bitcast_convert_type(
            jnp.full((1, 1), (126 - (hg * g + hh)) * (1 << 23), jnp.int32), F32)
        slope = slope * LOG2E
        slopes.append(slope)
        nbias_diag.append(-slope * jnp.abs(dist))
        nbias_off.append(-slope * dist)

    def block(kb, carry, diag):
        start = pl.multiple_of(kb * tq, tq)
        ss = [_dot_nt(k_ref[pl.ds(start, tq), cs], q) * (scale * LOG2E) for q, cs in zip(qs, qk_cols)]
        for c in range(2 * g):
            hh = c // 2
            if diag:
                ss[c] = jnp.where(visible, ss[c] + nbias_diag[hh], NEG)
            else:
                ss[c] = ss[c] + (nbias_off[hh] - slopes[hh] * ((qi - kb) * tq).astype(F32))
        return tuple(_online_step_t(ss[c], vt_scr[kb, v_cols[c // 2], :], *carry[c])
                     for c in range(2 * g))

    def body(kb, carry):
        return block(kb, carry, False)

    init = tuple((jnp.full((1, tq), NEG, F32), jnp.zeros((1, tq), F32), jnp.zeros((dv, tq), F32))
                 for _ in range(2 * g))
    carry = lax.fori_loop(0, qi, body, init)
    carry = block(qi, carry, True)

    lf = lam_ref[...]
    lam = (jnp.exp(jnp.sum(lf[0:1, :] * lf[1:2, :], axis=-1, keepdims=True))
           - jnp.exp(jnp.sum(lf[2:3, :] * lf[3:4, :], axis=-1, keepdims=True)) + lam_init)
    for hh in range(g):
        (_, l1, a1), (_, l2, a2) = carry[2 * hh], carry[2 * hh + 1]
        o = (a1 / l1 - lam * (a2 / l2)).T
        o_ref[:, v_cols[hh]] = (_rms(o, g_ref[...]) * (1.0 - lam_init)).astype(o_ref.dtype)


def diff_attention(qkv, diff_lambda, subln_g, batch, seq, n_heads, q_col, k_col, v_col, lam_init,
                   g=DIFF_HEADS_PER_STEP):
    tq = ATTN_TILE
    nq = seq // tq
    dv = 2 * HEAD_DIM
    assert n_heads == 8, "ALiBi slopes are built as exact powers of two"
    assert n_heads % g == 0 and q_col % g == 0 and k_col % g == 0 and v_col % g == 0
    return pl.pallas_call(
        functools.partial(_diff_kernel, tq=tq, scale=1.0 / math.sqrt(HEAD_DIM), lam_init=lam_init, g=g),
        grid=(batch, n_heads // g, nq),
        in_specs=[
            pl.BlockSpec((tq, g * dv), lambda b, h, i: (b * nq + i, q_col // g + h)),
            pl.BlockSpec((seq, g * dv), lambda b, h, i: (b, k_col // g + h)),
            pl.BlockSpec((seq, g * dv), lambda b, h, i: (b, v_col // g + h)),
            pl.BlockSpec((4, HEAD_DIM), lambda b, h, i: (0, 0)),
            pl.BlockSpec((1, dv), lambda b, h, i: (0, 0)),
        ],
        out_specs=pl.BlockSpec((tq, g * dv), lambda b, h, i: (b * nq + i, h)),
        out_shape=jax.ShapeDtypeStruct((batch * seq, n_heads * dv), BF16),
        scratch_shapes=[pltpu.VMEM((nq, g * dv, tq), BF16)],
        compiler_params=_params("parallel", "parallel", "arbitrary"),
        name="diff_attention",
    )(qkv, qkv, qkv, diff_lambda, subln_g.reshape(1, dv))


def _chunk_kernel(q_ref, k_ref, v_ref, tab_ref, o_ref, *, tq, scale, n_win, g):
    qi = pl.program_id(2)
    d = HEAD_DIM
    heads = [slice(hh * d, (hh + 1) * d) for hh in range(g)]
    kbs = [qi - (n_win - 1) + w for w in range(n_win)]
    starts = [pl.multiple_of(jnp.maximum(kb, 0) * tq, tq) for kb in kbs]
    scores = []
    for hh, cs in enumerate(heads):
        q = q_ref[:, cs]
        s_blocks = []
        for w in range(n_win):
            s = (_dot_nt(q, k_ref[pl.ds(starts[w], tq), cs]) * (scale * LOG2E)
                 + tab_ref[hh, :, w * tq:(w + 1) * tq])
            if w < n_win - 1:
                s = jnp.where(kbs[w] >= 0, s, NEG)
            s_blocks.append(s)
        scores.append(s_blocks)
    for cs, s_blocks in zip(heads, scores):
        m = s_blocks[0].max(axis=-1, keepdims=True)
        for s in s_blocks[1:]:
            m = jnp.maximum(m, s.max(axis=-1, keepdims=True))
        l = jnp.zeros_like(m)
        acc = jnp.zeros((tq, d), F32)
        for w, s in enumerate(s_blocks):
            p = jnp.exp2(s - m)
            l = l + jnp.sum(p, axis=-1, keepdims=True)
            acc = acc + _dot(p.astype(BF16), v_ref[pl.ds(starts[w], tq), cs])
        o_ref[:, cs] = (acc / l).astype(o_ref.dtype)


def chunk_bias_table(rel_bias, tq):
    left = LEFT_CHUNKS * CHUNK
    n_heads = rel_bias.shape[0]
    width = tq + left
    period = 2 * left
    assert left >= REL_CLIP and tq <= left
    rb = rel_bias.astype(F32)
    hi = jnp.broadcast_to(rb[:, -1:], (n_heads, left - REL_CLIP))
    mid = rb[:, ::-1]
    lo = jnp.broadcast_to(rb[:, :1], (n_heads, tq - REL_CLIP - 1))
    neg_d = jnp.broadcast_to(rb[:, -1:], (n_heads, period - width))
    diag = jnp.concatenate([hi, mid, lo, neg_d], axis=1)
    flat = jnp.tile(diag, (1, tq))[:, :tq * (period - 1)]
    bias = flat.reshape(n_heads, tq, period - 1)[:, :, :width]
    i = jnp.arange(tq)[:, None]
    j = jnp.arange(width)[None, :]
    qc, kc = i // CHUNK, j // CHUNK
    allowed = (kc >= qc) & (kc <= qc + LEFT_CHUNKS)
    return jnp.where(allowed[None], bias * LOG2E, NEG)


def chunk_attention(qkv, rel_bias, batch, seq, n_heads, q_col, k_col, v_col, g=CHUNK_HEADS_PER_STEP):
    tq = ATTN_TILE
    nq = seq // tq
    d = HEAD_DIM
    left = LEFT_CHUNKS * CHUNK
    assert left % tq == 0
    assert n_heads % g == 0 and q_col % g == 0 and k_col % g == 0 and v_col % g == 0
    n_win = left // tq + 1
    table = chunk_bias_table(rel_bias, tq)
    return pl.pallas_call(
        functools.partial(_chunk_kernel, tq=tq, scale=1.0 / math.sqrt(d), n_win=n_win, g=g),
        grid=(n_heads // g, batch, nq),
        in_specs=[
            pl.BlockSpec((tq, g * d), lambda h, b, i: (b * nq + i, q_col // g + h)),
            pl.BlockSpec((seq, g * d), lambda h, b, i: (b, k_col // g + h)),
            pl.BlockSpec((seq, g * d), lambda h, b, i: (b, v_col // g + h)),
            pl.BlockSpec((g, tq, tq + left), lambda h, b, i: (h, 0, 0)),
        ],
        out_specs=pl.BlockSpec((tq, g * d), lambda h, b, i: (b * nq + i, h)),
        out_shape=jax.ShapeDtypeStruct((batch * seq, n_heads * d), BF16),
        compiler_params=_params("parallel", "parallel", "arbitrary"),
        name="chunk_attention",
    )(qkv, qkv, qkv, table)


def rope_tables(seq):
    half = QK_ROPE // 2
    pos = jnp.arange(seq, dtype=F32)
    inv_freq = ROPE_THETA ** (-jnp.arange(0, QK_ROPE, 2, dtype=F32) / QK_ROPE)
    ang = pos[:, None] * inv_freq[None, :]
    cos, sin = jnp.cos(ang), jnp.sin(ang)
    z = lambda n: jnp.zeros((seq, n), F32)
    cos_t = jnp.concatenate([cos, cos, z(LANES - 2 * half)], axis=1)
    sin_a = jnp.concatenate([-sin, z(LANES - half)], axis=1)
    sin_b = jnp.concatenate([z(half), sin, z(LANES - 2 * half)], axis=1)
    return cos_t, sin_a, sin_b


def _rope(x, cos_t, sin_a, sin_b):
    half = QK_ROPE // 2
    return x * cos_t + pltpu.roll(x, LANES - half, 1) * sin_a + pltpu.roll(x, half, 1) * sin_b


def _mla_q_kernel(c_ref, g_ref, w_ref, cos_ref, sa_ref, sb_ref, o_ref, an_ref, *, hp):
    @pl.when(pl.program_id(1) == 0)
    def _():
        an_ref[...] = _rms(c_ref[...], g_ref[...]).astype(an_ref.dtype)

    hw = QK_NOPE + LANES
    res = _dot(an_ref[...], w_ref[...])
    cos_t, sin_a, sin_b = cos_ref[...], sa_ref[...], sb_ref[...]
    for hh in range(hp):
        c0 = hh * hw
        o_ref[:, c0:c0 + QK_NOPE] = res[:, c0:c0 + QK_NOPE].astype(o_ref.dtype)
        o_ref[:, c0 + QK_NOPE:c0 + hw] = _rope(res[:, c0 + QK_NOPE:c0 + hw],
                                               cos_t, sin_a, sin_b).astype(o_ref.dtype)


def mla_q(lat, g, w_q, tables, seq, n_heads, tm=512, hp=MLA_PROJ_HEADS_PER_STEP):
    t = lat.shape[0]
    kq = g.shape[0]
    hw = QK_NOPE + LANES
    npos = seq // tm
    tab_spec = pl.BlockSpec((tm, LANES), lambda i, j: (i % npos, 0))
    return pl.pallas_call(
        functools.partial(_mla_q_kernel, hp=hp),
        grid=(t // tm, n_heads // hp),
        in_specs=[
            pl.BlockSpec((tm, kq), lambda i, j: (i, 0)),
            pl.BlockSpec((1, kq), lambda i, j: (0, 0)),
            pl.BlockSpec((kq, hp * hw), lambda i, j: (0, j)),
            tab_spec, tab_spec, tab_spec,
        ],
        out_specs=pl.BlockSpec((tm, hp * hw), lambda i, j: (i, j)),
        out_shape=jax.ShapeDtypeStruct((t, n_heads * hw), BF16),
        scratch_shapes=[pltpu.VMEM((tm, kq), BF16)],
        compiler_params=_params("parallel", "arbitrary"),
        name="mla_q",
    )(lat, g.reshape(1, kq), w_q, *tables)


def _mla_kv_kernel(c_ref, kr_ref, g_ref, w_ref, cos_ref, sa_ref, sb_ref, k_ref, vt_ref,
                   an_ref, kr_scr, *, hp, tm, tk):
    @pl.when(pl.program_id(1) == 0)
    def _():
        an_ref[...] = _rms(c_ref[...], g_ref[...]).astype(an_ref.dtype)
        kr_scr[...] = _rope(kr_ref[...], cos_ref[...], sa_ref[...], sb_ref[...]).astype(kr_scr.dtype)

    hw = QK_NOPE + LANES
    wv = QK_NOPE + V_MLA
    res = _dot(an_ref[...], w_ref[...])
    kr = kr_scr[...]
    for hh in range(hp):
        k_ref[:, hh * hw:hh * hw + QK_NOPE] = res[:, hh * wv:hh * wv + QK_NOPE].astype(k_ref.dtype)
        k_ref[:, hh * hw + QK_NOPE:(hh + 1) * hw] = kr
        v = res[:, hh * wv + QK_NOPE:(hh + 1) * wv]
        for kt in range(tm // tk):
            vt_ref[kt, hh * V_MLA:(hh + 1) * V_MLA, :] = v[kt * tk:(kt + 1) * tk, :].T.astype(vt_ref.dtype)


def mla_kv(lat, g, w_kv, tables, seq, n_heads, ckv_col, kr_col, tm=512, hp=MLA_PROJ_HEADS_PER_STEP):
    t = lat.shape[0]
    kkv = g.shape[0]
    hw = QK_NOPE + LANES
    tk = ATTN_TILE
    assert tm % tk == 0
    npos = seq // tm
    tab_spec = pl.BlockSpec((tm, LANES), lambda i, j: (i % npos, 0))
    return pl.pallas_call(
        functools.partial(_mla_kv_kernel, hp=hp, tm=tm, tk=tk),
        grid=(t // tm, n_heads // hp),
        in_specs=[
            pl.BlockSpec((tm, kkv), lambda i, j: (i, ckv_col)),
            pl.BlockSpec((tm, LANES), lambda i, j: (i, kr_col)),
            pl.BlockSpec((1, kkv), lambda i, j: (0, 0)),
            pl.BlockSpec((kkv, hp * (QK_NOPE + V_MLA)), lambda i, j: (0, j)),
            tab_spec, tab_spec, tab_spec,
        ],
        out_specs=[pl.BlockSpec((tm, hp * hw), lambda i, j: (i, j)),
                   pl.BlockSpec((tm // tk, hp * V_MLA, tk), lambda i, j: (i, j, 0))],
        out_shape=[jax.ShapeDtypeStruct((t, n_heads * hw), BF16),
                   jax.ShapeDtypeStruct((t // tk, n_heads * V_MLA, tk), BF16)],
        scratch_shapes=[pltpu.VMEM((tm, kkv), BF16), pltpu.VMEM((tm, LANES), BF16)],
        compiler_params=_params("parallel", "arbitrary"),
        name="mla_kv",
    )(lat, lat, g.reshape(1, kkv), w_kv, *tables)


def _mla_attn_kernel(q_ref, k_ref, vt_ref, o_ref, *, tq, scale, g):
    qi = pl.program_id(2)
    hw = QK_NOPE + LANES
    dv = V_MLA
    qk_cols = [slice(hh * hw, (hh + 1) * hw) for hh in range(g)]
    v_cols = [slice(hh * dv, (hh + 1) * dv) for hh in range(g)]
    qs = [q_ref[:, cs] for cs in qk_cols]
    key, qry, _ = _chunk_causal(tq)
    visible = (key // CHUNK) <= (qry // CHUNK)

    def block(kb, carry, diag):
        start = pl.multiple_of(kb * tq, tq)
        ss = [_dot_nt(k_ref[pl.ds(start, tq), cs], q) * (scale * LOG2E) for q, cs in zip(qs, qk_cols)]
        if diag:
            ss = [jnp.where(visible, s, NEG) for s in ss]
        return tuple(_online_step_t(s, vt_ref[kb, vs, :], *st)
                     for s, vs, st in zip(ss, v_cols, carry))

    def body(kb, carry):
        return block(kb, carry, False)

    init = tuple((jnp.full((1, tq), NEG, F32), jnp.zeros((1, tq), F32), jnp.zeros((dv, tq), F32))
                 for _ in range(g))
    carry = lax.fori_loop(0, qi, body, init)
    carry = block(qi, carry, True)
    for vs, (_, l, acc) in zip(v_cols, carry):
        o_ref[:, vs] = (acc / l).T.astype(o_ref.dtype)


def mla_attention(q, k, vt, batch, seq, n_heads, g=MLA_HEADS_PER_STEP):
    tq = ATTN_TILE
    nq = seq // tq
    hw = QK_NOPE + LANES
    assert n_heads % g == 0
    return pl.pallas_call(
        functools.partial(_mla_attn_kernel, tq=tq, scale=1.0 / math.sqrt(QK_NOPE + QK_ROPE), g=g),
        grid=(batch, n_heads // g, nq),
        in_specs=[
            pl.BlockSpec((tq, g * hw), lambda b, h, i: (b * nq + i, h)),
            pl.BlockSpec((seq, g * hw), lambda b, h, i: (b, h), pipeline_mode=pl.Buffered(1)),
            pl.BlockSpec((nq, g * V_MLA, tq), lambda b, h, i: (b, h, 0), pipeline_mode=pl.Buffered(1)),
        ],
        out_specs=pl.BlockSpec((tq, g * V_MLA), lambda b, h, i: (b * nq + i, h)),
        out_shape=jax.ShapeDtypeStruct((batch * seq, n_heads * V_MLA), BF16),
        compiler_params=_params("parallel", "parallel", "arbitrary"),
        name="mla_attention",
    )(q, k, vt)


def _even_mixer(hn, w_in, w_out, i, diff_lambda, subln_g, batch, seq, layer):
    d_model = hn.shape[1]
    w_sb = d_model // 2
    n_sb = w_sb // HEAD_DIM
    n_diff = w_sb // (2 * HEAD_DIM)
    cb = w_sb // LANES
    qkv = matmul([(hn, w_in, i, 0)], w_in.shape[2], BF16, tm=1024, tn=512, name="even_in_proj")
    a = sb_attention(qkv, batch, seq, n_sb, 0, cb, 2 * cb)
    lam_init = 0.8 - 0.6 * math.exp(-0.3 * layer)
    bo = diff_attention(qkv, diff_lambda, subln_g, batch, seq, n_diff,
                        3 * cb // 2, 4 * cb // 2, 5 * cb // 2, lam_init)
    return matmul([(a, w_out, i, 0), (bo, w_out, i, 1)], d_model, F32, tm=1024, tn=512,
                  name="mix_out_proj")


def _odd_mixer(hn, w_in, w_out, i, rel_bias, q_norm_g, w_uq, kv_norm_g, w_ukv, batch, seq):
    d_model = hn.shape[1]
    w_ch = d_model // 2
    n_ch = w_ch // HEAD_DIM
    n_mla = w_ch // HEAD_DIM
    cb = w_ch // LANES
    q_lora, kv_lora = q_norm_g.shape[0], kv_norm_g.shape[0]
    n_attn = 3 * w_ch
    n_lat = q_lora + kv_lora + QK_ROPE
    lat_pad = -n_lat % LANES
    w_lat = jnp.pad(w_in[i:i + 1, :, n_attn:], ((0, 0), (0, 0), (0, lat_pad))).astype(BF16)
    qkv = matmul([(hn, w_in, i, 0)], n_attn, BF16, tm=1024, tn=512, name="odd_in_proj")
    lat = matmul([(hn, w_lat, 0, 0)], n_lat + lat_pad, F32, tm=512, tn=n_lat + lat_pad,
                 name="odd_lat_proj")
    c = chunk_attention(qkv, rel_bias, batch, seq, n_ch, 0, cb, 2 * cb)

    tables = rope_tables(seq)
    hw = QK_NOPE + LANES
    wq = w_uq.reshape(q_lora, n_mla, QK_NOPE + QK_ROPE)
    wq = jnp.pad(wq, ((0, 0), (0, 0), (0, hw - QK_NOPE - QK_ROPE))).reshape(q_lora, n_mla * hw).astype(BF16)
    q = mla_q(lat, q_norm_g, wq, tables, seq, n_mla)
    assert q_lora % kv_lora == 0 and (q_lora + kv_lora) % LANES == 0
    k, v = mla_kv(lat, kv_norm_g, w_ukv.astype(BF16), tables, seq, n_mla,
                  q_lora // kv_lora, (q_lora + kv_lora) // LANES)
    dm = mla_attention(q, k, v, batch, seq, n_mla)
    return matmul([(c, w_out, i, 0), (dm, w_out, i, 1)], d_model, F32, tm=1024, tn=512,
                  name="mix_out_proj")


def _ffn(h, w_in, conv_w, conv_b, w_out, layer, seq):
    g = ffn_in(h, w_in, conv_w, conv_b, layer, seq)
    return matmul([(g, w_out.astype(BF16), layer, 0)], w_out.shape[2], F32, tm=512, tn=256,
                  name="ffn_out_proj")


def kernel(x, norm_g, even_w_in, even_w_out, diff_lambda, diff_subln_g, odd_w_in, odd_w_out,
           ch_rel_bias, mla_q_norm_g, mla_w_uq, mla_kv_norm_g, mla_w_ukv, ffn_w_in, ffn_conv_w,
           ffn_conv_b, ffn_w_out):
    batch, seq, d_model = x.shape
    depth = norm_g.shape[0]
    xf = x.reshape(batch * seq, d_model)
    hn = norm_cast(xf, norm_g[0, 0])
    for layer in range(depth):
        g = norm_g[layer]
        i = layer // 2
        if layer % 2 == 0:
            mix = _even_mixer(hn, even_w_in, even_w_out, i, diff_lambda[i], diff_subln_g[i],
                              batch, seq, layer)
        else:
            mix = _odd_mixer(hn, odd_w_in, odd_w_out, i, ch_rel_bias[i], mla_q_norm_g[i],
                             mla_w_uq[i], mla_kv_norm_g[i], mla_w_ukv[i], batch, seq)
        xf, h2 = resid_norm(xf, mix, g[1], g[2])
        f = _ffn(h2, ffn_w_in, ffn_conv_w, ffn_conv_b, ffn_w_out, layer, seq)
        if layer + 1 < depth:
            xf, hn = resid_norm(xf, f, g[3], norm_g[layer + 1, 0])
        else:
            xf = resid(xf, f, g[3])
    return xf.reshape(batch, seq, d_model)
```

```python
import functools
import math

import jax
import jax.numpy as jnp
from jax import lax
from jax.experimental import pallas as pl
from jax.experimental.pallas import tpu as pltpu

F32 = jnp.float32
BF16 = jnp.bfloat16

CHUNK = 64
HEAD_DIM = 128
LEFT_CHUNKS = 8
REL_CLIP = 128
QK_NOPE = 128
QK_ROPE = 64
V_MLA = 128
ROPE_THETA = 10000.0
CONV_W = 3
EPS = 1e-6
NEG = -1e30
LOG2E = math.log2(math.e)

LANES = 128
BF16_SUBLANES = 16
VMEM_LIMIT = 52 * 1024 * 1024

ATTN_TILE = 256
SB_HEADS_PER_STEP = 8
MLA_HEADS_PER_STEP = 8
DIFF_HEADS_PER_STEP = 4
CHUNK_HEADS_PER_STEP = 4
MLA_PROJ_HEADS_PER_STEP = 4


def _params(*sem):
    return pltpu.CompilerParams(dimension_semantics=sem, vmem_limit_bytes=VMEM_LIMIT)


def _rms(x, g):
    ms = jnp.mean(x * x, axis=-1, keepdims=True)
    return x * lax.rsqrt(ms + EPS) * g


def _dot(a, b):
    return jnp.dot(a, b, preferred_element_type=F32)


def _dot_nt(a, b):
    return lax.dot_general(a, b, (((1,), (1,)), ((), ())), preferred_element_type=F32)


def _norm_cast_kernel(x_ref, g_ref, h_ref):
    h_ref[...] = _rms(x_ref[...], g_ref[...]).astype(h_ref.dtype)


def _resid_norm_kernel(x_ref, y_ref, g1_ref, g2_ref, xo_ref, h_ref):
    xn = x_ref[...] + _rms(y_ref[...], g1_ref[...])
    xo_ref[...] = xn
    h_ref[...] = _rms(xn, g2_ref[...]).astype(h_ref.dtype)


def _resid_kernel(x_ref, y_ref, g_ref, xo_ref):
    xo_ref[...] = x_ref[...] + _rms(y_ref[...], g_ref[...])


def _row_spec(tr, d):
    return pl.BlockSpec((tr, d), lambda i: (i, 0))


def _vec_spec(d):
    return pl.BlockSpec((1, d), lambda i: (0, 0))


def norm_cast(x, g, tr=256):
    t, d = x.shape
    return pl.pallas_call(
        _norm_cast_kernel,
        grid=(t // tr,),
        in_specs=[_row_spec(tr, d), _vec_spec(d)],
        out_specs=_row_spec(tr, d),
        out_shape=jax.ShapeDtypeStruct((t, d), BF16),
        compiler_params=_params("parallel"),
        name="norm_cast",
    )(x, g.reshape(1, d))


def resid_norm(x, y, g1, g2, tr=256):
    t, d = x.shape
    return pl.pallas_call(
        _resid_norm_kernel,
        grid=(t // tr,),
        in_specs=[_row_spec(tr, d), _row_spec(tr, d), _vec_spec(d), _vec_spec(d)],
        out_specs=[_row_spec(tr, d), _row_spec(tr, d)],
        out_shape=[jax.ShapeDtypeStruct((t, d), F32), jax.ShapeDtypeStruct((t, d), BF16)],
        compiler_params=_params("parallel"),
        name="resid_norm",
    )(x, y, g1.reshape(1, d), g2.reshape(1, d))


def resid(x, y, g, tr=256):
    t, d = x.shape
    return pl.pallas_call(
        _resid_kernel,
        grid=(t // tr,),
        in_specs=[_row_spec(tr, d), _row_spec(tr, d), _vec_spec(d)],
        out_specs=_row_spec(tr, d),
        out_shape=jax.ShapeDtypeStruct((t, d), F32),
        compiler_params=_params("parallel"),
        name="resid",
    )(x, y, g.reshape(1, d))


def _matmul_kernel(*refs, n_pairs):
    o_ref = refs[2 * n_pairs]
    acc = _dot(refs[0][...], refs[n_pairs][...].astype(BF16))
    for p in range(1, n_pairs):
        acc = acc + _dot(refs[p][...], refs[n_pairs + p][...].astype(BF16))
    o_ref[...] = acc.astype(o_ref.dtype)


def matmul(pairs, n, out_dtype, tm, tn, name):
    m = pairs[0][0].shape[0]
    n_pairs = len(pairs)
    a_specs, w_specs, args_a, args_w = [], [], [], []
    for a, w, layer, rb in pairs:
        k = a.shape[1]
        a_specs.append(pl.BlockSpec((tm, k), lambda i, j: (i, 0)))
        w_specs.append(pl.BlockSpec((None, k, tn), lambda i, j, layer=layer, rb=rb: (layer, rb, j)))
        args_a.append(a)
        args_w.append(w)
    return pl.pallas_call(
        functools.partial(_matmul_kernel, n_pairs=n_pairs),
        grid=(m // tm, n // tn),
        in_specs=a_specs + w_specs,
        out_specs=pl.BlockSpec((tm, tn), lambda i, j: (i, j)),
        out_shape=jax.ShapeDtypeStruct((m, n), out_dtype),
        compiler_params=_params("parallel", "arbitrary"),
        name=name,
    )(*args_a, *args_w)


def _column_slab_kernel(w_ref, o_ref, *, col0, n_cols):
    col = col0 + pl.program_id(0) * LANES + lax.broadcasted_iota(jnp.int32, w_ref.shape, 1)
    o_ref[...] = jnp.where(col < n_cols, w_ref[...], 0.0).astype(o_ref.dtype)


def column_slab_bf16(w, layer, col0, width):
    _, k, n_cols = w.shape
    assert col0 % LANES == 0 and width % LANES == 0
    return pl.pallas_call(
        functools.partial(_column_slab_kernel, col0=col0, n_cols=n_cols),
        grid=(width // LANES,),
        in_specs=[pl.BlockSpec((None, k, LANES), lambda j: (layer, 0, col0 // LANES + j))],
        out_specs=pl.BlockSpec((None, k, LANES), lambda j: (0, 0, j)),
        out_shape=jax.ShapeDtypeStruct((1, k, width), BF16),
        compiler_params=_params("parallel"),
        name="column_slab_bf16",
    )(w)


def _gelu_tanh(x):
    c = math.sqrt(2.0 / math.pi)
    return x * (0.5 * (1.0 + jnp.tanh(c * (x + 0.044715 * (x * x * x)))))


def _ffn_in_kernel(a_ref, ah_ref, wg_ref, wv_ref, cwg_ref, cwv_ref, cbg_ref, cbv_ref, wo_ref,
                   o_ref, wo16_ref, *, tm, seq):
    i = pl.program_id(0)
    wo16_ref[...] = wo_ref[...].astype(wo16_ref.dtype)
    tn = o_ref.shape[1]
    a = a_ref[...]
    ah = ah_ref[...]
    seq_start = (i * tm) % seq == 0
    row8 = lax.broadcasted_iota(jnp.int32, (8, tn), 0)

    def conv(w_ref, cw_ref, cb_ref):
        w = w_ref[...].astype(BF16)
        u = _dot(a, w)
        uh = _dot(ah, w)
        uh = jnp.where(seq_start, 0.0, uh)
        p1 = uh[BF16_SUBLANES - 1:BF16_SUBLANES, :]
        p2 = uh[BF16_SUBLANES - 2:BF16_SUBLANES - 1, :]
        r1 = pltpu.roll(u, 1, 0)
        r2 = pltpu.roll(u, 2, 0)
        h1 = jnp.where(row8 == 0, p1, r1[:8, :])
        h2 = jnp.where(row8 == 0, p2, jnp.where(row8 == 1, p1, r2[:8, :]))
        u1 = jnp.concatenate([h1, r1[8:, :]], axis=0)
        u2 = jnp.concatenate([h2, r2[8:, :]], axis=0)
        cw = cw_ref[...]
        return cb_ref[...] + (cw[0:1, :] * u2 + cw[1:2, :] * u1 + cw[2:3, :] * u)

    gate = conv(wg_ref, cwg_ref, cbg_ref)
    val = conv(wv_ref, cwv_ref, cbv_ref)
    o_ref[...] = (_gelu_tanh(gate) * val).astype(o_ref.dtype)


def ffn_in(h, w_in, conv_w, conv_b, w_out, layer, seq, tm=1024, tn=256):
    t, k = h.shape
    f = w_in.shape[2] // 2
    d_out = w_out.shape[2]
    nf = f // tn
    n_steps = (t // tm) * nf
    assert f % (n_steps * BF16_SUBLANES) == 0
    slab = f // n_steps
    halo = BF16_SUBLANES
    hb = tm // halo
    conv_b = conv_b.reshape(conv_b.shape[0], 1, 2 * f)
    return pl.pallas_call(
        functools.partial(_ffn_in_kernel, tm=tm, seq=seq),
        grid=(t // tm, nf),
        in_specs=[
            pl.BlockSpec((tm, k), lambda i, j: (i, 0)),
            pl.BlockSpec((halo, k), lambda i, j: (jnp.maximum(i * hb - 1, 0), 0)),
            pl.BlockSpec((None, k, tn), lambda i, j: (layer, 0, j)),
            pl.BlockSpec((None, k, tn), lambda i, j: (layer, 0, j + nf)),
            pl.BlockSpec((None, CONV_W, tn), lambda i, j: (layer, 0, j)),
            pl.BlockSpec((None, CONV_W, tn), lambda i, j: (layer, 0, j + nf)),
            pl.BlockSpec((None, 1, tn), lambda i, j: (layer, 0, j)),
            pl.BlockSpec((None, 1, tn), lambda i, j: (layer, 0, j + nf)),
            pl.BlockSpec((None, slab, d_out), lambda i, j: (layer, i * nf + j, 0)),
        ],
        out_specs=[pl.BlockSpec((tm, tn), lambda i, j: (i, j)),
                   pl.BlockSpec((slab, d_out), lambda i, j: (i * nf + j, 0))],
        out_shape=[jax.ShapeDtypeStruct((t, f), BF16),
                   jax.ShapeDtypeStruct((f, d_out), BF16)],
        compiler_params=_params("parallel", "arbitrary"),
        name="ffn_in",
    )(h, h, w_in, w_in, conv_w, conv_w, conv_b, conv_b, w_out)


def _transpose_value_tiles(v_ref, vt_scr, tq):
    def transpose_tile(kb, c):
        start = pl.multiple_of(kb * tq, tq)
        vt_scr[kb] = v_ref[pl.ds(start, tq), :].astype(F32).T.astype(vt_scr.dtype)
        return c
    lax.fori_loop(0, vt_scr.shape[0], transpose_tile, 0)


def _sb_kernel(q_ref, k_ref, v_ref, o_ref, vt_scr, *, tq, scale, g):
    qi = pl.program_id(2)
    d = HEAD_DIM
    heads = [slice(hh * d, (hh + 1) * d) for hh in range(g)]
    qs = [q_ref[:, cs] for cs in heads]

    @pl.when(qi == 0)
    def _():
        _transpose_value_tiles(v_ref, vt_scr, tq)

    key = lax.broadcasted_iota(jnp.int32, (tq, tq), 0)
    qry = lax.broadcasted_iota(jnp.int32, (tq, tq), 1)
    strict = key < qry
    r2 = lax.broadcasted_iota(jnp.int32, (tq, 2 * tq), 0)
    c2 = lax.broadcasted_iota(jnp.int32, (tq, 2 * tq), 1)
    tri = (jnp.where(c2 >= tq, c2 - tq, c2) >= r2).astype(BF16)

    def block(kb, carry, diag):
        start = pl.multiple_of(kb * tq, tq)
        zs = [_dot_nt(k_ref[pl.ds(start, tq), cs], q) * (scale * LOG2E) for q, cs in zip(qs, heads)]
        csums = []
        for z in zs:
            neg_abs = lax.bitcast_convert_type(
                lax.bitcast_convert_type(z, jnp.uint32) | jnp.uint32(0x80000000), F32)
            sp = jnp.maximum(z, 0.0) + jnp.log2(1.0 + jnp.exp2(neg_abs))
            if diag:
                sp = jnp.where(strict, sp, 0.0)
            hi = sp.astype(BF16)
            lo = (sp - hi.astype(F32)).astype(BF16)
            csums.append(_dot(tri, jnp.concatenate([hi, lo], axis=0)))
        out = []
        for z, csum, cs, (rsum, acc) in zip(zs, csums, heads, carry):
            w = jnp.exp2(z - csum - rsum)
            if diag:
                w = jnp.where(strict, w, 0.0)
            acc = acc + _dot(vt_scr[kb, cs, :], w.astype(BF16))
            rsum = rsum + csum[0:1, :]
            out.append((rsum, acc))
        return tuple(out)

    init = tuple((jnp.zeros((1, tq), F32), jnp.zeros((d, tq), F32)) for _ in range(g))
    carry = block(qi, init, True)

    def body(it, carry):
        return block(qi - 1 - it, carry, False)

    carry = lax.fori_loop(0, qi, body, carry)
    for cs, (_, acc) in zip(heads, carry):
        o_ref[:, cs] = acc.T.astype(o_ref.dtype)


def sb_attention(qkv, batch, seq, n_heads, q_col, k_col, v_col, g=SB_HEADS_PER_STEP):
    tq = ATTN_TILE
    nq = seq // tq
    d = HEAD_DIM
    assert n_heads % g == 0 and q_col % g == 0 and k_col % g == 0 and v_col % g == 0
    return pl.pallas_call(
        functools.partial(_sb_kernel, tq=tq, scale=1.0 / math.sqrt(d), g=g),
        grid=(batch, n_heads // g, nq),
        in_specs=[
            pl.BlockSpec((tq, g * d), lambda b, h, i: (b * nq + i, q_col // g + h)),
            pl.BlockSpec((seq, g * d), lambda b, h, i: (b, k_col // g + h)),
            pl.BlockSpec((seq, g * d), lambda b, h, i: (b, v_col // g + h)),
        ],
        out_specs=pl.BlockSpec((tq, g * d), lambda b, h, i: (b * nq + i, h)),
        out_shape=jax.ShapeDtypeStruct((batch * seq, n_heads * d), BF16),
        scratch_shapes=[pltpu.VMEM((nq, g * d, tq), BF16)],
        compiler_params=_params("parallel", "parallel", "arbitrary"),
        name="sb_attention",
    )(qkv, qkv, qkv)


def _online_step_t(s, vt, m, l, acc):
    m_new = jnp.maximum(m, jnp.max(s, axis=0, keepdims=True))
    alpha = jnp.exp2(m - m_new)
    p = jnp.exp2(s - m_new)
    l = alpha * l + jnp.sum(p, axis=0, keepdims=True)
    acc = alpha * acc + _dot(vt, p.astype(BF16))
    return m_new, l, acc


def _chunk_causal(tq):
    row = lax.broadcasted_iota(jnp.int32, (tq, tq), 0)
    col = lax.broadcasted_iota(jnp.int32, (tq, tq), 1)
    return row, col, (col // CHUNK) <= (row // CHUNK)


def _diff_kernel(q_ref, k_ref, v_ref, lam_ref, g_ref, o_ref, vt_scr, *, tq, scale, lam_init, g):
    hg = pl.program_id(1)
    qi = pl.program_id(2)
    d = HEAD_DIM
    dv = 2 * HEAD_DIM
    qk_cols = [slice(c * d, (c + 1) * d) for c in range(2 * g)]
    v_cols = [slice(hh * dv, (hh + 1) * dv) for hh in range(g)]
    qs = [q_ref[:, cs] for cs in qk_cols]

    @pl.when(qi == 0)
    def _():
        _transpose_value_tiles(v_ref, vt_scr, tq)

    key, qry, _ = _chunk_causal(tq)
    visible = (key // CHUNK) <= (qry // CHUNK)
    dist = (qry - key).astype(F32)
    nbias_diag, nbias_off, slopes = [], [], []
    for hh in range(g):
        slope = lax.bitcast_convert_type(
            jnp.full((1, 1), (126 - (hg * g + hh)) * (1 << 23), jnp.int32), F32)
        slope = slope * LOG2E
        slopes.append(slope)
        nbias_diag.append(-slope * jnp.abs(dist))
        nbias_off.append(-slope * dist)

    def block(kb, carry, diag):
        start = pl.multiple_of(kb * tq, tq)
        ss = [_dot_nt(k_ref[pl.ds(start, tq), cs], q) * (scale * LOG2E) for q, cs in zip(qs, qk_cols)]
        for c in range(2 * g):
            hh = c // 2
            if diag:
                ss[c] = jnp.where(visible, ss[c] + nbias_diag[hh], NEG)
            else:
                ss[c] = ss[c] + (nbias_off[hh] - slopes[hh] * ((qi - kb) * tq).astype(F32))
        return tuple(_online_step_t(ss[c], vt_scr[kb, v_cols[c // 2], :], *carry[c])
                     for c in range(2 * g))

    def body(kb, carry):
        return block(kb, carry, False)

    init = tuple((jnp.full((1, tq), NEG, F32), jnp.zeros((1, tq), F32), jnp.zeros((dv, tq), F32))
                 for _ in range(2 * g))
    carry = lax.fori_loop(0, qi, body, init)
    carry = block(qi, carry, True)

    lf = lam_ref[...]
    lam = (jnp.exp(jnp.sum(lf[0:1, :] * lf[1:2, :], axis=-1, keepdims=True))
           - jnp.exp(jnp.sum(lf[2:3, :] * lf[3:4, :], axis=-1, keepdims=True)) + lam_init)
    for hh in range(g):
        (_, l1, a1), (_, l2, a2) = carry[2 * hh], carry[2 * hh + 1]
        o = (a1 / l1 - lam * (a2 / l2)).T
        o_ref[:, v_cols[hh]] = (_rms(o, g_ref[...]) * (1.0 - lam_init)).astype(o_ref.dtype)


def diff_attention(qkv, diff_lambda, subln_g, batch, seq, n_heads, q_col, k_col, v_col, lam_init,
                   g=DIFF_HEADS_PER_STEP):
    tq = ATTN_TILE
    nq = seq // tq
    dv = 2 * HEAD_DIM
    assert n_heads == 8, "ALiBi slopes are built as exact powers of two"
    assert n_heads % g == 0 and q_col % g == 0 and k_col % g == 0 and v_col % g == 0
    return pl.pallas_call(
        functools.partial(_diff_kernel, tq=tq, scale=1.0 / math.sqrt(HEAD_DIM), lam_init=lam_init, g=g),
        grid=(batch, n_heads // g, nq),
        in_specs=[
            pl.BlockSpec((tq, g * dv), lambda b, h, i: (b * nq + i, q_col // g + h)),
            pl.BlockSpec((seq, g * dv), lambda b, h, i: (b, k_col // g + h)),
            pl.BlockSpec((seq, g * dv), lambda b, h, i: (b, v_col // g + h)),
            pl.BlockSpec((4, HEAD_DIM), lambda b, h, i: (0, 0)),
            pl.BlockSpec((1, dv), lambda b, h, i: (0, 0)),
        ],
        out_specs=pl.BlockSpec((tq, g * dv), lambda b, h, i: (b * nq + i, h)),
        out_shape=jax.ShapeDtypeStruct((batch * seq, n_heads * dv), BF16),
        scratch_shapes=[pltpu.VMEM((nq, g * dv, tq), BF16)],
        compiler_params=_params("parallel", "parallel", "arbitrary"),
        name="diff_attention",
    )(qkv, qkv, qkv, diff_lambda, subln_g.reshape(1, dv))


def _chunk_kernel(q_ref, k_ref, v_ref, tab_ref, o_ref, *, tq, scale, n_win, g):
    qi = pl.program_id(2)
    d = HEAD_DIM
    heads = [slice(hh * d, (hh + 1) * d) for hh in range(g)]
    kbs = [qi - (n_win - 1) + w for w in range(n_win)]
    starts = [pl.multiple_of(jnp.maximum(kb, 0) * tq, tq) for kb in kbs]
    scores = []
    for hh, cs in enumerate(heads):
        q = q_ref[:, cs]
        s_blocks = []
        for w in range(n_win):
            s = (_dot_nt(q, k_ref[pl.ds(starts[w], tq), cs]) * (scale * LOG2E)
                 + tab_ref[hh, :, w * tq:(w + 1) * tq])
            if w < n_win - 1:
                s = jnp.where(kbs[w] >= 0, s, NEG)
            s_blocks.append(s)
        scores.append(s_blocks)
    for cs, s_blocks in zip(heads, scores):
        m = s_blocks[0].max(axis=-1, keepdims=True)
        for s in s_blocks[1:]:
            m = jnp.maximum(m, s.max(axis=-1, keepdims=True))
        l = jnp.zeros_like(m)
        acc = jnp.zeros((tq, d), F32)
        for w, s in enumerate(s_blocks):
            p = jnp.exp2(s - m)
            l = l + jnp.sum(p, axis=-1, keepdims=True)
            acc = acc + _dot(p.astype(BF16), v_ref[pl.ds(starts[w], tq), cs])
        o_ref[:, cs] = (acc / l).astype(o_ref.dtype)


def chunk_bias_table(rel_bias, tq):
    left = LEFT_CHUNKS * CHUNK
    n_heads = rel_bias.shape[0]
    width = tq + left
    period = 2 * left
    assert left >= REL_CLIP and tq <= left
    rb = rel_bias.astype(F32)
    hi = jnp.broadcast_to(rb[:, -1:], (n_heads, left - REL_CLIP))
    mid = rb[:, ::-1]
    lo = jnp.broadcast_to(rb[:, :1], (n_heads, tq - REL_CLIP - 1))
    neg_d = jnp.broadcast_to(rb[:, -1:], (n_heads, period - width))
    diag = jnp.concatenate([hi, mid, lo, neg_d], axis=1)
    flat = jnp.tile(diag, (1, tq))[:, :tq * (period - 1)]
    bias = flat.reshape(n_heads, tq, period - 1)[:, :, :width]
    i = jnp.arange(tq)[:, None]
    j = jnp.arange(width)[None, :]
    qc, kc = i // CHUNK, j // CHUNK
    allowed = (kc >= qc) & (kc <= qc + LEFT_CHUNKS)
    return jnp.where(allowed[None], bias * LOG2E, NEG)


def chunk_attention(qkv, rel_bias, batch, seq, n_heads, q_col, k_col, v_col, g=CHUNK_HEADS_PER_STEP):
    tq = ATTN_TILE
    nq = seq // tq
    d = HEAD_DIM
    left = LEFT_CHUNKS * CHUNK
    assert left % tq == 0
    assert n_heads % g == 0 and q_col % g == 0 and k_col % g == 0 and v_col % g == 0
    n_win = left // tq + 1
    table = chunk_bias_table(rel_bias, tq)
    return pl.pallas_call(
        functools.partial(_chunk_kernel, tq=tq, scale=1.0 / math.sqrt(d), n_win=n_win, g=g),
        grid=(n_heads // g, batch, nq),
        in_specs=[
            pl.BlockSpec((tq, g * d), lambda h, b, i: (b * nq + i, q_col // g + h)),
            pl.BlockSpec((seq, g * d), lambda h, b, i: (b, k_col // g + h)),
            pl.BlockSpec((seq, g * d), lambda h, b, i: (b, v_col // g + h)),
            pl.BlockSpec((g, tq, tq + left), lambda h, b, i: (h, 0, 0)),
        ],
        out_specs=pl.BlockSpec((tq, g * d), lambda h, b, i: (b * nq + i, h)),
        out_shape=jax.ShapeDtypeStruct((batch * seq, n_heads * d), BF16),
        compiler_params=_params("parallel", "parallel", "arbitrary"),
        name="chunk_attention",
    )(qkv, qkv, qkv, table)


def rope_tables(seq):
    half = QK_ROPE // 2
    pos = jnp.arange(seq, dtype=F32)
    inv_freq = ROPE_THETA ** (-jnp.arange(0, QK_ROPE, 2, dtype=F32) / QK_ROPE)
    ang = pos[:, None] * inv_freq[None, :]
    cos, sin = jnp.cos(ang), jnp.sin(ang)
    z = lambda n: jnp.zeros((seq, n), F32)
    cos_t = jnp.concatenate([cos, cos, z(LANES - 2 * half)], axis=1)
    sin_a = jnp.concatenate([-sin, z(LANES - half)], axis=1)
    sin_b = jnp.concatenate([z(half), sin, z(LANES - 2 * half)], axis=1)
    return cos_t, sin_a, sin_b


def _rope(x, cos_t, sin_a, sin_b):
    half = QK_ROPE // 2
    return x * cos_t + pltpu.roll(x, LANES - half, 1) * sin_a + pltpu.roll(x, half, 1) * sin_b


def _mla_q_kernel(c_ref, g_ref, w_ref, cos_ref, sa_ref, sb_ref, o_ref, an_ref, *, hp):
    @pl.when(pl.program_id(1) == 0)
    def _():
        an_ref[...] = _rms(c_ref[...], g_ref[...]).astype(an_ref.dtype)

    hw = QK_NOPE + LANES
    res = _dot(an_ref[...], w_ref[...])
    cos_t, sin_a, sin_b = cos_ref[...], sa_ref[...], sb_ref[...]
    for hh in range(hp):
        c0 = hh * hw
        o_ref[:, c0:c0 + QK_NOPE] = res[:, c0:c0 + QK_NOPE].astype(o_ref.dtype)
        o_ref[:, c0 + QK_NOPE:c0 + hw] = _rope(res[:, c0 + QK_NOPE:c0 + hw],
                                               cos_t, sin_a, sin_b).astype(o_ref.dtype)


def mla_q(lat, g, w_q, tables, seq, n_heads, tm=512, hp=MLA_PROJ_HEADS_PER_STEP):
    t = lat.shape[0]
    kq = g.shape[0]
    hw = QK_NOPE + LANES
    npos = seq // tm
    tab_spec = pl.BlockSpec((tm, LANES), lambda i, j: (i % npos, 0))
    return pl.pallas_call(
        functools.partial(_mla_q_kernel, hp=hp),
        grid=(t // tm, n_heads // hp),
        in_specs=[
            pl.BlockSpec((tm, kq), lambda i, j: (i, 0)),
            pl.BlockSpec((1, kq), lambda i, j: (0, 0)),
            pl.BlockSpec((kq, hp * hw), lambda i, j: (0, j)),
            tab_spec, tab_spec, tab_spec,
        ],
        out_specs=pl.BlockSpec((tm, hp * hw), lambda i, j: (i, j)),
        out_shape=jax.ShapeDtypeStruct((t, n_heads * hw), BF16),
        scratch_shapes=[pltpu.VMEM((tm, kq), BF16)],
        compiler_params=_params("parallel", "arbitrary"),
        name="mla_q",
    )(lat, g.reshape(1, kq), w_q, *tables)


def _mla_kv_kernel(c_ref, kr_ref, g_ref, w_ref, cos_ref, sa_ref, sb_ref, k_ref, vt_ref,
                   an_ref, kr_scr, *, hp, tm, tk):
    @pl.when(pl.program_id(1) == 0)
    def _():
        an_ref[...] = _rms(c_ref[...], g_ref[...]).astype(an_ref.dtype)
        kr_scr[...] = _rope(kr_ref[...], cos_ref[...], sa_ref[...], sb_ref[...]).astype(kr_scr.dtype)

    hw = QK_NOPE + LANES
    wv = QK_NOPE + V_MLA
    res = _dot(an_ref[...], w_ref[...])
    kr = kr_scr[...]
    for hh in range(hp):
        k_ref[:, hh * hw:hh * hw + QK_NOPE] = res[:, hh * wv:hh * wv + QK_NOPE].astype(k_ref.dtype)
        k_ref[:, hh * hw + QK_NOPE:(hh + 1) * hw] = kr
        v = res[:, hh * wv + QK_NOPE:(hh + 1) * wv]
        for kt in range(tm // tk):
            vt_ref[kt, hh * V_MLA:(hh + 1) * V_MLA, :] = v[kt * tk:(kt + 1) * tk, :].T.astype(vt_ref.dtype)


def mla_kv(lat, g, w_kv, tables, seq, n_heads, ckv_col, kr_col, tm=512, hp=MLA_PROJ_HEADS_PER_STEP):
    t = lat.shape[0]
    kkv = g.shape[0]
    hw = QK_NOPE + LANES
    tk = ATTN_TILE
    assert tm % tk == 0
    npos = seq // tm
    tab_spec = pl.BlockSpec((tm, LANES), lambda i, j: (i % npos, 0))
    return pl.pallas_call(
        functools.partial(_mla_kv_kernel, hp=hp, tm=tm, tk=tk),
        grid=(t // tm, n_heads // hp),
        in_specs=[
            pl.BlockSpec((tm, kkv), lambda i, j: (i, ckv_col)),
            pl.BlockSpec((tm, LANES), lambda i, j: (i, kr_col)),
            pl.BlockSpec((1, kkv), lambda i, j: (0, 0)),
            pl.BlockSpec((kkv, hp * (QK_NOPE + V_MLA)), lambda i, j: (0, j)),
            tab_spec, tab_spec, tab_spec,
        ],
        out_specs=[pl.BlockSpec((tm, hp * hw), lambda i, j: (i, j)),
                   pl.BlockSpec((tm // tk, hp * V_MLA, tk), lambda i, j: (i, j, 0))],
        out_shape=[jax.ShapeDtypeStruct((t, n_heads * hw), BF16),
                   jax.ShapeDtypeStruct((t // tk, n_heads * V_MLA, tk), BF16)],
        scratch_shapes=[pltpu.VMEM((tm, kkv), BF16), pltpu.VMEM((tm, LANES), BF16)],
        compiler_params=_params("parallel", "arbitrary"),
        name="mla_kv",
    )(lat, lat, g.reshape(1, kkv), w_kv, *tables)


def _mla_attn_kernel(q_ref, k_ref, vt_ref, o_ref, *, tq, scale, g):
    qi = pl.program_id(2)
    hw = QK_NOPE + LANES
    dv = V_MLA
    qk_cols = [slice(hh * hw, (hh + 1) * hw) for hh in range(g)]
    v_cols = [slice(hh * dv, (hh + 1) * dv) for hh in range(g)]
    qs = [q_ref[:, cs] for cs in qk_cols]
    key, qry, _ = _chunk_causal(tq)
    visible = (key // CHUNK) <= (qry // CHUNK)

    def block(kb, carry, diag):
        start = pl.multiple_of(kb * tq, tq)
        ss = [_dot_nt(k_ref[pl.ds(start, tq), cs], q) * (scale * LOG2E) for q, cs in zip(qs, qk_cols)]
        if diag:
            ss = [jnp.where(visible, s, NEG) for s in ss]
        return tuple(_online_step_t(s, vt_ref[kb, vs, :], *st)
                     for s, vs, st in zip(ss, v_cols, carry))

    def body(kb, carry):
        return block(kb, carry, False)

    init = tuple((jnp.full((1, tq), NEG, F32), jnp.zeros((1, tq), F32), jnp.zeros((dv, tq), F32))
                 for _ in range(g))
    carry = lax.fori_loop(0, qi, body, init)
    carry = block(qi, carry, True)
    for vs, (_, l, acc) in zip(v_cols, carry):
        o_ref[:, vs] = (acc / l).T.astype(o_ref.dtype)


def mla_attention(q, k, vt, batch, seq, n_heads, g=MLA_HEADS_PER_STEP):
    tq = ATTN_TILE
    nq = seq // tq
    hw = QK_NOPE + LANES
    assert n_heads % g == 0
    return pl.pallas_call(
        functools.partial(_mla_attn_kernel, tq=tq, scale=1.0 / math.sqrt(QK_NOPE + QK_ROPE), g=g),
        grid=(batch, n_heads // g, nq),
        in_specs=[
            pl.BlockSpec((tq, g * hw), lambda b, h, i: (b * nq + i, h)),
            pl.BlockSpec((seq, g * hw), lambda b, h, i: (b, h), pipeline_mode=pl.Buffered(1)),
            pl.BlockSpec((nq, g * V_MLA, tq), lambda b, h, i: (b, h, 0), pipeline_mode=pl.Buffered(1)),
        ],
        out_specs=pl.BlockSpec((tq, g * V_MLA), lambda b, h, i: (b * nq + i, h)),
        out_shape=jax.ShapeDtypeStruct((batch * seq, n_heads * V_MLA), BF16),
        compiler_params=_params("parallel", "parallel", "arbitrary"),
        name="mla_attention",
    )(q, k, vt)


def _even_mixer(hn, w_in, w_out, i, diff_lambda, subln_g, batch, seq, layer):
    d_model = hn.shape[1]
    w_sb = d_model // 2
    n_sb = w_sb // HEAD_DIM
    n_diff = w_sb // (2 * HEAD_DIM)
    cb = w_sb // LANES
    qkv = matmul([(hn, w_in, i, 0)], w_in.shape[2], BF16, tm=1024, tn=512, name="even_in_proj")
    a = sb_attention(qkv, batch, seq, n_sb, 0, cb, 2 * cb)
    lam_init = 0.8 - 0.6 * math.exp(-0.3 * layer)
    bo = diff_attention(qkv, diff_lambda, subln_g, batch, seq, n_diff,
                        3 * cb // 2, 4 * cb // 2, 5 * cb // 2, lam_init)
    return matmul([(a, w_out, i, 0), (bo, w_out, i, 1)], d_model, F32, tm=1024, tn=512,
                  name="mix_out_proj")


def _odd_mixer(hn, w_in, w_out, i, rel_bias, q_norm_g, w_uq, kv_norm_g, w_ukv, batch, seq):
    d_model = hn.shape[1]
    w_ch = d_model // 2
    n_ch = w_ch // HEAD_DIM
    n_mla = w_ch // HEAD_DIM
    cb = w_ch // LANES
    q_lora, kv_lora = q_norm_g.shape[0], kv_norm_g.shape[0]
    n_attn = 3 * w_ch
    n_lat = q_lora + kv_lora + QK_ROPE
    lat_pad = -n_lat % LANES
    w_lat = column_slab_bf16(w_in, i, n_attn, n_lat + lat_pad)
    qkv = matmul([(hn, w_in, i, 0)], n_attn, BF16, tm=1024, tn=512, name="odd_in_proj")
    lat = matmul([(hn, w_lat, 0, 0)], n_lat + lat_pad, F32, tm=512, tn=n_lat + lat_pad,
                 name="odd_lat_proj")
    c = chunk_attention(qkv, rel_bias, batch, seq, n_ch, 0, cb, 2 * cb)

    tables = rope_tables(seq)
    hw = QK_NOPE + LANES
    wq = w_uq.reshape(q_lora, n_mla, QK_NOPE + QK_ROPE)
    wq = jnp.pad(wq, ((0, 0), (0, 0), (0, hw - QK_NOPE - QK_ROPE))).reshape(q_lora, n_mla * hw).astype(BF16)
    q = mla_q(lat, q_norm_g, wq, tables, seq, n_mla)
    assert q_lora % kv_lora == 0 and (q_lora + kv_lora) % LANES == 0
    k, v = mla_kv(lat, kv_norm_g, w_ukv.astype(BF16), tables, seq, n_mla,
                  q_lora // kv_lora, (q_lora + kv_lora) // LANES)
    dm = mla_attention(q, k, v, batch, seq, n_mla)
    return matmul([(c, w_out, i, 0), (dm, w_out, i, 1)], d_model, F32, tm=1024, tn=512,
                  name="mix_out_proj")


def _ffn(h, w_in, conv_w, conv_b, w_out, layer, seq):
    g, w_out16 = ffn_in(h, w_in, conv_w, conv_b, w_out, layer, seq)
    return matmul([(g, w_out16[None], 0, 0)], w_out.shape[2], F32, tm=512, tn=256,
                  name="ffn_out_proj")


def kernel(x, norm_g, even_w_in, even_w_out, diff_lambda, diff_subln_g, odd_w_in, odd_w_out,
           ch_rel_bias, mla_q_norm_g, mla_w_uq, mla_kv_norm_g, mla_w_ukv, ffn_w_in, ffn_conv_w,
           ffn_conv_b, ffn_w_out):
    batch, seq, d_model = x.shape
    depth = norm_g.shape[0]
    xf = x.reshape(batch * seq, d_model)
    hn = norm_cast(xf, norm_g[0, 0])
    for layer in range(depth):
        g = norm_g[layer]
        i = layer // 2
        if layer % 2 == 0:
            mix = _even_mixer(hn, even_w_in, even_w_out, i, diff_lambda[i], diff_subln_g[i],
                              batch, seq, layer)
        else:
            mix = _odd_mixer(hn, odd_w_in, odd_w_out, i, ch_rel_bias[i], mla_q_norm_g[i],
                             mla_w_uq[i], mla_kv_norm_g[i], mla_w_ukv[i], batch, seq)
        xf, h2 = resid_norm(xf, mix, g[1], g[2])
        f = _ffn(h2, ffn_w_in, ffn_conv_w, ffn_conv_b, ffn_w_out, layer, seq)
        if layer + 1 < depth:
            xf, hn = resid_norm(xf, f, g[3], norm_g[layer + 1, 0])
        else:
            xf = resid(xf, f, g[3])
    return xf.reshape(batch, seq, d_model)
```

```python
import functools
import math

import jax
import jax.numpy as jnp
from jax import lax
from jax.experimental import pallas as pl
from jax.experimental.pallas import tpu as pltpu

F32 = jnp.float32
BF16 = jnp.bfloat16

CHUNK = 64
HEAD_DIM = 128
LEFT_CHUNKS = 8
REL_CLIP = 128
QK_NOPE = 128
QK_ROPE = 64
V_MLA = 128
ROPE_THETA = 10000.0
CONV_W = 3
EPS = 1e-6
NEG = -1e30
LOG2E = math.log2(math.e)

LANES = 128
BF16_SUBLANES = 16
VMEM_LIMIT = 52 * 1024 * 1024

ATTN_TILE = 256
SB_HEADS_PER_STEP = 8
MLA_HEADS_PER_STEP = 8
DIFF_HEADS_PER_STEP = 4
CHUNK_HEADS_PER_STEP = 4
MLA_PROJ_HEADS_PER_STEP = 4


def _params(*sem):
    return pltpu.CompilerParams(dimension_semantics=sem, vmem_limit_bytes=VMEM_LIMIT)


def _rms(x, g):
    ms = jnp.mean(x * x, axis=-1, keepdims=True)
    return x * lax.rsqrt(ms + EPS) * g


def _dot(a, b):
    return jnp.dot(a, b, preferred_element_type=F32)


def _dot_nt(a, b):
    return lax.dot_general(a, b, (((1,), (1,)), ((), ())), preferred_element_type=F32)


def _norm_cast_kernel(x_ref, g_ref, h_ref):
    h_ref[...] = _rms(x_ref[...], g_ref[...]).astype(h_ref.dtype)


def _resid_norm_kernel(x_ref, y_ref, g1_ref, g2_ref, xo_ref, h_ref):
    xn = x_ref[...] + _rms(y_ref[...], g1_ref[...])
    xo_ref[...] = xn
    h_ref[...] = _rms(xn, g2_ref[...]).astype(h_ref.dtype)


def _resid_kernel(x_ref, y_ref, g_ref, xo_ref):
    xo_ref[...] = x_ref[...] + _rms(y_ref[...], g_ref[...])


def _row_spec(tr, d):
    return pl.BlockSpec((tr, d), lambda i: (i, 0))


def _vec_spec(d):
    return pl.BlockSpec((1, d), lambda i: (0, 0))


def norm_cast(x, g, tr=256):
    t, d = x.shape
    return pl.pallas_call(
        _norm_cast_kernel,
        grid=(t // tr,),
        in_specs=[_row_spec(tr, d), _vec_spec(d)],
        out_specs=_row_spec(tr, d),
        out_shape=jax.ShapeDtypeStruct((t, d), BF16),
        compiler_params=_params("parallel"),
        name="norm_cast",
    )(x, g.reshape(1, d))


def resid_norm(x, y, g1, g2, tr=256):
    t, d = x.shape
    return pl.pallas_call(
        _resid_norm_kernel,
        grid=(t // tr,),
        in_specs=[_row_spec(tr, d), _row_spec(tr, d), _vec_spec(d), _vec_spec(d)],
        out_specs=[_row_spec(tr, d), _row_spec(tr, d)],
        out_shape=[jax.ShapeDtypeStruct((t, d), F32), jax.ShapeDtypeStruct((t, d), BF16)],
        compiler_params=_params("parallel"),
        name="resid_norm",
    )(x, y, g1.reshape(1, d), g2.reshape(1, d))


def resid(x, y, g, tr=256):
    t, d = x.shape
    return pl.pallas_call(
        _resid_kernel,
        grid=(t // tr,),
        in_specs=[_row_spec(tr, d), _row_spec(tr, d), _vec_spec(d)],
        out_specs=_row_spec(tr, d),
        out_shape=jax.ShapeDtypeStruct((t, d), F32),
        compiler_params=_params("parallel"),
        name="resid",
    )(x, y, g.reshape(1, d))


def _matmul_kernel(*refs, n_pairs, w_t):
    o_ref = refs[2 * n_pairs]
    dot = _dot_nt if w_t else _dot
    acc = dot(refs[0][...], refs[n_pairs][...].astype(BF16))
    for p in range(1, n_pairs):
        acc = acc + dot(refs[p][...], refs[n_pairs + p][...].astype(BF16))
    o_ref[...] = acc.astype(o_ref.dtype)


def matmul(pairs, n, out_dtype, tm, tn, name, w_t=False):
    m = pairs[0][0].shape[0]
    n_pairs = len(pairs)
    a_specs, w_specs, args_a, args_w = [], [], [], []
    for a, w, layer, rb in pairs:
        k = a.shape[1]
        a_specs.append(pl.BlockSpec((tm, k), lambda i, j: (i, 0)))
        if w_t:
            w_specs.append(pl.BlockSpec((None, tn, k), lambda i, j, layer=layer, rb=rb: (layer, j, rb)))
        else:
            w_specs.append(pl.BlockSpec((None, k, tn), lambda i, j, layer=layer, rb=rb: (layer, rb, j)))
        args_a.append(a)
        args_w.append(w)
    return pl.pallas_call(
        functools.partial(_matmul_kernel, n_pairs=n_pairs, w_t=w_t),
        grid=(m // tm, n // tn),
        in_specs=a_specs + w_specs,
        out_specs=pl.BlockSpec((tm, tn), lambda i, j: (i, j)),
        out_shape=jax.ShapeDtypeStruct((m, n), out_dtype),
        compiler_params=_params("parallel", "arbitrary"),
        name=name,
    )(*args_a, *args_w)


def _row_slab_kernel(w_ref, o_ref, *, row0, n_rows):
    row = row0 + pl.program_id(0) * LANES + lax.broadcasted_iota(jnp.int32, w_ref.shape, 0)
    o_ref[...] = jnp.where(row < n_rows, w_ref[...], 0.0).astype(o_ref.dtype)


def row_slab_bf16(w_t, layer, row0, height):
    _, n_rows, k = w_t.shape
    assert row0 % LANES == 0 and height % LANES == 0
    return pl.pallas_call(
        functools.partial(_row_slab_kernel, row0=row0, n_rows=n_rows),
        grid=(height // LANES,),
        in_specs=[pl.BlockSpec((None, LANES, k), lambda j: (layer, row0 // LANES + j, 0))],
        out_specs=pl.BlockSpec((None, LANES, k), lambda j: (0, j, 0)),
        out_shape=jax.ShapeDtypeStruct((1, height, k), BF16),
        compiler_params=_params("parallel"),
        name="row_slab_bf16",
    )(w_t)


def _gelu_tanh(x):
    c = math.sqrt(2.0 / math.pi)
    return x * (0.5 * (1.0 + jnp.tanh(c * (x + 0.044715 * (x * x * x)))))


def _ffn_in_kernel(a_ref, ah_ref, wg_ref, wv_ref, cwg_ref, cwv_ref, cbg_ref, cbv_ref, wo_ref,
                   o_ref, wo16_ref, *, tm, seq):
    i = pl.program_id(0)
    wo16_ref[...] = wo_ref[...].astype(wo16_ref.dtype)
    tn = o_ref.shape[1]
    a = a_ref[...]
    ah = ah_ref[...]
    seq_start = (i * tm) % seq == 0
    row8 = lax.broadcasted_iota(jnp.int32, (8, tn), 0)

    def conv(w_ref, cw_ref, cb_ref):
        w = w_ref[...].astype(BF16)
        u = _dot(a, w)
        uh = _dot(ah, w)
        uh = jnp.where(seq_start, 0.0, uh)
        p1 = uh[BF16_SUBLANES - 1:BF16_SUBLANES, :]
        p2 = uh[BF16_SUBLANES - 2:BF16_SUBLANES - 1, :]
        r1 = pltpu.roll(u, 1, 0)
        r2 = pltpu.roll(u, 2, 0)
        h1 = jnp.where(row8 == 0, p1, r1[:8, :])
        h2 = jnp.where(row8 == 0, p2, jnp.where(row8 == 1, p1, r2[:8, :]))
        u1 = jnp.concatenate([h1, r1[8:, :]], axis=0)
        u2 = jnp.concatenate([h2, r2[8:, :]], axis=0)
        cw = cw_ref[...]
        return cb_ref[...] + (cw[0:1, :] * u2 + cw[1:2, :] * u1 + cw[2:3, :] * u)

    gate = conv(wg_ref, cwg_ref, cbg_ref)
    val = conv(wv_ref, cwv_ref, cbv_ref)
    o_ref[...] = (_gelu_tanh(gate) * val).astype(o_ref.dtype)


def ffn_in(h, w_in, conv_w, conv_b, w_out, layer, seq, tm=1024, tn=256):
    t, k = h.shape
    f = w_in.shape[2] // 2
    d_out = w_out.shape[2]
    nf = f // tn
    n_steps = (t // tm) * nf
    assert f % (n_steps * BF16_SUBLANES) == 0
    slab = f // n_steps
    halo = BF16_SUBLANES
    hb = tm // halo
    conv_b = conv_b.reshape(conv_b.shape[0], 1, 2 * f)
    return pl.pallas_call(
        functools.partial(_ffn_in_kernel, tm=tm, seq=seq),
        grid=(t // tm, nf),
        in_specs=[
            pl.BlockSpec((tm, k), lambda i, j: (i, 0)),
            pl.BlockSpec((halo, k), lambda i, j: (jnp.maximum(i * hb - 1, 0), 0)),
            pl.BlockSpec((None, k, tn), lambda i, j: (layer, 0, j)),
            pl.BlockSpec((None, k, tn), lambda i, j: (layer, 0, j + nf)),
            pl.BlockSpec((None, CONV_W, tn), lambda i, j: (layer, 0, j)),
            pl.BlockSpec((None, CONV_W, tn), lambda i, j: (layer, 0, j + nf)),
            pl.BlockSpec((None, 1, tn), lambda i, j: (layer, 0, j)),
            pl.BlockSpec((None, 1, tn), lambda i, j: (layer, 0, j + nf)),
            pl.BlockSpec((None, slab, d_out), lambda i, j: (layer, i * nf + j, 0)),
        ],
        out_specs=[pl.BlockSpec((tm, tn), lambda i, j: (i, j)),
                   pl.BlockSpec((slab, d_out), lambda i, j: (i * nf + j, 0))],
        out_shape=[jax.ShapeDtypeStruct((t, f), BF16),
                   jax.ShapeDtypeStruct((f, d_out), BF16)],
        compiler_params=_params("parallel", "arbitrary"),
        name="ffn_in",
    )(h, h, w_in, w_in, conv_w, conv_w, conv_b, conv_b, w_out)


def _transpose_value_tiles(v_ref, vt_scr, tq):
    def transpose_tile(kb, c):
        start = pl.multiple_of(kb * tq, tq)
        vt_scr[kb] = v_ref[pl.ds(start, tq), :].astype(F32).T.astype(vt_scr.dtype)
        return c
    lax.fori_loop(0, vt_scr.shape[0], transpose_tile, 0)


def _sb_kernel(q_ref, k_ref, v_ref, o_ref, vt_scr, *, tq, scale, g):
    qi = pl.program_id(2)
    d = HEAD_DIM
    heads = [slice(hh * d, (hh + 1) * d) for hh in range(g)]
    qs = [q_ref[:, cs] for cs in heads]

    @pl.when(qi == 0)
    def _():
        _transpose_value_tiles(v_ref, vt_scr, tq)

    key = lax.broadcasted_iota(jnp.int32, (tq, tq), 0)
    qry = lax.broadcasted_iota(jnp.int32, (tq, tq), 1)
    strict = key < qry
    r2 = lax.broadcasted_iota(jnp.int32, (tq, 2 * tq), 0)
    c2 = lax.broadcasted_iota(jnp.int32, (tq, 2 * tq), 1)
    tri = (jnp.where(c2 >= tq, c2 - tq, c2) >= r2).astype(BF16)

    def block(kb, carry, diag):
        start = pl.multiple_of(kb * tq, tq)
        zs = [_dot_nt(k_ref[pl.ds(start, tq), cs], q) * (scale * LOG2E) for q, cs in zip(qs, heads)]
        csums = []
        for z in zs:
            neg_abs = lax.bitcast_convert_type(
                lax.bitcast_convert_type(z, jnp.uint32) | jnp.uint32(0x80000000), F32)
            sp = jnp.maximum(z, 0.0) + jnp.log2(1.0 + jnp.exp2(neg_abs))
            if diag:
                sp = jnp.where(strict, sp, 0.0)
            hi = sp.astype(BF16)
            lo = (sp - hi.astype(F32)).astype(BF16)
            csums.append(_dot(tri, jnp.concatenate([hi, lo], axis=0)))
        out = []
        for z, csum, cs, (rsum, acc) in zip(zs, csums, heads, carry):
            w = jnp.exp2(z - csum - rsum)
            if diag:
                w = jnp.where(strict, w, 0.0)
            acc = acc + _dot(vt_scr[kb, cs, :], w.astype(BF16))
            rsum = rsum + csum[0:1, :]
            out.append((rsum, acc))
        return tuple(out)

    init = tuple((jnp.zeros((1, tq), F32), jnp.zeros((d, tq), F32)) for _ in range(g))
    carry = block(qi, init, True)

    def body(it, carry):
        return block(qi - 1 - it, carry, False)

    carry = lax.fori_loop(0, qi, body, carry)
    for cs, (_, acc) in zip(heads, carry):
        o_ref[:, cs] = acc.T.astype(o_ref.dtype)


def sb_attention(qkv, batch, seq, n_heads, q_col, k_col, v_col, g=SB_HEADS_PER_STEP):
    tq = ATTN_TILE
    nq = seq // tq
    d = HEAD_DIM
    assert n_heads % g == 0 and q_col % g == 0 and k_col % g == 0 and v_col % g == 0
    return pl.pallas_call(
        functools.partial(_sb_kernel, tq=tq, scale=1.0 / math.sqrt(d), g=g),
        grid=(batch, n_heads // g, nq),
        in_specs=[
            pl.BlockSpec((tq, g * d), lambda b, h, i: (b * nq + i, q_col // g + h)),
            pl.BlockSpec((seq, g * d), lambda b, h, i: (b, k_col // g + h)),
            pl.BlockSpec((seq, g * d), lambda b, h, i: (b, v_col // g + h)),
        ],
        out_specs=pl.BlockSpec((tq, g * d), lambda b, h, i: (b * nq + i, h)),
        out_shape=jax.ShapeDtypeStruct((batch * seq, n_heads * d), BF16),
        scratch_shapes=[pltpu.VMEM((nq, g * d, tq), BF16)],
        compiler_params=_params("parallel", "parallel", "arbitrary"),
        name="sb_attention",
    )(qkv, qkv, qkv)


def _online_step_t(s, vt, m, l, acc):
    m_new = jnp.maximum(m, jnp.max(s, axis=0, keepdims=True))
    alpha = jnp.exp2(m - m_new)
    p = jnp.exp2(s - m_new)
    l = alpha * l + jnp.sum(p, axis=0, keepdims=True)
    acc = alpha * acc + _dot(vt, p.astype(BF16))
    return m_new, l, acc


def _chunk_causal(tq):
    row = lax.broadcasted_iota(jnp.int32, (tq, tq), 0)
    col = lax.broadcasted_iota(jnp.int32, (tq, tq), 1)
    return row, col, (col // CHUNK) <= (row // CHUNK)


def _diff_kernel(q_ref, k_ref, v_ref, lam_ref, g_ref, o_ref, vt_scr, *, tq, scale, lam_init, g):
    hg = pl.program_id(1)
    qi = pl.program_id(2)
    d = HEAD_DIM
    dv = 2 * HEAD_DIM
    qk_cols = [slice(c * d, (c + 1) * d) for c in range(2 * g)]
    v_cols = [slice(hh * dv, (hh + 1) * dv) for hh in range(g)]
    qs = [q_ref[:, cs] for cs in qk_cols]

    @pl.when(qi == 0)
    def _():
        _transpose_value_tiles(v_ref, vt_scr, tq)

    key, qry, _ = _chunk_causal(tq)
    visible = (key // CHUNK) <= (qry // CHUNK)
    dist = (qry - key).astype(F32)
    nbias_diag, nbias_off, slopes = [], [], []
    for hh in range(g):
        slope = lax.bitcast_convert_type(
            jnp.full((1, 1), (126 - (hg * g + hh)) * (1 << 23), jnp.int32), F32)
        slope = slope * LOG2E
        slopes.append(slope)
        nbias_diag.append(-slope * jnp.abs(dist))
        nbias_off.append(-slope * dist)

    def block(kb, carry, diag):
        start = pl.multiple_of(kb * tq, tq)
        ss = [_dot_nt(k_ref[pl.ds(start, tq), cs], q) * (scale * LOG2E) for q, cs in zip(qs, qk_cols)]
        for c in range(2 * g):
            hh = c // 2
            if diag:
                ss[c] = jnp.where(visible, ss[c] + nbias_diag[hh], NEG)
            else:
                ss[c] = ss[c] + (nbias_off[hh] - slopes[hh] * ((qi - kb) * tq).astype(F32))
        return tuple(_online_step_t(ss[c], vt_scr[kb, v_cols[c // 2], :], *carry[c])
                     for c in range(2 * g))

    def body(kb, carry):
        return block(kb, carry, False)

    init = tuple((jnp.full((1, tq), NEG, F32), jnp.zeros((1, tq), F32), jnp.zeros((dv, tq), F32))
                 for _ in range(2 * g))
    carry = lax.fori_loop(0, qi, body, init)
    carry = block(qi, carry, True)

    lf = lam_ref[...]
    lam = (jnp.exp(jnp.sum(lf[0:1, :] * lf[1:2, :], axis=-1, keepdims=True))
           - jnp.exp(jnp.sum(lf[2:3, :] * lf[3:4, :], axis=-1, keepdims=True)) + lam_init)
    for hh in range(g):
        (_, l1, a1), (_, l2, a2) = carry[2 * hh], carry[2 * hh + 1]
        o = (a1 / l1 - lam * (a2 / l2)).T
        o_ref[:, v_cols[hh]] = (_rms(o, g_ref[...]) * (1.0 - lam_init)).astype(o_ref.dtype)


def diff_attention(qkv, diff_lambda, subln_g, batch, seq, n_heads, q_col, k_col, v_col, lam_init,
                   g=DIFF_HEADS_PER_STEP):
    tq = ATTN_TILE
    nq = seq // tq
    dv = 2 * HEAD_DIM
    assert n_heads == 8, "ALiBi slopes are built as exact powers of two"
    assert n_heads % g == 0 and q_col % g == 0 and k_col % g == 0 and v_col % g == 0
    return pl.pallas_call(
        functools.partial(_diff_kernel, tq=tq, scale=1.0 / math.sqrt(HEAD_DIM), lam_init=lam_init, g=g),
        grid=(batch, n_heads // g, nq),
        in_specs=[
            pl.BlockSpec((tq, g * dv), lambda b, h, i: (b * nq + i, q_col // g + h)),
            pl.BlockSpec((seq, g * dv), lambda b, h, i: (b, k_col // g + h)),
            pl.BlockSpec((seq, g * dv), lambda b, h, i: (b, v_col // g + h)),
            pl.BlockSpec((4, HEAD_DIM), lambda b, h, i: (0, 0)),
            pl.BlockSpec((1, dv), lambda b, h, i: (0, 0)),
        ],
        out_specs=pl.BlockSpec((tq, g * dv), lambda b, h, i: (b * nq + i, h)),
        out_shape=jax.ShapeDtypeStruct((batch * seq, n_heads * dv), BF16),
        scratch_shapes=[pltpu.VMEM((nq, g * dv, tq), BF16)],
        compiler_params=_params("parallel", "parallel", "arbitrary"),
        name="diff_attention",
    )(qkv, qkv, qkv, diff_lambda, subln_g.reshape(1, dv))


def _chunk_kernel(q_ref, k_ref, v_ref, tab_ref, o_ref, *, tq, scale, n_win, g):
    qi = pl.program_id(2)
    d = HEAD_DIM
    heads = [slice(hh * d, (hh + 1) * d) for hh in range(g)]
    kbs = [qi - (n_win - 1) + w for w in range(n_win)]
    starts = [pl.multiple_of(jnp.maximum(kb, 0) * tq, tq) for kb in kbs]
    scores = []
    for hh, cs in enumerate(heads):
        q = q_ref[:, cs]
        s_blocks = []
        for w in range(n_win):
            s = (_dot_nt(q, k_ref[pl.ds(starts[w], tq), cs]) * (scale * LOG2E)
                 + tab_ref[hh, :, w * tq:(w + 1) * tq])
            if w < n_win - 1:
                s = jnp.where(kbs[w] >= 0, s, NEG)
            s_blocks.append(s)
        scores.append(s_blocks)
    for cs, s_blocks in zip(heads, scores):
        m = s_blocks[0].max(axis=-1, keepdims=True)
        for s in s_blocks[1:]:
            m = jnp.maximum(m, s.max(axis=-1, keepdims=True))
        l = jnp.zeros_like(m)
        acc = jnp.zeros((tq, d), F32)
        for w, s in enumerate(s_blocks):
            p = jnp.exp2(s - m)
            l = l + jnp.sum(p, axis=-1, keepdims=True)
            acc = acc + _dot(p.astype(BF16), v_ref[pl.ds(starts[w], tq), cs])
        o_ref[:, cs] = (acc / l).astype(o_ref.dtype)


def chunk_bias_table(rel_bias, tq):
    left = LEFT_CHUNKS * CHUNK
    n_heads = rel_bias.shape[0]
    width = tq + left
    period = 2 * left
    assert left >= REL_CLIP and tq <= left
    rb = rel_bias.astype(F32)
    hi = jnp.broadcast_to(rb[:, -1:], (n_heads, left - REL_CLIP))
    mid = rb[:, ::-1]
    lo = jnp.broadcast_to(rb[:, :1], (n_heads, tq - REL_CLIP - 1))
    neg_d = jnp.broadcast_to(rb[:, -1:], (n_heads, period - width))
    diag = jnp.concatenate([hi, mid, lo, neg_d], axis=1)
    flat = jnp.tile(diag, (1, tq))[:, :tq * (period - 1)]
    bias = flat.reshape(n_heads, tq, period - 1)[:, :, :width]
    i = jnp.arange(tq)[:, None]
    j = jnp.arange(width)[None, :]
    qc, kc = i // CHUNK, j // CHUNK
    allowed = (kc >= qc) & (kc <= qc + LEFT_CHUNKS)
    return jnp.where(allowed[None], bias * LOG2E, NEG)


def chunk_attention(qkv, rel_bias, batch, seq, n_heads, q_col, k_col, v_col, g=CHUNK_HEADS_PER_STEP):
    tq = ATTN_TILE
    nq = seq // tq
    d = HEAD_DIM
    left = LEFT_CHUNKS * CHUNK
    assert left % tq == 0
    assert n_heads % g == 0 and q_col % g == 0 and k_col % g == 0 and v_col % g == 0
    n_win = left // tq + 1
    table = chunk_bias_table(rel_bias, tq)
    return pl.pallas_call(
        functools.partial(_chunk_kernel, tq=tq, scale=1.0 / math.sqrt(d), n_win=n_win, g=g),
        grid=(n_heads // g, batch, nq),
        in_specs=[
            pl.BlockSpec((tq, g * d), lambda h, b, i: (b * nq + i, q_col // g + h)),
            pl.BlockSpec((seq, g * d), lambda h, b, i: (b, k_col // g + h)),
            pl.BlockSpec((seq, g * d), lambda h, b, i: (b, v_col // g + h)),
            pl.BlockSpec((g, tq, tq + left), lambda h, b, i: (h, 0, 0)),
        ],
        out_specs=pl.BlockSpec((tq, g * d), lambda h, b, i: (b * nq + i, h)),
        out_shape=jax.ShapeDtypeStruct((batch * seq, n_heads * d), BF16),
        compiler_params=_params("parallel", "parallel", "arbitrary"),
        name="chunk_attention",
    )(qkv, qkv, qkv, table)


def rope_tables(seq):
    half = QK_ROPE // 2
    pos = jnp.arange(seq, dtype=F32)
    inv_freq = ROPE_THETA ** (-jnp.arange(0, QK_ROPE, 2, dtype=F32) / QK_ROPE)
    ang = pos[:, None] * inv_freq[None, :]
    cos, sin = jnp.cos(ang), jnp.sin(ang)
    z = lambda n: jnp.zeros((seq, n), F32)
    cos_t = jnp.concatenate([cos, cos, z(LANES - 2 * half)], axis=1)
    sin_a = jnp.concatenate([-sin, z(LANES - half)], axis=1)
    sin_b = jnp.concatenate([z(half), sin, z(LANES - 2 * half)], axis=1)
    return cos_t, sin_a, sin_b


def _rope(x, cos_t, sin_a, sin_b):
    half = QK_ROPE // 2
    return x * cos_t + pltpu.roll(x, LANES - half, 1) * sin_a + pltpu.roll(x, half, 1) * sin_b


def _mla_q_kernel(c_ref, g_ref, w_ref, cos_ref, sa_ref, sb_ref, o_ref, an_ref, *, hp):
    @pl.when(pl.program_id(1) == 0)
    def _():
        an_ref[...] = _rms(c_ref[...], g_ref[...]).astype(an_ref.dtype)

    hw = QK_NOPE + LANES
    res = _dot(an_ref[...], w_ref[...])
    cos_t, sin_a, sin_b = cos_ref[...], sa_ref[...], sb_ref[...]
    for hh in range(hp):
        c0 = hh * hw
        o_ref[:, c0:c0 + QK_NOPE] = res[:, c0:c0 + QK_NOPE].astype(o_ref.dtype)
        o_ref[:, c0 + QK_NOPE:c0 + hw] = _rope(res[:, c0 + QK_NOPE:c0 + hw],
                                               cos_t, sin_a, sin_b).astype(o_ref.dtype)


def mla_q(lat, g, w_q, tables, seq, n_heads, tm=512, hp=MLA_PROJ_HEADS_PER_STEP):
    t = lat.shape[0]
    kq = g.shape[0]
    hw = QK_NOPE + LANES
    npos = seq // tm
    tab_spec = pl.BlockSpec((tm, LANES), lambda i, j: (i % npos, 0))
    return pl.pallas_call(
        functools.partial(_mla_q_kernel, hp=hp),
        grid=(t // tm, n_heads // hp),
        in_specs=[
            pl.BlockSpec((tm, kq), lambda i, j: (i, 0)),
            pl.BlockSpec((1, kq), lambda i, j: (0, 0)),
            pl.BlockSpec((kq, hp * hw), lambda i, j: (0, j)),
            tab_spec, tab_spec, tab_spec,
        ],
        out_specs=pl.BlockSpec((tm, hp * hw), lambda i, j: (i, j)),
        out_shape=jax.ShapeDtypeStruct((t, n_heads * hw), BF16),
        scratch_shapes=[pltpu.VMEM((tm, kq), BF16)],
        compiler_params=_params("parallel", "arbitrary"),
        name="mla_q",
    )(lat, g.reshape(1, kq), w_q, *tables)


def _mla_kv_kernel(c_ref, kr_ref, g_ref, w_ref, cos_ref, sa_ref, sb_ref, k_ref, vt_ref,
                   an_ref, kr_scr, *, hp, tm, tk):
    @pl.when(pl.program_id(1) == 0)
    def _():
        an_ref[...] = _rms(c_ref[...], g_ref[...]).astype(an_ref.dtype)
        kr_scr[...] = _rope(kr_ref[...], cos_ref[...], sa_ref[...], sb_ref[...]).astype(kr_scr.dtype)

    hw = QK_NOPE + LANES
    wv = QK_NOPE + V_MLA
    res = _dot(an_ref[...], w_ref[...])
    kr = kr_scr[...]
    for hh in range(hp):
        k_ref[:, hh * hw:hh * hw + QK_NOPE] = res[:, hh * wv:hh * wv + QK_NOPE].astype(k_ref.dtype)
        k_ref[:, hh * hw + QK_NOPE:(hh + 1) * hw] = kr
        v = res[:, hh * wv + QK_NOPE:(hh + 1) * wv]
        for kt in range(tm // tk):
            vt_ref[kt, hh * V_MLA:(hh + 1) * V_MLA, :] = v[kt * tk:(kt + 1) * tk, :].T.astype(vt_ref.dtype)


def mla_kv(lat, g, w_kv, tables, seq, n_heads, ckv_col, kr_col, tm=512, hp=MLA_PROJ_HEADS_PER_STEP):
    t = lat.shape[0]
    kkv = g.shape[0]
    hw = QK_NOPE + LANES
    tk = ATTN_TILE
    assert tm % tk == 0
    npos = seq // tm
    tab_spec = pl.BlockSpec((tm, LANES), lambda i, j: (i % npos, 0))
    return pl.pallas_call(
        functools.partial(_mla_kv_kernel, hp=hp, tm=tm, tk=tk),
        grid=(t // tm, n_heads // hp),
        in_specs=[
            pl.BlockSpec((tm, kkv), lambda i, j: (i, ckv_col)),
            pl.BlockSpec((tm, LANES), lambda i, j: (i, kr_col)),
            pl.BlockSpec((1, kkv), lambda i, j: (0, 0)),
            pl.BlockSpec((kkv, hp * (QK_NOPE + V_MLA)), lambda i, j: (0, j)),
            tab_spec, tab_spec, tab_spec,
        ],
        out_specs=[pl.BlockSpec((tm, hp * hw), lambda i, j: (i, j)),
                   pl.BlockSpec((tm // tk, hp * V_MLA, tk), lambda i, j: (i, j, 0))],
        out_shape=[jax.ShapeDtypeStruct((t, n_heads * hw), BF16),
                   jax.ShapeDtypeStruct((t // tk, n_heads * V_MLA, tk), BF16)],
        scratch_shapes=[pltpu.VMEM((tm, kkv), BF16), pltpu.VMEM((tm, LANES), BF16)],
        compiler_params=_params("parallel", "arbitrary"),
        name="mla_kv",
    )(lat, lat, g.reshape(1, kkv), w_kv, *tables)


def _mla_attn_kernel(q_ref, k_ref, vt_ref, o_ref, *, tq, scale, g):
    qi = pl.program_id(2)
    hw = QK_NOPE + LANES
    dv = V_MLA
    qk_cols = [slice(hh * hw, (hh + 1) * hw) for hh in range(g)]
    v_cols = [slice(hh * dv, (hh + 1) * dv) for hh in range(g)]
    qs = [q_ref[:, cs] for cs in qk_cols]
    key, qry, _ = _chunk_causal(tq)
    visible = (key // CHUNK) <= (qry // CHUNK)

    def block(kb, carry, diag):
        start = pl.multiple_of(kb * tq, tq)
        ss = [_dot_nt(k_ref[pl.ds(start, tq), cs], q) * (scale * LOG2E) for q, cs in zip(qs, qk_cols)]
        if diag:
            ss = [jnp.where(visible, s, NEG) for s in ss]
        return tuple(_online_step_t(s, vt_ref[kb, vs, :], *st)
                     for s, vs, st in zip(ss, v_cols, carry))

    def body(kb, carry):
        return block(kb, carry, False)

    init = tuple((jnp.full((1, tq), NEG, F32), jnp.zeros((1, tq), F32), jnp.zeros((dv, tq), F32))
                 for _ in range(g))
    carry = lax.fori_loop(0, qi, body, init)
    carry = block(qi, carry, True)
    for vs, (_, l, acc) in zip(v_cols, carry):
        o_ref[:, vs] = (acc / l).T.astype(o_ref.dtype)


def mla_attention(q, k, vt, batch, seq, n_heads, g=MLA_HEADS_PER_STEP):
    tq = ATTN_TILE
    nq = seq // tq
    hw = QK_NOPE + LANES
    assert n_heads % g == 0
    return pl.pallas_call(
        functools.partial(_mla_attn_kernel, tq=tq, scale=1.0 / math.sqrt(QK_NOPE + QK_ROPE), g=g),
        grid=(batch, n_heads // g, nq),
        in_specs=[
            pl.BlockSpec((tq, g * hw), lambda b, h, i: (b * nq + i, h)),
            pl.BlockSpec((seq, g * hw), lambda b, h, i: (b, h), pipeline_mode=pl.Buffered(1)),
            pl.BlockSpec((nq, g * V_MLA, tq), lambda b, h, i: (b, h, 0), pipeline_mode=pl.Buffered(1)),
        ],
        out_specs=pl.BlockSpec((tq, g * V_MLA), lambda b, h, i: (b * nq + i, h)),
        out_shape=jax.ShapeDtypeStruct((batch * seq, n_heads * V_MLA), BF16),
        compiler_params=_params("parallel", "parallel", "arbitrary"),
        name="mla_attention",
    )(q, k, vt)


def _even_mixer(hn, w_in, w_out, i, diff_lambda, subln_g, batch, seq, layer):
    d_model = hn.shape[1]
    w_sb = d_model // 2
    n_sb = w_sb // HEAD_DIM
    n_diff = w_sb // (2 * HEAD_DIM)
    cb = w_sb // LANES
    qkv = matmul([(hn, w_in, i, 0)], w_in.shape[2], BF16, tm=1024, tn=512, name="even_in_proj")
    a = sb_attention(qkv, batch, seq, n_sb, 0, cb, 2 * cb)
    lam_init = 0.8 - 0.6 * math.exp(-0.3 * layer)
    bo = diff_attention(qkv, diff_lambda, subln_g, batch, seq, n_diff,
                        3 * cb // 2, 4 * cb // 2, 5 * cb // 2, lam_init)
    return matmul([(a, w_out, i, 0), (bo, w_out, i, 1)], d_model, F32, tm=1024, tn=512,
                  name="mix_out_proj")


def _odd_mixer(hn, w_in, w_out, i, rel_bias, q_norm_g, w_uq, kv_norm_g, w_ukv, batch, seq):
    d_model = hn.shape[1]
    w_ch = d_model // 2
    n_ch = w_ch // HEAD_DIM
    n_mla = w_ch // HEAD_DIM
    cb = w_ch // LANES
    q_lora, kv_lora = q_norm_g.shape[0], kv_norm_g.shape[0]
    n_attn = 3 * w_ch
    n_lat = q_lora + kv_lora + QK_ROPE
    lat_pad = -n_lat % LANES
    w_in_t = jnp.swapaxes(w_in, 1, 2)
    w_lat_t = row_slab_bf16(w_in_t, i, n_attn, n_lat + lat_pad)
    qkv = matmul([(hn, w_in_t, i, 0)], n_attn, BF16, tm=1024, tn=512, name="odd_in_proj", w_t=True)
    lat = matmul([(hn, w_lat_t, 0, 0)], n_lat + lat_pad, F32, tm=512, tn=n_lat + lat_pad,
                 name="odd_lat_proj", w_t=True)
    c = chunk_attention(qkv, rel_bias, batch, seq, n_ch, 0, cb, 2 * cb)

    tables = rope_tables(seq)
    hw = QK_NOPE + LANES
    wq = w_uq.reshape(q_lora, n_mla, QK_NOPE + QK_ROPE)
    wq = jnp.pad(wq, ((0, 0), (0, 0), (0, hw - QK_NOPE - QK_ROPE))).reshape(q_lora, n_mla * hw).astype(BF16)
    q = mla_q(lat, q_norm_g, wq, tables, seq, n_mla)
    assert q_lora % kv_lora == 0 and (q_lora + kv_lora) % LANES == 0
    k, v = mla_kv(lat, kv_norm_g, w_ukv.astype(BF16), tables, seq, n_mla,
                  q_lora // kv_lora, (q_lora + kv_lora) // LANES)
    dm = mla_attention(q, k, v, batch, seq, n_mla)
    return matmul([(c, w_out, i, 0), (dm, w_out, i, 1)], d_model, F32, tm=1024, tn=512,
                  name="mix_out_proj")


def _ffn(h, w_in, conv_w, conv_b, w_out, layer, seq):
    g, w_out16 = ffn_in(h, w_in, conv_w, conv_b, w_out, layer, seq)
    return matmul([(g, w_out16[None], 0, 0)], w_out.shape[2], F32, tm=512, tn=256,
                  name="ffn_out_proj")


def kernel(x, norm_g, even_w_in, even_w_out, diff_lambda, diff_subln_g, odd_w_in, odd_w_out,
           ch_rel_bias, mla_q_norm_g, mla_w_uq, mla_kv_norm_g, mla_w_ukv, ffn_w_in, ffn_conv_w,
           ffn_conv_b, ffn_w_out):
    batch, seq, d_model = x.shape
    depth = norm_g.shape[0]
    xf = x.reshape(batch * seq, d_model)
    hn = norm_cast(xf, norm_g[0, 0])
    for layer in range(depth):
        g = norm_g[layer]
        i = layer // 2
        if layer % 2 == 0:
            mix = _even_mixer(hn, even_w_in, even_w_out, i, diff_lambda[i], diff_subln_g[i],
                              batch, seq, layer)
        else:
            mix = _odd_mixer(hn, odd_w_in, odd_w_out, i, ch_rel_bias[i], mla_q_norm_g[i],
                             mla_w_uq[i], mla_kv_norm_g[i], mla_w_ukv[i], batch, seq)
        xf, h2 = resid_norm(xf, mix, g[1], g[2])
        f = _ffn(h2, ffn_w_in, ffn_conv_w, ffn_conv_b, ffn_w_out, layer, seq)
        if layer + 1 < depth:
            xf, hn = resid_norm(xf, f, g[3], norm_g[layer + 1, 0])
        else:
            xf = resid(xf, f, g[3])
    return xf.reshape(batch, seq, d_model)
```

```python
import functools
import math

import jax
import jax.numpy as jnp
from jax import lax
from jax.experimental import pallas as pl
from jax.experimental.pallas import tpu as pltpu

F32 = jnp.float32
BF16 = jnp.bfloat16

CHUNK = 64
HEAD_DIM = 128
LEFT_CHUNKS = 8
REL_CLIP = 128
QK_NOPE = 128
QK_ROPE = 64
V_MLA = 128
ROPE_THETA = 10000.0
CONV_W = 3
EPS = 1e-6
NEG = -1e30
LOG2E = math.log2(math.e)

LANES = 128
BF16_SUBLANES = 16
VMEM_LIMIT = 52 * 1024 * 1024

ATTN_TILE = 256
SB_HEADS_PER_STEP = 8
MLA_HEADS_PER_STEP = 8
DIFF_HEADS_PER_STEP = 4
CHUNK_HEADS_PER_STEP = 4
MLA_PROJ_HEADS_PER_STEP = 4


def _params(*sem):
    return pltpu.CompilerParams(dimension_semantics=sem, vmem_limit_bytes=VMEM_LIMIT)


def _rms(x, g):
    ms = jnp.mean(x * x, axis=-1, keepdims=True)
    return x * lax.rsqrt(ms + EPS) * g


def _dot(a, b):
    return jnp.dot(a, b, preferred_element_type=F32)


def _dot_nt(a, b):
    return lax.dot_general(a, b, (((1,), (1,)), ((), ())), preferred_element_type=F32)


def _norm_cast_kernel(x_ref, g_ref, h_ref):
    h_ref[...] = _rms(x_ref[...], g_ref[...]).astype(h_ref.dtype)


def _resid_norm_kernel(x_ref, y_ref, g1_ref, g2_ref, xo_ref, h_ref):
    xn = x_ref[...] + _rms(y_ref[...], g1_ref[...])
    xo_ref[...] = xn
    h_ref[...] = _rms(xn, g2_ref[...]).astype(h_ref.dtype)


def _resid_kernel(x_ref, y_ref, g_ref, xo_ref):
    xo_ref[...] = x_ref[...] + _rms(y_ref[...], g_ref[...])


def _row_spec(tr, d):
    return pl.BlockSpec((tr, d), lambda i: (i, 0))


def _vec_spec(d):
    return pl.BlockSpec((1, d), lambda i: (0, 0))


def norm_cast(x, g, tr=256):
    t, d = x.shape
    return pl.pallas_call(
        _norm_cast_kernel,
        grid=(t // tr,),
        in_specs=[_row_spec(tr, d), _vec_spec(d)],
        out_specs=_row_spec(tr, d),
        out_shape=jax.ShapeDtypeStruct((t, d), BF16),
        compiler_params=_params("parallel"),
        name="norm_cast",
    )(x, g.reshape(1, d))


def resid_norm(x, y, g1, g2, tr=256):
    t, d = x.shape
    return pl.pallas_call(
        _resid_norm_kernel,
        grid=(t // tr,),
        in_specs=[_row_spec(tr, d), _row_spec(tr, d), _vec_spec(d), _vec_spec(d)],
        out_specs=[_row_spec(tr, d), _row_spec(tr, d)],
        out_shape=[jax.ShapeDtypeStruct((t, d), F32), jax.ShapeDtypeStruct((t, d), BF16)],
        compiler_params=_params("parallel"),
        name="resid_norm",
    )(x, y, g1.reshape(1, d), g2.reshape(1, d))


def resid(x, y, g, tr=256):
    t, d = x.shape
    return pl.pallas_call(
        _resid_kernel,
        grid=(t // tr,),
        in_specs=[_row_spec(tr, d), _row_spec(tr, d), _vec_spec(d)],
        out_specs=_row_spec(tr, d),
        out_shape=jax.ShapeDtypeStruct((t, d), F32),
        compiler_params=_params("parallel"),
        name="resid",
    )(x, y, g.reshape(1, d))


def _matmul_kernel(*refs, n_pairs, w_t):
    o_ref = refs[2 * n_pairs]
    dot = _dot_nt if w_t else _dot
    acc = dot(refs[0][...], refs[n_pairs][...].astype(BF16))
    for p in range(1, n_pairs):
        acc = acc + dot(refs[p][...], refs[n_pairs + p][...].astype(BF16))
    o_ref[...] = acc.astype(o_ref.dtype)


def matmul(pairs, n, out_dtype, tm, tn, name, w_t=False):
    m = pairs[0][0].shape[0]
    n_pairs = len(pairs)
    a_specs, w_specs, args_a, args_w = [], [], [], []
    for a, w, layer, rb in pairs:
        k = a.shape[1]
        a_specs.append(pl.BlockSpec((tm, k), lambda i, j: (i, 0)))
        if w_t:
            w_specs.append(pl.BlockSpec((None, tn, k), lambda i, j, layer=layer, rb=rb: (layer, j, rb)))
        else:
            w_specs.append(pl.BlockSpec((None, k, tn), lambda i, j, layer=layer, rb=rb: (layer, rb, j)))
        args_a.append(a)
        args_w.append(w)
    return pl.pallas_call(
        functools.partial(_matmul_kernel, n_pairs=n_pairs, w_t=w_t),
        grid=(m // tm, n // tn),
        in_specs=a_specs + w_specs,
        out_specs=pl.BlockSpec((tm, tn), lambda i, j: (i, j)),
        out_shape=jax.ShapeDtypeStruct((m, n), out_dtype),
        compiler_params=_params("parallel", "arbitrary"),
        name=name,
    )(*args_a, *args_w)


def _row_slab_kernel(w_ref, o_ref, *, row0, n_rows):
    row = row0 + pl.program_id(0) * LANES + lax.broadcasted_iota(jnp.int32, w_ref.shape, 0)
    o_ref[...] = jnp.where(row < n_rows, w_ref[...], 0.0).astype(o_ref.dtype)


def row_slab_bf16(w_t, layer, row0, height):
    _, n_rows, k = w_t.shape
    assert row0 % LANES == 0 and height % LANES == 0
    return pl.pallas_call(
        functools.partial(_row_slab_kernel, row0=row0, n_rows=n_rows),
        grid=(height // LANES,),
        in_specs=[pl.BlockSpec((None, LANES, k), lambda j: (layer, row0 // LANES + j, 0))],
        out_specs=pl.BlockSpec((None, LANES, k), lambda j: (0, j, 0)),
        out_shape=jax.ShapeDtypeStruct((1, height, k), BF16),
        compiler_params=_params("parallel"),
        name="row_slab_bf16",
    )(w_t)


def _gelu_tanh(x):
    c = math.sqrt(2.0 / math.pi)
    return x * (0.5 * (1.0 + jnp.tanh(c * (x + 0.044715 * (x * x * x)))))


def _ffn_in_kernel(a_ref, ah_ref, wg_ref, wv_ref, cwg_ref, cwv_ref, cbg_ref, cbv_ref, wo_ref,
                   o_ref, wo16_ref, *, tm, seq):
    i = pl.program_id(0)
    wo16_ref[...] = wo_ref[...].astype(wo16_ref.dtype)
    tn = o_ref.shape[1]
    a = a_ref[...]
    ah = ah_ref[...]
    seq_start = (i * tm) % seq == 0
    row8 = lax.broadcasted_iota(jnp.int32, (8, tn), 0)

    def conv(w_ref, cw_ref, cb_ref):
        w = w_ref[...].astype(BF16)
        u = _dot(a, w)
        uh = _dot(ah, w)
        uh = jnp.where(seq_start, 0.0, uh)
        p1 = uh[BF16_SUBLANES - 1:BF16_SUBLANES, :]
        p2 = uh[BF16_SUBLANES - 2:BF16_SUBLANES - 1, :]
        r1 = pltpu.roll(u, 1, 0)
        r2 = pltpu.roll(u, 2, 0)
        h1 = jnp.where(row8 == 0, p1, r1[:8, :])
        h2 = jnp.where(row8 == 0, p2, jnp.where(row8 == 1, p1, r2[:8, :]))
        u1 = jnp.concatenate([h1, r1[8:, :]], axis=0)
        u2 = jnp.concatenate([h2, r2[8:, :]], axis=0)
        cw = cw_ref[...]
        return cb_ref[...] + (cw[0:1, :] * u2 + cw[1:2, :] * u1 + cw[2:3, :] * u)

    gate = conv(wg_ref, cwg_ref, cbg_ref)
    val = conv(wv_ref, cwv_ref, cbv_ref)
    o_ref[...] = (_gelu_tanh(gate) * val).astype(o_ref.dtype)


def ffn_in(h, w_in, conv_w, conv_b, w_out, layer, seq, tm=2048, tn=256):
    t, k = h.shape
    f = w_in.shape[2] // 2
    d_out = w_out.shape[2]
    nf = f // tn
    n_steps = (t // tm) * nf
    assert f % (n_steps * BF16_SUBLANES) == 0
    slab = f // n_steps
    halo = BF16_SUBLANES
    hb = tm // halo
    conv_b = conv_b.reshape(conv_b.shape[0], 1, 2 * f)
    return pl.pallas_call(
        functools.partial(_ffn_in_kernel, tm=tm, seq=seq),
        grid=(t // tm, nf),
        in_specs=[
            pl.BlockSpec((tm, k), lambda i, j: (i, 0), pipeline_mode=pl.Buffered(1)),
            pl.BlockSpec((halo, k), lambda i, j: (jnp.maximum(i * hb - 1, 0), 0)),
            pl.BlockSpec((None, k, tn), lambda i, j: (layer, 0, j)),
            pl.BlockSpec((None, k, tn), lambda i, j: (layer, 0, j + nf)),
            pl.BlockSpec((None, CONV_W, tn), lambda i, j: (layer, 0, j)),
            pl.BlockSpec((None, CONV_W, tn), lambda i, j: (layer, 0, j + nf)),
            pl.BlockSpec((None, 1, tn), lambda i, j: (layer, 0, j)),
            pl.BlockSpec((None, 1, tn), lambda i, j: (layer, 0, j + nf)),
            pl.BlockSpec((None, slab, d_out), lambda i, j: (layer, i * nf + j, 0)),
        ],
        out_specs=[pl.BlockSpec((tm, tn), lambda i, j: (i, j)),
                   pl.BlockSpec((slab, d_out), lambda i, j: (i * nf + j, 0))],
        out_shape=[jax.ShapeDtypeStruct((t, f), BF16),
                   jax.ShapeDtypeStruct((f, d_out), BF16)],
        compiler_params=_params("parallel", "arbitrary"),
        name="ffn_in",
    )(h, h, w_in, w_in, conv_w, conv_w, conv_b, conv_b, w_out)


def _transpose_value_tiles(v_ref, vt_scr, tq):
    def transpose_tile(kb, c):
        start = pl.multiple_of(kb * tq, tq)
        vt_scr[kb] = v_ref[pl.ds(start, tq), :].astype(F32).T.astype(vt_scr.dtype)
        return c
    lax.fori_loop(0, vt_scr.shape[0], transpose_tile, 0)


def _sb_kernel(q_ref, k_ref, v_ref, o_ref, vt_scr, *, tq, scale, g):
    qi = pl.program_id(2)
    d = HEAD_DIM
    heads = [slice(hh * d, (hh + 1) * d) for hh in range(g)]
    qs = [q_ref[:, cs] for cs in heads]

    @pl.when(qi == 0)
    def _():
        _transpose_value_tiles(v_ref, vt_scr, tq)

    key = lax.broadcasted_iota(jnp.int32, (tq, tq), 0)
    qry = lax.broadcasted_iota(jnp.int32, (tq, tq), 1)
    strict = key < qry
    r2 = lax.broadcasted_iota(jnp.int32, (tq, 2 * tq), 0)
    c2 = lax.broadcasted_iota(jnp.int32, (tq, 2 * tq), 1)
    tri = (jnp.where(c2 >= tq, c2 - tq, c2) >= r2).astype(BF16)

    def block(kb, carry, diag):
        start = pl.multiple_of(kb * tq, tq)
        zs = [_dot_nt(k_ref[pl.ds(start, tq), cs], q) * (scale * LOG2E) for q, cs in zip(qs, heads)]
        csums = []
        for z in zs:
            neg_abs = lax.bitcast_convert_type(
                lax.bitcast_convert_type(z, jnp.uint32) | jnp.uint32(0x80000000), F32)
            sp = jnp.maximum(z, 0.0) + jnp.log2(1.0 + jnp.exp2(neg_abs))
            if diag:
                sp = jnp.where(strict, sp, 0.0)
            hi = sp.astype(BF16)
            lo = (sp - hi.astype(F32)).astype(BF16)
            csums.append(_dot(tri, jnp.concatenate([hi, lo], axis=0)))
        out = []
        for z, csum, cs, (rsum, acc) in zip(zs, csums, heads, carry):
            w = jnp.exp2(z - csum - rsum)
            if diag:
                w = jnp.where(strict, w, 0.0)
            acc = acc + _dot(vt_scr[kb, cs, :], w.astype(BF16))
            rsum = rsum + csum[0:1, :]
            out.append((rsum, acc))
        return tuple(out)

    init = tuple((jnp.zeros((1, tq), F32), jnp.zeros((d, tq), F32)) for _ in range(g))
    carry = block(qi, init, True)

    def body(it, carry):
        return block(qi - 1 - it, carry, False)

    carry = lax.fori_loop(0, qi, body, carry)
    for cs, (_, acc) in zip(heads, carry):
        o_ref[:, cs] = acc.T.astype(o_ref.dtype)


def sb_attention(qkv, batch, seq, n_heads, q_col, k_col, v_col, g=SB_HEADS_PER_STEP):
    tq = ATTN_TILE
    nq = seq // tq
    d = HEAD_DIM
    assert n_heads % g == 0 and q_col % g == 0 and k_col % g == 0 and v_col % g == 0
    return pl.pallas_call(
        functools.partial(_sb_kernel, tq=tq, scale=1.0 / math.sqrt(d), g=g),
        grid=(batch, n_heads // g, nq),
        in_specs=[
            pl.BlockSpec((tq, g * d), lambda b, h, i: (b * nq + i, q_col // g + h)),
            pl.BlockSpec((seq, g * d), lambda b, h, i: (b, k_col // g + h)),
            pl.BlockSpec((seq, g * d), lambda b, h, i: (b, v_col // g + h)),
        ],
        out_specs=pl.BlockSpec((tq, g * d), lambda b, h, i: (b * nq + i, h)),
        out_shape=jax.ShapeDtypeStruct((batch * seq, n_heads * d), BF16),
        scratch_shapes=[pltpu.VMEM((nq, g * d, tq), BF16)],
        compiler_params=_params("parallel", "parallel", "arbitrary"),
        name="sb_attention",
    )(qkv, qkv, qkv)


def _online_step_t(s, vt, m, l, acc):
    m_new = jnp.maximum(m, jnp.max(s, axis=0, keepdims=True))
    alpha = jnp.exp2(m - m_new)
    p = jnp.exp2(s - m_new)
    l = alpha * l + jnp.sum(p, axis=0, keepdims=True)
    acc = alpha * acc + _dot(vt, p.astype(BF16))
    return m_new, l, acc


def _chunk_causal(tq):
    row = lax.broadcasted_iota(jnp.int32, (tq, tq), 0)
    col = lax.broadcasted_iota(jnp.int32, (tq, tq), 1)
    return row, col, (col // CHUNK) <= (row // CHUNK)


def _diff_kernel(q_ref, k_ref, v_ref, lam_ref, g_ref, o_ref, vt_scr, *, tq, scale, lam_init, g):
    hg = pl.program_id(1)
    qi = pl.program_id(2)
    d = HEAD_DIM
    dv = 2 * HEAD_DIM
    qk_cols = [slice(c * d, (c + 1) * d) for c in range(2 * g)]
    v_cols = [slice(hh * dv, (hh + 1) * dv) for hh in range(g)]
    qs = [q_ref[:, cs] for cs in qk_cols]

    @pl.when(qi == 0)
    def _():
        _transpose_value_tiles(v_ref, vt_scr, tq)

    key, qry, _ = _chunk_causal(tq)
    visible = (key // CHUNK) <= (qry // CHUNK)
    dist = (qry - key).astype(F32)
    nbias_diag, nbias_off, slopes = [], [], []
    for hh in range(g):
        slope = lax.bitcast_convert_type(
            jnp.full((1, 1), (126 - (hg * g + hh)) * (1 << 23), jnp.int32), F32)
        slope = slope * LOG2E
        slopes.append(slope)
        nbias_diag.append(-slope * jnp.abs(dist))
        nbias_off.append(-slope * dist)

    def block(kb, carry, diag):
        start = pl.multiple_of(kb * tq, tq)
        ss = [_dot_nt(k_ref[pl.ds(start, tq), cs], q) * (scale * LOG2E) for q, cs in zip(qs, qk_cols)]
        for c in range(2 * g):
            hh = c // 2
            if diag:
                ss[c] = jnp.where(visible, ss[c] + nbias_diag[hh], NEG)
            else:
                ss[c] = ss[c] + (nbias_off[hh] - slopes[hh] * ((qi - kb) * tq).astype(F32))
        return tuple(_online_step_t(ss[c], vt_scr[kb, v_cols[c // 2], :], *carry[c])
                     for c in range(2 * g))

    def body(kb, carry):
        return block(kb, carry, False)

    init = tuple((jnp.full((1, tq), NEG, F32), jnp.zeros((1, tq), F32), jnp.zeros((dv, tq), F32))
                 for _ in range(2 * g))
    carry = lax.fori_loop(0, qi, body, init)
    carry = block(qi, carry, True)

    lf = lam_ref[...]
    lam = (jnp.exp(jnp.sum(lf[0:1, :] * lf[1:2, :], axis=-1, keepdims=True))
           - jnp.exp(jnp.sum(lf[2:3, :] * lf[3:4, :], axis=-1, keepdims=True)) + lam_init)
    for hh in range(g):
        (_, l1, a1), (_, l2, a2) = carry[2 * hh], carry[2 * hh + 1]
        o = (a1 / l1 - lam * (a2 / l2)).T
        o_ref[:, v_cols[hh]] = (_rms(o, g_ref[...]) * (1.0 - lam_init)).astype(o_ref.dtype)


def diff_attention(qkv, diff_lambda, subln_g, batch, seq, n_heads, q_col, k_col, v_col, lam_init,
                   g=DIFF_HEADS_PER_STEP):
    tq = ATTN_TILE
    nq = seq // tq
    dv = 2 * HEAD_DIM
    assert n_heads == 8, "ALiBi slopes are built as exact powers of two"
    assert n_heads % g == 0 and q_col % g == 0 and k_col % g == 0 and v_col % g == 0
    return pl.pallas_call(
        functools.partial(_diff_kernel, tq=tq, scale=1.0 / math.sqrt(HEAD_DIM), lam_init=lam_init, g=g),
        grid=(batch, n_heads // g, nq),
        in_specs=[
            pl.BlockSpec((tq, g * dv), lambda b, h, i: (b * nq + i, q_col // g + h)),
            pl.BlockSpec((seq, g * dv), lambda b, h, i: (b, k_col // g + h)),
            pl.BlockSpec((seq, g * dv), lambda b, h, i: (b, v_col // g + h)),
            pl.BlockSpec((4, HEAD_DIM), lambda b, h, i: (0, 0)),
            pl.BlockSpec((1, dv), lambda b, h, i: (0, 0)),
        ],
        out_specs=pl.BlockSpec((tq, g * dv), lambda b, h, i: (b * nq + i, h)),
        out_shape=jax.ShapeDtypeStruct((batch * seq, n_heads * dv), BF16),
        scratch_shapes=[pltpu.VMEM((nq, g * dv, tq), BF16)],
        compiler_params=_params("parallel", "parallel", "arbitrary"),
        name="diff_attention",
    )(qkv, qkv, qkv, diff_lambda, subln_g.reshape(1, dv))


def _chunk_kernel(q_ref, k_ref, v_ref, tab_ref, o_ref, *, tq, scale, n_win, g):
    qi = pl.program_id(2)
    d = HEAD_DIM
    heads = [slice(hh * d, (hh + 1) * d) for hh in range(g)]
    kbs = [qi - (n_win - 1) + w for w in range(n_win)]
    starts = [pl.multiple_of(jnp.maximum(kb, 0) * tq, tq) for kb in kbs]
    scores = []
    for hh, cs in enumerate(heads):
        q = q_ref[:, cs]
        s_blocks = []
        for w in range(n_win):
            s = (_dot_nt(q, k_ref[pl.ds(starts[w], tq), cs]) * (scale * LOG2E)
                 + tab_ref[hh, :, w * tq:(w + 1) * tq])
            if w < n_win - 1:
                s = jnp.where(kbs[w] >= 0, s, NEG)
            s_blocks.append(s)
        scores.append(s_blocks)
    for cs, s_blocks in zip(heads, scores):
        m = s_blocks[0].max(axis=-1, keepdims=True)
        for s in s_blocks[1:]:
            m = jnp.maximum(m, s.max(axis=-1, keepdims=True))
        l = jnp.zeros_like(m)
        acc = jnp.zeros((tq, d), F32)
        for w, s in enumerate(s_blocks):
            p = jnp.exp2(s - m)
            l = l + jnp.sum(p, axis=-1, keepdims=True)
            acc = acc + _dot(p.astype(BF16), v_ref[pl.ds(starts[w], tq), cs])
        o_ref[:, cs] = (acc / l).astype(o_ref.dtype)


def chunk_bias_table(rel_bias, tq):
    left = LEFT_CHUNKS * CHUNK
    n_heads = rel_bias.shape[0]
    width = tq + left
    period = 2 * left
    assert left >= REL_CLIP and tq <= left
    rb = rel_bias.astype(F32)
    hi = jnp.broadcast_to(rb[:, -1:], (n_heads, left - REL_CLIP))
    mid = rb[:, ::-1]
    lo = jnp.broadcast_to(rb[:, :1], (n_heads, tq - REL_CLIP - 1))
    neg_d = jnp.broadcast_to(rb[:, -1:], (n_heads, period - width))
    diag = jnp.concatenate([hi, mid, lo, neg_d], axis=1)
    flat = jnp.tile(diag, (1, tq))[:, :tq * (period - 1)]
    bias = flat.reshape(n_heads, tq, period - 1)[:, :, :width]
    i = jnp.arange(tq)[:, None]
    j = jnp.arange(width)[None, :]
    qc, kc = i // CHUNK, j // CHUNK
    allowed = (kc >= qc) & (kc <= qc + LEFT_CHUNKS)
    return jnp.where(allowed[None], bias * LOG2E, NEG)


def chunk_attention(qkv, rel_bias, batch, seq, n_heads, q_col, k_col, v_col, g=CHUNK_HEADS_PER_STEP):
    tq = ATTN_TILE
    nq = seq // tq
    d = HEAD_DIM
    left = LEFT_CHUNKS * CHUNK
    assert left % tq == 0
    assert n_heads % g == 0 and q_col % g == 0 and k_col % g == 0 and v_col % g == 0
    n_win = left // tq + 1
    table = chunk_bias_table(rel_bias, tq)
    return pl.pallas_call(
        functools.partial(_chunk_kernel, tq=tq, scale=1.0 / math.sqrt(d), n_win=n_win, g=g),
        grid=(n_heads // g, batch, nq),
        in_specs=[
            pl.BlockSpec((tq, g * d), lambda h, b, i: (b * nq + i, q_col // g + h)),
            pl.BlockSpec((seq, g * d), lambda h, b, i: (b, k_col // g + h)),
            pl.BlockSpec((seq, g * d), lambda h, b, i: (b, v_col // g + h)),
            pl.BlockSpec((g, tq, tq + left), lambda h, b, i: (h, 0, 0)),
        ],
        out_specs=pl.BlockSpec((tq, g * d), lambda h, b, i: (b * nq + i, h)),
        out_shape=jax.ShapeDtypeStruct((batch * seq, n_heads * d), BF16),
        compiler_params=_params("parallel", "parallel", "arbitrary"),
        name="chunk_attention",
    )(qkv, qkv, qkv, table)


def rope_tables(seq):
    half = QK_ROPE // 2
    pos = jnp.arange(seq, dtype=F32)
    inv_freq = ROPE_THETA ** (-jnp.arange(0, QK_ROPE, 2, dtype=F32) / QK_ROPE)
    ang = pos[:, None] * inv_freq[None, :]
    cos, sin = jnp.cos(ang), jnp.sin(ang)
    z = lambda n: jnp.zeros((seq, n), F32)
    cos_t = jnp.concatenate([cos, cos, z(LANES - 2 * half)], axis=1)
    sin_a = jnp.concatenate([-sin, z(LANES - half)], axis=1)
    sin_b = jnp.concatenate([z(half), sin, z(LANES - 2 * half)], axis=1)
    return cos_t, sin_a, sin_b


def _rope(x, cos_t, sin_a, sin_b):
    half = QK_ROPE // 2
    return x * cos_t + pltpu.roll(x, LANES - half, 1) * sin_a + pltpu.roll(x, half, 1) * sin_b


def _mla_q_kernel(c_ref, g_ref, w_ref, cos_ref, sa_ref, sb_ref, o_ref, an_ref, *, hp):
    @pl.when(pl.program_id(1) == 0)
    def _():
        an_ref[...] = _rms(c_ref[...], g_ref[...]).astype(an_ref.dtype)

    hw = QK_NOPE + LANES
    res = _dot(an_ref[...], w_ref[...])
    cos_t, sin_a, sin_b = cos_ref[...], sa_ref[...], sb_ref[...]
    for hh in range(hp):
        c0 = hh * hw
        o_ref[:, c0:c0 + QK_NOPE] = res[:, c0:c0 + QK_NOPE].astype(o_ref.dtype)
        o_ref[:, c0 + QK_NOPE:c0 + hw] = _rope(res[:, c0 + QK_NOPE:c0 + hw],
                                               cos_t, sin_a, sin_b).astype(o_ref.dtype)


def mla_q(lat, g, w_q, tables, seq, n_heads, tm=512, hp=MLA_PROJ_HEADS_PER_STEP):
    t = lat.shape[0]
    kq = g.shape[0]
    hw = QK_NOPE + LANES
    npos = seq // tm
    tab_spec = pl.BlockSpec((tm, LANES), lambda i, j: (i % npos, 0))
    return pl.pallas_call(
        functools.partial(_mla_q_kernel, hp=hp),
        grid=(t // tm, n_heads // hp),
        in_specs=[
            pl.BlockSpec((tm, kq), lambda i, j: (i, 0)),
            pl.BlockSpec((1, kq), lambda i, j: (0, 0)),
            pl.BlockSpec((kq, hp * hw), lambda i, j: (0, j)),
            tab_spec, tab_spec, tab_spec,
        ],
        out_specs=pl.BlockSpec((tm, hp * hw), lambda i, j: (i, j)),
        out_shape=jax.ShapeDtypeStruct((t, n_heads * hw), BF16),
        scratch_shapes=[pltpu.VMEM((tm, kq), BF16)],
        compiler_params=_params("parallel", "arbitrary"),
        name="mla_q",
    )(lat, g.reshape(1, kq), w_q, *tables)


def _mla_kv_kernel(c_ref, kr_ref, g_ref, w_ref, cos_ref, sa_ref, sb_ref, k_ref, vt_ref,
                   an_ref, kr_scr, *, hp, tm, tk):
    @pl.when(pl.program_id(1) == 0)
    def _():
        an_ref[...] = _rms(c_ref[...], g_ref[...]).astype(an_ref.dtype)
        kr_scr[...] = _rope(kr_ref[...], cos_ref[...], sa_ref[...], sb_ref[...]).astype(kr_scr.dtype)

    hw = QK_NOPE + LANES
    wv = QK_NOPE + V_MLA
    res = _dot(an_ref[...], w_ref[...])
    kr = kr_scr[...]
    for hh in range(hp):
        k_ref[:, hh * hw:hh * hw + QK_NOPE] = res[:, hh * wv:hh * wv + QK_NOPE].astype(k_ref.dtype)
        k_ref[:, hh * hw + QK_NOPE:(hh + 1) * hw] = kr
        v = res[:, hh * wv + QK_NOPE:(hh + 1) * wv]
        for kt in range(tm // tk):
            vt_ref[kt, hh * V_MLA:(hh + 1) * V_MLA, :] = v[kt * tk:(kt + 1) * tk, :].T.astype(vt_ref.dtype)


def mla_kv(lat, g, w_kv, tables, seq, n_heads, ckv_col, kr_col, tm=512, hp=MLA_PROJ_HEADS_PER_STEP):
    t = lat.shape[0]
    kkv = g.shape[0]
    hw = QK_NOPE + LANES
    tk = ATTN_TILE
    assert tm % tk == 0
    npos = seq // tm
    tab_spec = pl.BlockSpec((tm, LANES), lambda i, j: (i % npos, 0))
    return pl.pallas_call(
        functools.partial(_mla_kv_kernel, hp=hp, tm=tm, tk=tk),
        grid=(t // tm, n_heads // hp),
        in_specs=[
            pl.BlockSpec((tm, kkv), lambda i, j: (i, ckv_col)),
            pl.BlockSpec((tm, LANES), lambda i, j: (i, kr_col)),
            pl.BlockSpec((1, kkv), lambda i, j: (0, 0)),
            pl.BlockSpec((kkv, hp * (QK_NOPE + V_MLA)), lambda i, j: (0, j)),
            tab_spec, tab_spec, tab_spec,
        ],
        out_specs=[pl.BlockSpec((tm, hp * hw), lambda i, j: (i, j)),
                   pl.BlockSpec((tm // tk, hp * V_MLA, tk), lambda i, j: (i, j, 0))],
        out_shape=[jax.ShapeDtypeStruct((t, n_heads * hw), BF16),
                   jax.ShapeDtypeStruct((t // tk, n_heads * V_MLA, tk), BF16)],
        scratch_shapes=[pltpu.VMEM((tm, kkv), BF16), pltpu.VMEM((tm, LANES), BF16)],
        compiler_params=_params("parallel", "arbitrary"),
        name="mla_kv",
    )(lat, lat, g.reshape(1, kkv), w_kv, *tables)


def _mla_attn_kernel(q_ref, k_ref, vt_ref, o_ref, *, tq, scale, g):
    qi = pl.program_id(2)
    hw = QK_NOPE + LANES
    dv = V_MLA
    qk_cols = [slice(hh * hw, (hh + 1) * hw) for hh in range(g)]
    v_cols = [slice(hh * dv, (hh + 1) * dv) for hh in range(g)]
    qs = [q_ref[:, cs] for cs in qk_cols]
    key, qry, _ = _chunk_causal(tq)
    visible = (key // CHUNK) <= (qry // CHUNK)

    def block(kb, carry, diag):
        start = pl.multiple_of(kb * tq, tq)
        ss = [_dot_nt(k_ref[pl.ds(start, tq), cs], q) * (scale * LOG2E) for q, cs in zip(qs, qk_cols)]
        if diag:
            ss = [jnp.where(visible, s, NEG) for s in ss]
        return tuple(_online_step_t(s, vt_ref[kb, vs, :], *st)
                     for s, vs, st in zip(ss, v_cols, carry))

    def body(kb, carry):
        return block(kb, carry, False)

    init = tuple((jnp.full((1, tq), NEG, F32), jnp.zeros((1, tq), F32), jnp.zeros((dv, tq), F32))
                 for _ in range(g))
    carry = lax.fori_loop(0, qi, body, init)
    carry = block(qi, carry, True)
    for vs, (_, l, acc) in zip(v_cols, carry):
        o_ref[:, vs] = (acc / l).T.astype(o_ref.dtype)


def mla_attention(q, k, vt, batch, seq, n_heads, g=MLA_HEADS_PER_STEP):
    tq = ATTN_TILE
    nq = seq // tq
    hw = QK_NOPE + LANES
    assert n_heads % g == 0
    return pl.pallas_call(
        functools.partial(_mla_attn_kernel, tq=tq, scale=1.0 / math.sqrt(QK_NOPE + QK_ROPE), g=g),
        grid=(batch, n_heads // g, nq),
        in_specs=[
            pl.BlockSpec((tq, g * hw), lambda b, h, i: (b * nq + i, h)),
            pl.BlockSpec((seq, g * hw), lambda b, h, i: (b, h), pipeline_mode=pl.Buffered(1)),
            pl.BlockSpec((nq, g * V_MLA, tq), lambda b, h, i: (b, h, 0), pipeline_mode=pl.Buffered(1)),
        ],
        out_specs=pl.BlockSpec((tq, g * V_MLA), lambda b, h, i: (b * nq + i, h)),
        out_shape=jax.ShapeDtypeStruct((batch * seq, n_heads * V_MLA), BF16),
        compiler_params=_params("parallel", "parallel", "arbitrary"),
        name="mla_attention",
    )(q, k, vt)


def _even_mixer(hn, w_in, w_out, i, diff_lambda, subln_g, batch, seq, layer):
    d_model = hn.shape[1]
    w_sb = d_model // 2
    n_sb = w_sb // HEAD_DIM
    n_diff = w_sb // (2 * HEAD_DIM)
    cb = w_sb // LANES
    qkv = matmul([(hn, w_in, i, 0)], w_in.shape[2], BF16, tm=1024, tn=512, name="even_in_proj")
    a = sb_attention(qkv, batch, seq, n_sb, 0, cb, 2 * cb)
    lam_init = 0.8 - 0.6 * math.exp(-0.3 * layer)
    bo = diff_attention(qkv, diff_lambda, subln_g, batch, seq, n_diff,
                        3 * cb // 2, 4 * cb // 2, 5 * cb // 2, lam_init)
    return matmul([(a, w_out, i, 0), (bo, w_out, i, 1)], d_model, F32, tm=1024, tn=512,
                  name="mix_out_proj")


def _odd_mixer(hn, w_in, w_out, i, rel_bias, q_norm_g, w_uq, kv_norm_g, w_ukv, batch, seq):
    d_model = hn.shape[1]
    w_ch = d_model // 2
    n_ch = w_ch // HEAD_DIM
    n_mla = w_ch // HEAD_DIM
    cb = w_ch // LANES
    q_lora, kv_lora = q_norm_g.shape[0], kv_norm_g.shape[0]
    n_attn = 3 * w_ch
    n_lat = q_lora + kv_lora + QK_ROPE
    lat_pad = -n_lat % LANES
    w_in_t = jnp.swapaxes(w_in, 1, 2)
    w_lat_t = row_slab_bf16(w_in_t, i, n_attn, n_lat + lat_pad)
    qkv = matmul([(hn, w_in_t, i, 0)], n_attn, BF16, tm=1024, tn=512, name="odd_in_proj", w_t=True)
    lat = matmul([(hn, w_lat_t, 0, 0)], n_lat + lat_pad, F32, tm=512, tn=n_lat + lat_pad,
                 name="odd_lat_proj", w_t=True)
    c = chunk_attention(qkv, rel_bias, batch, seq, n_ch, 0, cb, 2 * cb)

    tables = rope_tables(seq)
    hw = QK_NOPE + LANES
    wq = w_uq.reshape(q_lora, n_mla, QK_NOPE + QK_ROPE)
    wq = jnp.pad(wq, ((0, 0), (0, 0), (0, hw - QK_NOPE - QK_ROPE))).reshape(q_lora, n_mla * hw).astype(BF16)
    q = mla_q(lat, q_norm_g, wq, tables, seq, n_mla)
    assert q_lora % kv_lora == 0 and (q_lora + kv_lora) % LANES == 0
    k, v = mla_kv(lat, kv_norm_g, w_ukv.astype(BF16), tables, seq, n_mla,
                  q_lora // kv_lora, (q_lora + kv_lora) // LANES)
    dm = mla_attention(q, k, v, batch, seq, n_mla)
    return matmul([(c, w_out, i, 0), (dm, w_out, i, 1)], d_model, F32, tm=1024, tn=512,
                  name="mix_out_proj")


def _ffn(h, w_in, conv_w, conv_b, w_out, layer, seq):
    g, w_out16 = ffn_in(h, w_in, conv_w, conv_b, w_out, layer, seq)
    return matmul([(g, w_out16[None], 0, 0)], w_out.shape[2], F32, tm=512, tn=256,
                  name="ffn_out_proj")


def kernel(x, norm_g, even_w_in, even_w_out, diff_lambda, diff_subln_g, odd_w_in, odd_w_out,
           ch_rel_bias, mla_q_norm_g, mla_w_uq, mla_kv_norm_g, mla_w_ukv, ffn_w_in, ffn_conv_w,
           ffn_conv_b, ffn_w_out):
    batch, seq, d_model = x.shape
    depth = norm_g.shape[0]
    xf = x.reshape(batch * seq, d_model)
    hn = norm_cast(xf, norm_g[0, 0])
    for layer in range(depth):
        g = norm_g[layer]
        i = layer // 2
        if layer % 2 == 0:
            mix = _even_mixer(hn, even_w_in, even_w_out, i, diff_lambda[i], diff_subln_g[i],
                              batch, seq, layer)
        else:
            mix = _odd_mixer(hn, odd_w_in, odd_w_out, i, ch_rel_bias[i], mla_q_norm_g[i],
                             mla_w_uq[i], mla_kv_norm_g[i], mla_w_ukv[i], batch, seq)
        xf, h2 = resid_norm(xf, mix, g[1], g[2])
        f = _ffn(h2, ffn_w_in, ffn_conv_w, ffn_conv_b, ffn_w_out, layer, seq)
        if layer + 1 < depth:
            xf, hn = resid_norm(xf, f, g[3], norm_g[layer + 1, 0])
        else:
            xf = resid(xf, f, g[3])
    return xf.reshape(batch, seq, d_model)
```

```python
import functools
import math

import jax
import jax.numpy as jnp
from jax import lax
from jax.experimental import pallas as pl
from jax.experimental.pallas import tpu as pltpu

F32 = jnp.float32
BF16 = jnp.bfloat16

CHUNK = 64
HEAD_DIM = 128
LEFT_CHUNKS = 8
REL_CLIP = 128
QK_NOPE = 128
QK_ROPE = 64
V_MLA = 128
ROPE_THETA = 10000.0
CONV_W = 3
EPS = 1e-6
NEG = -1e30
LOG2E = math.log2(math.e)

LANES = 128
BF16_SUBLANES = 16
VMEM_LIMIT = 52 * 1024 * 1024

ATTN_TILE = 256
SB_HEADS_PER_STEP = 8
MLA_HEADS_PER_STEP = 8
DIFF_HEADS_PER_STEP = 4
CHUNK_HEADS_PER_STEP = 4
MLA_PROJ_HEADS_PER_STEP = 4


def _params(*sem):
    return pltpu.CompilerParams(dimension_semantics=sem, vmem_limit_bytes=VMEM_LIMIT)


def _rms(x, g):
    ms = jnp.mean(x * x, axis=-1, keepdims=True)
    return x * lax.rsqrt(ms + EPS) * g


def _dot(a, b):
    return jnp.dot(a, b, preferred_element_type=F32)


def _dot_nt(a, b):
    return lax.dot_general(a, b, (((1,), (1,)), ((), ())), preferred_element_type=F32)


def _norm_cast_kernel(x_ref, g_ref, h_ref):
    h_ref[...] = _rms(x_ref[...], g_ref[...]).astype(h_ref.dtype)


def _resid_norm_kernel(x_ref, y_ref, g1_ref, g2_ref, xo_ref, h_ref):
    xn = x_ref[...] + _rms(y_ref[...], g1_ref[...])
    xo_ref[...] = xn
    h_ref[...] = _rms(xn, g2_ref[...]).astype(h_ref.dtype)


def _resid_kernel(x_ref, y_ref, g_ref, xo_ref):
    xo_ref[...] = x_ref[...] + _rms(y_ref[...], g_ref[...])


def _row_spec(tr, d):
    return pl.BlockSpec((tr, d), lambda i: (i, 0))


def _vec_spec(d):
    return pl.BlockSpec((1, d), lambda i: (0, 0))


def norm_cast(x, g, tr=256):
    t, d = x.shape
    return pl.pallas_call(
        _norm_cast_kernel,
        grid=(t // tr,),
        in_specs=[_row_spec(tr, d), _vec_spec(d)],
        out_specs=_row_spec(tr, d),
        out_shape=jax.ShapeDtypeStruct((t, d), BF16),
        compiler_params=_params("parallel"),
        name="norm_cast",
    )(x, g.reshape(1, d))


def resid_norm(x, y, g1, g2, tr=256):
    t, d = x.shape
    return pl.pallas_call(
        _resid_norm_kernel,
        grid=(t // tr,),
        in_specs=[_row_spec(tr, d), _row_spec(tr, d), _vec_spec(d), _vec_spec(d)],
        out_specs=[_row_spec(tr, d), _row_spec(tr, d)],
        out_shape=[jax.ShapeDtypeStruct((t, d), F32), jax.ShapeDtypeStruct((t, d), BF16)],
        compiler_params=_params("parallel"),
        name="resid_norm",
    )(x, y, g1.reshape(1, d), g2.reshape(1, d))


def resid(x, y, g, tr=256):
    t, d = x.shape
    return pl.pallas_call(
        _resid_kernel,
        grid=(t // tr,),
        in_specs=[_row_spec(tr, d), _row_spec(tr, d), _vec_spec(d)],
        out_specs=_row_spec(tr, d),
        out_shape=jax.ShapeDtypeStruct((t, d), F32),
        compiler_params=_params("parallel"),
        name="resid",
    )(x, y, g.reshape(1, d))


def _matmul_kernel(*refs, n_pairs, w_t):
    o_ref = refs[2 * n_pairs]
    dot = _dot_nt if w_t else _dot
    acc = dot(refs[0][...], refs[n_pairs][...].astype(BF16))
    for p in range(1, n_pairs):
        acc = acc + dot(refs[p][...], refs[n_pairs + p][...].astype(BF16))
    o_ref[...] = acc.astype(o_ref.dtype)


def matmul(pairs, n, out_dtype, tm, tn, name, w_t=False, a_single_buffer=False):
    m = pairs[0][0].shape[0]
    n_pairs = len(pairs)
    a_specs, w_specs, args_a, args_w = [], [], [], []
    for a, w, layer, rb in pairs:
        k = a.shape[1]
        mode = dict(pipeline_mode=pl.Buffered(1)) if a_single_buffer else {}
        a_specs.append(pl.BlockSpec((tm, k), lambda i, j: (i, 0), **mode))
        if w_t:
            w_specs.append(pl.BlockSpec((None, tn, k), lambda i, j, layer=layer, rb=rb: (layer, j, rb)))
        else:
            w_specs.append(pl.BlockSpec((None, k, tn), lambda i, j, layer=layer, rb=rb: (layer, rb, j)))
        args_a.append(a)
        args_w.append(w)
    return pl.pallas_call(
        functools.partial(_matmul_kernel, n_pairs=n_pairs, w_t=w_t),
        grid=(m // tm, n // tn),
        in_specs=a_specs + w_specs,
        out_specs=pl.BlockSpec((tm, tn), lambda i, j: (i, j)),
        out_shape=jax.ShapeDtypeStruct((m, n), out_dtype),
        compiler_params=_params("parallel", "arbitrary"),
        name=name,
    )(*args_a, *args_w)


def _row_slab_kernel(w_ref, o_ref, *, row0, n_rows):
    row = row0 + pl.program_id(0) * LANES + lax.broadcasted_iota(jnp.int32, w_ref.shape, 0)
    o_ref[...] = jnp.where(row < n_rows, w_ref[...], 0.0).astype(o_ref.dtype)


def row_slab_bf16(w_t, layer, row0, height):
    _, n_rows, k = w_t.shape
    assert row0 % LANES == 0 and height % LANES == 0
    return pl.pallas_call(
        functools.partial(_row_slab_kernel, row0=row0, n_rows=n_rows),
        grid=(height // LANES,),
        in_specs=[pl.BlockSpec((None, LANES, k), lambda j: (layer, row0 // LANES + j, 0))],
        out_specs=pl.BlockSpec((None, LANES, k), lambda j: (0, j, 0)),
        out_shape=jax.ShapeDtypeStruct((1, height, k), BF16),
        compiler_params=_params("parallel"),
        name="row_slab_bf16",
    )(w_t)


def _gelu_tanh(x):
    c = math.sqrt(2.0 / math.pi)
    return x * (0.5 * (1.0 + jnp.tanh(c * (x + 0.044715 * (x * x * x)))))


def _ffn_in_kernel(a_ref, ah_ref, wg_ref, wv_ref, cwg_ref, cwv_ref, cbg_ref, cbv_ref, wo_ref,
                   o_ref, wo16_ref, *, tm, seq):
    i = pl.program_id(0)
    wo16_ref[...] = wo_ref[...].astype(wo16_ref.dtype)
    tn = o_ref.shape[1]
    a = a_ref[...]
    ah = ah_ref[...]
    seq_start = (i * tm) % seq == 0
    row8 = lax.broadcasted_iota(jnp.int32, (8, tn), 0)

    def conv(w_ref, cw_ref, cb_ref):
        w = w_ref[...].astype(BF16)
        u = _dot(a, w)
        uh = _dot(ah, w)
        uh = jnp.where(seq_start, 0.0, uh)
        p1 = uh[BF16_SUBLANES - 1:BF16_SUBLANES, :]
        p2 = uh[BF16_SUBLANES - 2:BF16_SUBLANES - 1, :]
        r1 = pltpu.roll(u, 1, 0)
        r2 = pltpu.roll(u, 2, 0)
        h1 = jnp.where(row8 == 0, p1, r1[:8, :])
        h2 = jnp.where(row8 == 0, p2, jnp.where(row8 == 1, p1, r2[:8, :]))
        u1 = jnp.concatenate([h1, r1[8:, :]], axis=0)
        u2 = jnp.concatenate([h2, r2[8:, :]], axis=0)
        cw = cw_ref[...]
        return cb_ref[...] + (cw[0:1, :] * u2 + cw[1:2, :] * u1 + cw[2:3, :] * u)

    gate = conv(wg_ref, cwg_ref, cbg_ref)
    val = conv(wv_ref, cwv_ref, cbv_ref)
    o_ref[...] = (_gelu_tanh(gate) * val).astype(o_ref.dtype)


def ffn_in(h, w_in, conv_w, conv_b, w_out, layer, seq, tm=2048, tn=256):
    t, k = h.shape
    f = w_in.shape[2] // 2
    d_out = w_out.shape[2]
    nf = f // tn
    n_steps = (t // tm) * nf
    assert f % (n_steps * BF16_SUBLANES) == 0
    slab = f // n_steps
    halo = BF16_SUBLANES
    hb = tm // halo
    conv_b = conv_b.reshape(conv_b.shape[0], 1, 2 * f)
    return pl.pallas_call(
        functools.partial(_ffn_in_kernel, tm=tm, seq=seq),
        grid=(t // tm, nf),
        in_specs=[
            pl.BlockSpec((tm, k), lambda i, j: (i, 0), pipeline_mode=pl.Buffered(1)),
            pl.BlockSpec((halo, k), lambda i, j: (jnp.maximum(i * hb - 1, 0), 0)),
            pl.BlockSpec((None, k, tn), lambda i, j: (layer, 0, j)),
            pl.BlockSpec((None, k, tn), lambda i, j: (layer, 0, j + nf)),
            pl.BlockSpec((None, CONV_W, tn), lambda i, j: (layer, 0, j)),
            pl.BlockSpec((None, CONV_W, tn), lambda i, j: (layer, 0, j + nf)),
            pl.BlockSpec((None, 1, tn), lambda i, j: (layer, 0, j)),
            pl.BlockSpec((None, 1, tn), lambda i, j: (layer, 0, j + nf)),
            pl.BlockSpec((None, slab, d_out), lambda i, j: (layer, i * nf + j, 0)),
        ],
        out_specs=[pl.BlockSpec((tm, tn), lambda i, j: (i, j)),
                   pl.BlockSpec((slab, d_out), lambda i, j: (i * nf + j, 0))],
        out_shape=[jax.ShapeDtypeStruct((t, f), BF16),
                   jax.ShapeDtypeStruct((f, d_out), BF16)],
        compiler_params=_params("parallel", "arbitrary"),
        name="ffn_in",
    )(h, h, w_in, w_in, conv_w, conv_w, conv_b, conv_b, w_out)


def _transpose_value_tiles(v_ref, vt_scr, tq):
    def transpose_tile(kb, c):
        start = pl.multiple_of(kb * tq, tq)
        vt_scr[kb] = v_ref[pl.ds(start, tq), :].astype(F32).T.astype(vt_scr.dtype)
        return c
    lax.fori_loop(0, vt_scr.shape[0], transpose_tile, 0)


def _sb_kernel(q_ref, k_ref, v_ref, o_ref, vt_scr, *, tq, scale, g):
    qi = pl.program_id(2)
    d = HEAD_DIM
    heads = [slice(hh * d, (hh + 1) * d) for hh in range(g)]
    qs = [q_ref[:, cs] for cs in heads]

    @pl.when(qi == 0)
    def _():
        _transpose_value_tiles(v_ref, vt_scr, tq)

    key = lax.broadcasted_iota(jnp.int32, (tq, tq), 0)
    qry = lax.broadcasted_iota(jnp.int32, (tq, tq), 1)
    strict = key < qry
    r2 = lax.broadcasted_iota(jnp.int32, (tq, 2 * tq), 0)
    c2 = lax.broadcasted_iota(jnp.int32, (tq, 2 * tq), 1)
    tri = (jnp.where(c2 >= tq, c2 - tq, c2) >= r2).astype(BF16)

    def block(kb, carry, diag):
        start = pl.multiple_of(kb * tq, tq)
        zs = [_dot_nt(k_ref[pl.ds(start, tq), cs], q) * (scale * LOG2E) for q, cs in zip(qs, heads)]
        csums = []
        for z in zs:
            neg_abs = lax.bitcast_convert_type(
                lax.bitcast_convert_type(z, jnp.uint32) | jnp.uint32(0x80000000), F32)
            sp = jnp.maximum(z, 0.0) + jnp.log2(1.0 + jnp.exp2(neg_abs))
            if diag:
                sp = jnp.where(strict, sp, 0.0)
            hi = sp.astype(BF16)
            lo = (sp - hi.astype(F32)).astype(BF16)
            csums.append(_dot(tri, jnp.concatenate([hi, lo], axis=0)))
        out = []
        for z, csum, cs, (rsum, acc) in zip(zs, csums, heads, carry):
            w = jnp.exp2(z - csum - rsum)
            if diag:
                w = jnp.where(strict, w, 0.0)
            acc = acc + _dot(vt_scr[kb, cs, :], w.astype(BF16))
            rsum = rsum + csum[0:1, :]
            out.append((rsum, acc))
        return tuple(out)

    init = tuple((jnp.zeros((1, tq), F32), jnp.zeros((d, tq), F32)) for _ in range(g))
    carry = block(qi, init, True)

    def body(it, carry):
        return block(qi - 1 - it, carry, False)

    carry = lax.fori_loop(0, qi, body, carry)
    for cs, (_, acc) in zip(heads, carry):
        o_ref[:, cs] = acc.T.astype(o_ref.dtype)


def sb_attention(qkv, batch, seq, n_heads, q_col, k_col, v_col, g=SB_HEADS_PER_STEP):
    tq = ATTN_TILE
    nq = seq // tq
    d = HEAD_DIM
    assert n_heads % g == 0 and q_col % g == 0 and k_col % g == 0 and v_col % g == 0
    return pl.pallas_call(
        functools.partial(_sb_kernel, tq=tq, scale=1.0 / math.sqrt(d), g=g),
        grid=(batch, n_heads // g, nq),
        in_specs=[
            pl.BlockSpec((tq, g * d), lambda b, h, i: (b * nq + i, q_col // g + h)),
            pl.BlockSpec((seq, g * d), lambda b, h, i: (b, k_col // g + h)),
            pl.BlockSpec((seq, g * d), lambda b, h, i: (b, v_col // g + h)),
        ],
        out_specs=pl.BlockSpec((tq, g * d), lambda b, h, i: (b * nq + i, h)),
        out_shape=jax.ShapeDtypeStruct((batch * seq, n_heads * d), BF16),
        scratch_shapes=[pltpu.VMEM((nq, g * d, tq), BF16)],
        compiler_params=_params("parallel", "parallel", "arbitrary"),
        name="sb_attention",
    )(qkv, qkv, qkv)


def _online_step_t(s, vt, m, l, acc):
    m_new = jnp.maximum(m, jnp.max(s, axis=0, keepdims=True))
    alpha = jnp.exp2(m - m_new)
    p = jnp.exp2(s - m_new)
    l = alpha * l + jnp.sum(p, axis=0, keepdims=True)
    acc = alpha * acc + _dot(vt, p.astype(BF16))
    return m_new, l, acc


def _chunk_causal(tq):
    row = lax.broadcasted_iota(jnp.int32, (tq, tq), 0)
    col = lax.broadcasted_iota(jnp.int32, (tq, tq), 1)
    return row, col, (col // CHUNK) <= (row // CHUNK)


def _diff_kernel(q_ref, k_ref, v_ref, lam_ref, g_ref, o_ref, vt_scr, *, tq, scale, lam_init, g):
    hg = pl.program_id(1)
    qi = pl.program_id(2)
    d = HEAD_DIM
    dv = 2 * HEAD_DIM
    qk_cols = [slice(c * d, (c + 1) * d) for c in range(2 * g)]
    v_cols = [slice(hh * dv, (hh + 1) * dv) for hh in range(g)]
    qs = [q_ref[:, cs] for cs in qk_cols]

    @pl.when(qi == 0)
    def _():
        _transpose_value_tiles(v_ref, vt_scr, tq)

    key, qry, _ = _chunk_causal(tq)
    visible = (key // CHUNK) <= (qry // CHUNK)
    dist = (qry - key).astype(F32)
    nbias_diag, nbias_off, slopes = [], [], []
    for hh in range(g):
        slope = lax.bitcast_convert_type(
            jnp.full((1, 1), (126 - (hg * g + hh)) * (1 << 23), jnp.int32), F32)
        slope = slope * LOG2E
        slopes.append(slope)
        nbias_diag.append(-slope * jnp.abs(dist))
        nbias_off.append(-slope * dist)

    def block(kb, carry, diag):
        start = pl.multiple_of(kb * tq, tq)
        ss = [_dot_nt(k_ref[pl.ds(start, tq), cs], q) * (scale * LOG2E) for q, cs in zip(qs, qk_cols)]
        for c in range(2 * g):
            hh = c // 2
            if diag:
                ss[c] = jnp.where(visible, ss[c] + nbias_diag[hh], NEG)
            else:
                ss[c] = ss[c] + (nbias_off[hh] - slopes[hh] * ((qi - kb) * tq).astype(F32))
        return tuple(_online_step_t(ss[c], vt_scr[kb, v_cols[c // 2], :], *carry[c])
                     for c in range(2 * g))

    def body(kb, carry):
        return block(kb, carry, False)

    init = tuple((jnp.full((1, tq), NEG, F32), jnp.zeros((1, tq), F32), jnp.zeros((dv, tq), F32))
                 for _ in range(2 * g))
    carry = lax.fori_loop(0, qi, body, init)
    carry = block(qi, carry, True)

    lf = lam_ref[...]
    lam = (jnp.exp(jnp.sum(lf[0:1, :] * lf[1:2, :], axis=-1, keepdims=True))
           - jnp.exp(jnp.sum(lf[2:3, :] * lf[3:4, :], axis=-1, keepdims=True)) + lam_init)
    for hh in range(g):
        (_, l1, a1), (_, l2, a2) = carry[2 * hh], carry[2 * hh + 1]
        o = (a1 / l1 - lam * (a2 / l2)).T
        o_ref[:, v_cols[hh]] = (_rms(o, g_ref[...]) * (1.0 - lam_init)).astype(o_ref.dtype)


def diff_attention(qkv, diff_lambda, subln_g, batch, seq, n_heads, q_col, k_col, v_col, lam_init,
                   g=DIFF_HEADS_PER_STEP):
    tq = ATTN_TILE
    nq = seq // tq
    dv = 2 * HEAD_DIM
    assert n_heads == 8, "ALiBi slopes are built as exact powers of two"
    assert n_heads % g == 0 and q_col % g == 0 and k_col % g == 0 and v_col % g == 0
    return pl.pallas_call(
        functools.partial(_diff_kernel, tq=tq, scale=1.0 / math.sqrt(HEAD_DIM), lam_init=lam_init, g=g),
        grid=(batch, n_heads // g, nq),
        in_specs=[
            pl.BlockSpec((tq, g * dv), lambda b, h, i: (b * nq + i, q_col // g + h)),
            pl.BlockSpec((seq, g * dv), lambda b, h, i: (b, k_col // g + h)),
            pl.BlockSpec((seq, g * dv), lambda b, h, i: (b, v_col // g + h)),
            pl.BlockSpec((4, HEAD_DIM), lambda b, h, i: (0, 0)),
            pl.BlockSpec((1, dv), lambda b, h, i: (0, 0)),
        ],
        out_specs=pl.BlockSpec((tq, g * dv), lambda b, h, i: (b * nq + i, h)),
        out_shape=jax.ShapeDtypeStruct((batch * seq, n_heads * dv), BF16),
        scratch_shapes=[pltpu.VMEM((nq, g * dv, tq), BF16)],
        compiler_params=_params("parallel", "parallel", "arbitrary"),
        name="diff_attention",
    )(qkv, qkv, qkv, diff_lambda, subln_g.reshape(1, dv))


def _chunk_kernel(q_ref, k_ref, v_ref, tab_ref, o_ref, *, tq, scale, n_win, g):
    qi = pl.program_id(2)
    d = HEAD_DIM
    heads = [slice(hh * d, (hh + 1) * d) for hh in range(g)]
    kbs = [qi - (n_win - 1) + w for w in range(n_win)]
    starts = [pl.multiple_of(jnp.maximum(kb, 0) * tq, tq) for kb in kbs]
    scores = []
    for hh, cs in enumerate(heads):
        q = q_ref[:, cs]
        s_blocks = []
        for w in range(n_win):
            s = (_dot_nt(q, k_ref[pl.ds(starts[w], tq), cs]) * (scale * LOG2E)
                 + tab_ref[hh, :, w * tq:(w + 1) * tq])
            if w < n_win - 1:
                s = jnp.where(kbs[w] >= 0, s, NEG)
            s_blocks.append(s)
        scores.append(s_blocks)
    for cs, s_blocks in zip(heads, scores):
        m = s_blocks[0].max(axis=-1, keepdims=True)
        for s in s_blocks[1:]:
            m = jnp.maximum(m, s.max(axis=-1, keepdims=True))
        l = jnp.zeros_like(m)
        acc = jnp.zeros((tq, d), F32)
        for w, s in enumerate(s_blocks):
            p = jnp.exp2(s - m)
            l = l + jnp.sum(p, axis=-1, keepdims=True)
            acc = acc + _dot(p.astype(BF16), v_ref[pl.ds(starts[w], tq), cs])
        o_ref[:, cs] = (acc / l).astype(o_ref.dtype)


def chunk_bias_table(rel_bias, tq):
    left = LEFT_CHUNKS * CHUNK
    n_heads = rel_bias.shape[0]
    width = tq + left
    period = 2 * left
    assert left >= REL_CLIP and tq <= left
    rb = rel_bias.astype(F32)
    hi = jnp.broadcast_to(rb[:, -1:], (n_heads, left - REL_CLIP))
    mid = rb[:, ::-1]
    lo = jnp.broadcast_to(rb[:, :1], (n_heads, tq - REL_CLIP - 1))
    neg_d = jnp.broadcast_to(rb[:, -1:], (n_heads, period - width))
    diag = jnp.concatenate([hi, mid, lo, neg_d], axis=1)
    flat = jnp.tile(diag, (1, tq))[:, :tq * (period - 1)]
    bias = flat.reshape(n_heads, tq, period - 1)[:, :, :width]
    i = jnp.arange(tq)[:, None]
    j = jnp.arange(width)[None, :]
    qc, kc = i // CHUNK, j // CHUNK
    allowed = (kc >= qc) & (kc <= qc + LEFT_CHUNKS)
    return jnp.where(allowed[None], bias * LOG2E, NEG)


def chunk_attention(qkv, rel_bias, batch, seq, n_heads, q_col, k_col, v_col, g=CHUNK_HEADS_PER_STEP):
    tq = ATTN_TILE
    nq = seq // tq
    d = HEAD_DIM
    left = LEFT_CHUNKS * CHUNK
    assert left % tq == 0
    assert n_heads % g == 0 and q_col % g == 0 and k_col % g == 0 and v_col % g == 0
    n_win = left // tq + 1
    table = chunk_bias_table(rel_bias, tq)
    return pl.pallas_call(
        functools.partial(_chunk_kernel, tq=tq, scale=1.0 / math.sqrt(d), n_win=n_win, g=g),
        grid=(n_heads // g, batch, nq),
        in_specs=[
            pl.BlockSpec((tq, g * d), lambda h, b, i: (b * nq + i, q_col // g + h)),
            pl.BlockSpec((seq, g * d), lambda h, b, i: (b, k_col // g + h)),
            pl.BlockSpec((seq, g * d), lambda h, b, i: (b, v_col // g + h)),
            pl.BlockSpec((g, tq, tq + left), lambda h, b, i: (h, 0, 0)),
        ],
        out_specs=pl.BlockSpec((tq, g * d), lambda h, b, i: (b * nq + i, h)),
        out_shape=jax.ShapeDtypeStruct((batch * seq, n_heads * d), BF16),
        compiler_params=_params("parallel", "parallel", "arbitrary"),
        name="chunk_attention",
    )(qkv, qkv, qkv, table)


def rope_tables(seq):
    half = QK_ROPE // 2
    pos = jnp.arange(seq, dtype=F32)
    inv_freq = ROPE_THETA ** (-jnp.arange(0, QK_ROPE, 2, dtype=F32) / QK_ROPE)
    ang = pos[:, None] * inv_freq[None, :]
    cos, sin = jnp.cos(ang), jnp.sin(ang)
    z = lambda n: jnp.zeros((seq, n), F32)
    cos_t = jnp.concatenate([cos, cos, z(LANES - 2 * half)], axis=1)
    sin_a = jnp.concatenate([-sin, z(LANES - half)], axis=1)
    sin_b = jnp.concatenate([z(half), sin, z(LANES - 2 * half)], axis=1)
    return cos_t, sin_a, sin_b


def _rope(x, cos_t, sin_a, sin_b):
    half = QK_ROPE // 2
    return x * cos_t + pltpu.roll(x, LANES - half, 1) * sin_a + pltpu.roll(x, half, 1) * sin_b


def _mla_q_kernel(c_ref, g_ref, w_ref, cos_ref, sa_ref, sb_ref, o_ref, an_ref, *, hp):
    @pl.when(pl.program_id(1) == 0)
    def _():
        an_ref[...] = _rms(c_ref[...], g_ref[...]).astype(an_ref.dtype)

    hw = QK_NOPE + LANES
    res = _dot(an_ref[...], w_ref[...])
    cos_t, sin_a, sin_b = cos_ref[...], sa_ref[...], sb_ref[...]
    for hh in range(hp):
        c0 = hh * hw
        o_ref[:, c0:c0 + QK_NOPE] = res[:, c0:c0 + QK_NOPE].astype(o_ref.dtype)
        o_ref[:, c0 + QK_NOPE:c0 + hw] = _rope(res[:, c0 + QK_NOPE:c0 + hw],
                                               cos_t, sin_a, sin_b).astype(o_ref.dtype)


def mla_q(lat, g, w_q, tables, seq, n_heads, tm=512, hp=MLA_PROJ_HEADS_PER_STEP):
    t = lat.shape[0]
    kq = g.shape[0]
    hw = QK_NOPE + LANES
    npos = seq // tm
    tab_spec = pl.BlockSpec((tm, LANES), lambda i, j: (i % npos, 0))
    return pl.pallas_call(
        functools.partial(_mla_q_kernel, hp=hp),
        grid=(t // tm, n_heads // hp),
        in_specs=[
            pl.BlockSpec((tm, kq), lambda i, j: (i, 0)),
            pl.BlockSpec((1, kq), lambda i, j: (0, 0)),
            pl.BlockSpec((kq, hp * hw), lambda i, j: (0, j)),
            tab_spec, tab_spec, tab_spec,
        ],
        out_specs=pl.BlockSpec((tm, hp * hw), lambda i, j: (i, j)),
        out_shape=jax.ShapeDtypeStruct((t, n_heads * hw), BF16),
        scratch_shapes=[pltpu.VMEM((tm, kq), BF16)],
        compiler_params=_params("parallel", "arbitrary"),
        name="mla_q",
    )(lat, g.reshape(1, kq), w_q, *tables)


def _mla_kv_kernel(c_ref, kr_ref, g_ref, w_ref, cos_ref, sa_ref, sb_ref, k_ref, vt_ref,
                   an_ref, kr_scr, *, hp, tm, tk):
    @pl.when(pl.program_id(1) == 0)
    def _():
        an_ref[...] = _rms(c_ref[...], g_ref[...]).astype(an_ref.dtype)
        kr_scr[...] = _rope(kr_ref[...], cos_ref[...], sa_ref[...], sb_ref[...]).astype(kr_scr.dtype)

    hw = QK_NOPE + LANES
    wv = QK_NOPE + V_MLA
    res = _dot(an_ref[...], w_ref[...])
    kr = kr_scr[...]
    for hh in range(hp):
        k_ref[:, hh * hw:hh * hw + QK_NOPE] = res[:, hh * wv:hh * wv + QK_NOPE].astype(k_ref.dtype)
        k_ref[:, hh * hw + QK_NOPE:(hh + 1) * hw] = kr
        v = res[:, hh * wv + QK_NOPE:(hh + 1) * wv]
        for kt in range(tm // tk):
            vt_ref[kt, hh * V_MLA:(hh + 1) * V_MLA, :] = v[kt * tk:(kt + 1) * tk, :].T.astype(vt_ref.dtype)


def mla_kv(lat, g, w_kv, tables, seq, n_heads, ckv_col, kr_col, tm=512, hp=MLA_PROJ_HEADS_PER_STEP):
    t = lat.shape[0]
    kkv = g.shape[0]
    hw = QK_NOPE + LANES
    tk = ATTN_TILE
    assert tm % tk == 0
    npos = seq // tm
    tab_spec = pl.BlockSpec((tm, LANES), lambda i, j: (i % npos, 0))
    return pl.pallas_call(
        functools.partial(_mla_kv_kernel, hp=hp, tm=tm, tk=tk),
        grid=(t // tm, n_heads // hp),
        in_specs=[
            pl.BlockSpec((tm, kkv), lambda i, j: (i, ckv_col)),
            pl.BlockSpec((tm, LANES), lambda i, j: (i, kr_col)),
            pl.BlockSpec((1, kkv), lambda i, j: (0, 0)),
            pl.BlockSpec((kkv, hp * (QK_NOPE + V_MLA)), lambda i, j: (0, j)),
            tab_spec, tab_spec, tab_spec,
        ],
        out_specs=[pl.BlockSpec((tm, hp * hw), lambda i, j: (i, j)),
                   pl.BlockSpec((tm // tk, hp * V_MLA, tk), lambda i, j: (i, j, 0))],
        out_shape=[jax.ShapeDtypeStruct((t, n_heads * hw), BF16),
                   jax.ShapeDtypeStruct((t // tk, n_heads * V_MLA, tk), BF16)],
        scratch_shapes=[pltpu.VMEM((tm, kkv), BF16), pltpu.VMEM((tm, LANES), BF16)],
        compiler_params=_params("parallel", "arbitrary"),
        name="mla_kv",
    )(lat, lat, g.reshape(1, kkv), w_kv, *tables)


def _mla_attn_kernel(q_ref, k_ref, vt_ref, o_ref, *, tq, scale, g):
    qi = pl.program_id(2)
    hw = QK_NOPE + LANES
    dv = V_MLA
    qk_cols = [slice(hh * hw, (hh + 1) * hw) for hh in range(g)]
    v_cols = [slice(hh * dv, (hh + 1) * dv) for hh in range(g)]
    qs = [q_ref[:, cs] for cs in qk_cols]
    key, qry, _ = _chunk_causal(tq)
    visible = (key // CHUNK) <= (qry // CHUNK)

    def block(kb, carry, diag):
        start = pl.multiple_of(kb * tq, tq)
        ss = [_dot_nt(k_ref[pl.ds(start, tq), cs], q) * (scale * LOG2E) for q, cs in zip(qs, qk_cols)]
        if diag:
            ss = [jnp.where(visible, s, NEG) for s in ss]
        return tuple(_online_step_t(s, vt_ref[kb, vs, :], *st)
                     for s, vs, st in zip(ss, v_cols, carry))

    def body(kb, carry):
        return block(kb, carry, False)

    init = tuple((jnp.full((1, tq), NEG, F32), jnp.zeros((1, tq), F32), jnp.zeros((dv, tq), F32))
                 for _ in range(g))
    carry = lax.fori_loop(0, qi, body, init)
    carry = block(qi, carry, True)
    for vs, (_, l, acc) in zip(v_cols, carry):
        o_ref[:, vs] = (acc / l).T.astype(o_ref.dtype)


def mla_attention(q, k, vt, batch, seq, n_heads, g=MLA_HEADS_PER_STEP):
    tq = ATTN_TILE
    nq = seq // tq
    hw = QK_NOPE + LANES
    assert n_heads % g == 0
    return pl.pallas_call(
        functools.partial(_mla_attn_kernel, tq=tq, scale=1.0 / math.sqrt(QK_NOPE + QK_ROPE), g=g),
        grid=(batch, n_heads // g, nq),
        in_specs=[
            pl.BlockSpec((tq, g * hw), lambda b, h, i: (b * nq + i, h)),
            pl.BlockSpec((seq, g * hw), lambda b, h, i: (b, h), pipeline_mode=pl.Buffered(1)),
            pl.BlockSpec((nq, g * V_MLA, tq), lambda b, h, i: (b, h, 0), pipeline_mode=pl.Buffered(1)),
        ],
        out_specs=pl.BlockSpec((tq, g * V_MLA), lambda b, h, i: (b * nq + i, h)),
        out_shape=jax.ShapeDtypeStruct((batch * seq, n_heads * V_MLA), BF16),
        compiler_params=_params("parallel", "parallel", "arbitrary"),
        name="mla_attention",
    )(q, k, vt)


def _even_mixer(hn, w_in, w_out, i, diff_lambda, subln_g, batch, seq, layer):
    d_model = hn.shape[1]
    w_sb = d_model // 2
    n_sb = w_sb // HEAD_DIM
    n_diff = w_sb // (2 * HEAD_DIM)
    cb = w_sb // LANES
    qkv = matmul([(hn, w_in, i, 0)], w_in.shape[2], BF16, tm=2048, tn=512, name="even_in_proj",
                 a_single_buffer=True)
    a = sb_attention(qkv, batch, seq, n_sb, 0, cb, 2 * cb)
    lam_init = 0.8 - 0.6 * math.exp(-0.3 * layer)
    bo = diff_attention(qkv, diff_lambda, subln_g, batch, seq, n_diff,
                        3 * cb // 2, 4 * cb // 2, 5 * cb // 2, lam_init)
    return matmul([(a, w_out, i, 0), (bo, w_out, i, 1)], d_model, F32, tm=2048, tn=512,
                  name="mix_out_proj", a_single_buffer=True)


def _odd_mixer(hn, w_in, w_out, i, rel_bias, q_norm_g, w_uq, kv_norm_g, w_ukv, batch, seq):
    d_model = hn.shape[1]
    w_ch = d_model // 2
    n_ch = w_ch // HEAD_DIM
    n_mla = w_ch // HEAD_DIM
    cb = w_ch // LANES
    q_lora, kv_lora = q_norm_g.shape[0], kv_norm_g.shape[0]
    n_attn = 3 * w_ch
    n_lat = q_lora + kv_lora + QK_ROPE
    lat_pad = -n_lat % LANES
    w_in_t = jnp.swapaxes(w_in, 1, 2)
    w_lat_t = row_slab_bf16(w_in_t, i, n_attn, n_lat + lat_pad)
    qkv = matmul([(hn, w_in_t, i, 0)], n_attn, BF16, tm=2048, tn=512, name="odd_in_proj", w_t=True,
                 a_single_buffer=True)
    lat = matmul([(hn, w_lat_t, 0, 0)], n_lat + lat_pad, F32, tm=512, tn=n_lat + lat_pad,
                 name="odd_lat_proj", w_t=True)
    c = chunk_attention(qkv, rel_bias, batch, seq, n_ch, 0, cb, 2 * cb)

    tables = rope_tables(seq)
    hw = QK_NOPE + LANES
    wq = w_uq.reshape(q_lora, n_mla, QK_NOPE + QK_ROPE)
    wq = jnp.pad(wq, ((0, 0), (0, 0), (0, hw - QK_NOPE - QK_ROPE))).reshape(q_lora, n_mla * hw).astype(BF16)
    q = mla_q(lat, q_norm_g, wq, tables, seq, n_mla)
    assert q_lora % kv_lora == 0 and (q_lora + kv_lora) % LANES == 0
    k, v = mla_kv(lat, kv_norm_g, w_ukv.astype(BF16), tables, seq, n_mla,
                  q_lora // kv_lora, (q_lora + kv_lora) // LANES)
    dm = mla_attention(q, k, v, batch, seq, n_mla)
    return matmul([(c, w_out, i, 0), (dm, w_out, i, 1)], d_model, F32, tm=2048, tn=512,
                  name="mix_out_proj", a_single_buffer=True)


def _ffn(h, w_in, conv_w, conv_b, w_out, layer, seq):
    g, w_out16 = ffn_in(h, w_in, conv_w, conv_b, w_out, layer, seq)
    return matmul([(g, w_out16[None], 0, 0)], w_out.shape[2], F32, tm=512, tn=256,
                  name="ffn_out_proj")


def kernel(x, norm_g, even_w_in, even_w_out, diff_lambda, diff_subln_g, odd_w_in, odd_w_out,
           ch_rel_bias, mla_q_norm_g, mla_w_uq, mla_kv_norm_g, mla_w_ukv, ffn_w_in, ffn_conv_w,
           ffn_conv_b, ffn_w_out):
    batch, seq, d_model = x.shape
    depth = norm_g.shape[0]
    xf = x.reshape(batch * seq, d_model)
    hn = norm_cast(xf, norm_g[0, 0])
    for layer in range(depth):
        g = norm_g[layer]
        i = layer // 2
        if layer % 2 == 0:
            mix = _even_mixer(hn, even_w_in, even_w_out, i, diff_lambda[i], diff_subln_g[i],
                              batch, seq, layer)
        else:
            mix = _odd_mixer(hn, odd_w_in, odd_w_out, i, ch_rel_bias[i], mla_q_norm_g[i],
                             mla_w_uq[i], mla_kv_norm_g[i], mla_w_ukv[i], batch, seq)
        xf, h2 = resid_norm(xf, mix, g[1], g[2])
        f = _ffn(h2, ffn_w_in, ffn_conv_w, ffn_conv_b, ffn_w_out, layer, seq)
        if layer + 1 < depth:
            xf, hn = resid_norm(xf, f, g[3], norm_g[layer + 1, 0])
        else:
            xf = resid(xf, f, g[3])
    return xf.reshape(batch, seq, d_model)
```

```python
import functools
import math

import jax
import jax.numpy as jnp
from jax import lax
from jax.experimental import pallas as pl
from jax.experimental.pallas import tpu as pltpu

F32 = jnp.float32
BF16 = jnp.bfloat16

CHUNK = 64
HEAD_DIM = 128
LEFT_CHUNKS = 8
REL_CLIP = 128
QK_NOPE = 128
QK_ROPE = 64
V_MLA = 128
ROPE_THETA = 10000.0
CONV_W = 3
EPS = 1e-6
NEG = -1e30
LOG2E = math.log2(math.e)

LANES = 128
BF16_SUBLANES = 16
VMEM_LIMIT = 52 * 1024 * 1024

ATTN_TILE = 256
SB_HEADS_PER_STEP = 8
MLA_HEADS_PER_STEP = 8
DIFF_HEADS_PER_STEP = 4
CHUNK_HEADS_PER_STEP = 4
MLA_PROJ_HEADS_PER_STEP = 4


def _params(*sem):
    return pltpu.CompilerParams(dimension_semantics=sem, vmem_limit_bytes=VMEM_LIMIT)


def _rms(x, g):
    ms = jnp.mean(x * x, axis=-1, keepdims=True)
    return x * lax.rsqrt(ms + EPS) * g


def _dot(a, b):
    return jnp.dot(a, b, preferred_element_type=F32)


def _dot_nt(a, b):
    return lax.dot_general(a, b, (((1,), (1,)), ((), ())), preferred_element_type=F32)


def _norm_cast_kernel(x_ref, g_ref, h_ref):
    h_ref[...] = _rms(x_ref[...], g_ref[...]).astype(h_ref.dtype)


def _resid_norm_kernel(x_ref, y_ref, g1_ref, g2_ref, xo_ref, h_ref):
    xn = x_ref[...] + _rms(y_ref[...], g1_ref[...])
    xo_ref[...] = xn
    h_ref[...] = _rms(xn, g2_ref[...]).astype(h_ref.dtype)


def _resid_kernel(x_ref, y_ref, g_ref, xo_ref):
    xo_ref[...] = x_ref[...] + _rms(y_ref[...], g_ref[...])


def _row_spec(tr, d):
    return pl.BlockSpec((tr, d), lambda i: (i, 0))


def _vec_spec(d):
    return pl.BlockSpec((1, d), lambda i: (0, 0))


def norm_cast(x, g, tr=256):
    t, d = x.shape
    return pl.pallas_call(
        _norm_cast_kernel,
        grid=(t // tr,),
        in_specs=[_row_spec(tr, d), _vec_spec(d)],
        out_specs=_row_spec(tr, d),
        out_shape=jax.ShapeDtypeStruct((t, d), BF16),
        compiler_params=_params("parallel"),
        name="norm_cast",
    )(x, g.reshape(1, d))


def resid_norm(x, y, g1, g2, tr=256):
    t, d = x.shape
    return pl.pallas_call(
        _resid_norm_kernel,
        grid=(t // tr,),
        in_specs=[_row_spec(tr, d), _row_spec(tr, d), _vec_spec(d), _vec_spec(d)],
        out_specs=[_row_spec(tr, d), _row_spec(tr, d)],
        out_shape=[jax.ShapeDtypeStruct((t, d), F32), jax.ShapeDtypeStruct((t, d), BF16)],
        compiler_params=_params("parallel"),
        name="resid_norm",
    )(x, y, g1.reshape(1, d), g2.reshape(1, d))


def resid(x, y, g, tr=256):
    t, d = x.shape
    return pl.pallas_call(
        _resid_kernel,
        grid=(t // tr,),
        in_specs=[_row_spec(tr, d), _row_spec(tr, d), _vec_spec(d)],
        out_specs=_row_spec(tr, d),
        out_shape=jax.ShapeDtypeStruct((t, d), F32),
        compiler_params=_params("parallel"),
        name="resid",
    )(x, y, g.reshape(1, d))


def _matmul_kernel(*refs, n_pairs, w_t):
    o_ref = refs[2 * n_pairs]
    dot = _dot_nt if w_t else _dot
    acc = dot(refs[0][...], refs[n_pairs][...].astype(BF16))
    for p in range(1, n_pairs):
        acc = acc + dot(refs[p][...], refs[n_pairs + p][...].astype(BF16))
    o_ref[...] = acc.astype(o_ref.dtype)


def matmul(pairs, n, out_dtype, tm, tn, name, w_t=False, a_single_buffer=False):
    m = pairs[0][0].shape[0]
    n_pairs = len(pairs)
    a_specs, w_specs, args_a, args_w = [], [], [], []
    for a, w, layer, rb in pairs:
        k = a.shape[1]
        mode = dict(pipeline_mode=pl.Buffered(1)) if a_single_buffer else {}
        a_specs.append(pl.BlockSpec((tm, k), lambda i, j: (i, 0), **mode))
        if w_t:
            w_specs.append(pl.BlockSpec((None, tn, k), lambda i, j, layer=layer, rb=rb: (layer, j, rb)))
        else:
            w_specs.append(pl.BlockSpec((None, k, tn), lambda i, j, layer=layer, rb=rb: (layer, rb, j)))
        args_a.append(a)
        args_w.append(w)
    return pl.pallas_call(
        functools.partial(_matmul_kernel, n_pairs=n_pairs, w_t=w_t),
        grid=(m // tm, n // tn),
        in_specs=a_specs + w_specs,
        out_specs=pl.BlockSpec((tm, tn), lambda i, j: (i, j)),
        out_shape=jax.ShapeDtypeStruct((m, n), out_dtype),
        compiler_params=_params("parallel", "arbitrary"),
        name=name,
    )(*args_a, *args_w)


def _row_slab_kernel(w_ref, o_ref, *, row0, n_rows):
    row = row0 + pl.program_id(0) * LANES + lax.broadcasted_iota(jnp.int32, w_ref.shape, 0)
    o_ref[...] = jnp.where(row < n_rows, w_ref[...], 0.0).astype(o_ref.dtype)


def row_slab_bf16(w_t, layer, row0, height):
    _, n_rows, k = w_t.shape
    assert row0 % LANES == 0 and height % LANES == 0
    return pl.pallas_call(
        functools.partial(_row_slab_kernel, row0=row0, n_rows=n_rows),
        grid=(height // LANES,),
        in_specs=[pl.BlockSpec((None, LANES, k), lambda j: (layer, row0 // LANES + j, 0))],
        out_specs=pl.BlockSpec((None, LANES, k), lambda j: (0, j, 0)),
        out_shape=jax.ShapeDtypeStruct((1, height, k), BF16),
        compiler_params=_params("parallel"),
        name="row_slab_bf16",
    )(w_t)


def _gelu_tanh(x):
    c = math.sqrt(2.0 / math.pi)
    return x * (0.5 * (1.0 + jnp.tanh(c * (x + 0.044715 * (x * x * x)))))


def _ffn_in_kernel(a_ref, ah_ref, wg_ref, wv_ref, cwg_ref, cwv_ref, cbg_ref, cbv_ref, wo_ref,
                   o_ref, wo16_ref, *, tm, seq):
    i = pl.program_id(0)
    wo16_ref[...] = wo_ref[...].astype(wo16_ref.dtype)
    tn = o_ref.shape[1]
    a = a_ref[...]
    ah = ah_ref[...]
    seq_start = (i * tm) % seq == 0
    row8 = lax.broadcasted_iota(jnp.int32, (8, tn), 0)

    def conv(w_ref, cw_ref, cb_ref):
        w = w_ref[...].astype(BF16)
        u = _dot(a, w)
        uh = _dot(ah, w)
        uh = jnp.where(seq_start, 0.0, uh)
        p1 = uh[BF16_SUBLANES - 1:BF16_SUBLANES, :]
        p2 = uh[BF16_SUBLANES - 2:BF16_SUBLANES - 1, :]
        r1 = pltpu.roll(u, 1, 0)
        r2 = pltpu.roll(u, 2, 0)
        h1 = jnp.where(row8 == 0, p1, r1[:8, :])
        h2 = jnp.where(row8 == 0, p2, jnp.where(row8 == 1, p1, r2[:8, :]))
        u1 = jnp.concatenate([h1, r1[8:, :]], axis=0)
        u2 = jnp.concatenate([h2, r2[8:, :]], axis=0)
        cw = cw_ref[...]
        return cb_ref[...] + (cw[0:1, :] * u2 + cw[1:2, :] * u1 + cw[2:3, :] * u)

    gate = conv(wg_ref, cwg_ref, cbg_ref)
    val = conv(wv_ref, cwv_ref, cbv_ref)
    o_ref[...] = (_gelu_tanh(gate) * val).astype(o_ref.dtype)


def ffn_in(h, w_in, conv_w, conv_b, w_out, layer, seq, tm=2048, tn=256):
    t, k = h.shape
    f = w_in.shape[2] // 2
    d_out = w_out.shape[2]
    nf = f // tn
    n_steps = (t // tm) * nf
    assert f % (n_steps * BF16_SUBLANES) == 0
    slab = f // n_steps
    halo = BF16_SUBLANES
    hb = tm // halo
    conv_b = conv_b.reshape(conv_b.shape[0], 1, 2 * f)
    return pl.pallas_call(
        functools.partial(_ffn_in_kernel, tm=tm, seq=seq),
        grid=(t // tm, nf),
        in_specs=[
            pl.BlockSpec((tm, k), lambda i, j: (i, 0), pipeline_mode=pl.Buffered(1)),
            pl.BlockSpec((halo, k), lambda i, j: (jnp.maximum(i * hb - 1, 0), 0)),
            pl.BlockSpec((None, k, tn), lambda i, j: (layer, 0, j)),
            pl.BlockSpec((None, k, tn), lambda i, j: (layer, 0, j + nf)),
            pl.BlockSpec((None, CONV_W, tn), lambda i, j: (layer, 0, j)),
            pl.BlockSpec((None, CONV_W, tn), lambda i, j: (layer, 0, j + nf)),
            pl.BlockSpec((None, 1, tn), lambda i, j: (layer, 0, j)),
            pl.BlockSpec((None, 1, tn), lambda i, j: (layer, 0, j + nf)),
            pl.BlockSpec((None, slab, d_out), lambda i, j: (layer, i * nf + j, 0)),
        ],
        out_specs=[pl.BlockSpec((tm, tn), lambda i, j: (i, j)),
                   pl.BlockSpec((slab, d_out), lambda i, j: (i * nf + j, 0))],
        out_shape=[jax.ShapeDtypeStruct((t, f), BF16),
                   jax.ShapeDtypeStruct((f, d_out), BF16)],
        compiler_params=_params("parallel", "arbitrary"),
        name="ffn_in",
    )(h, h, w_in, w_in, conv_w, conv_w, conv_b, conv_b, w_out)


def _transpose_value_tiles(v_ref, vt_scr, tq):
    def transpose_tile(kb, c):
        start = pl.multiple_of(kb * tq, tq)
        vt_scr[kb] = v_ref[pl.ds(start, tq), :].astype(F32).T.astype(vt_scr.dtype)
        return c
    lax.fori_loop(0, vt_scr.shape[0], transpose_tile, 0)


def _sb_kernel(q_ref, k_ref, v_ref, o_ref, vt_scr, *, tq, scale, g):
    qi = pl.program_id(2)
    d = HEAD_DIM
    heads = [slice(hh * d, (hh + 1) * d) for hh in range(g)]
    qs = [q_ref[:, cs] for cs in heads]

    @pl.when(qi == 0)
    def _():
        _transpose_value_tiles(v_ref, vt_scr, tq)

    key = lax.broadcasted_iota(jnp.int32, (tq, tq), 0)
    qry = lax.broadcasted_iota(jnp.int32, (tq, tq), 1)
    strict = key < qry
    r2 = lax.broadcasted_iota(jnp.int32, (tq, 2 * tq), 0)
    c2 = lax.broadcasted_iota(jnp.int32, (tq, 2 * tq), 1)
    tri = (jnp.where(c2 >= tq, c2 - tq, c2) >= r2).astype(BF16)

    def block(kb, carry, diag):
        start = pl.multiple_of(kb * tq, tq)
        zs = [_dot_nt(k_ref[pl.ds(start, tq), cs], q) * (scale * LOG2E) for q, cs in zip(qs, heads)]
        csums = []
        for z in zs:
            neg_abs = lax.bitcast_convert_type(
                lax.bitcast_convert_type(z, jnp.uint32) | jnp.uint32(0x80000000), F32)
            sp = jnp.maximum(z, 0.0) + jnp.log2(1.0 + jnp.exp2(neg_abs))
            if diag:
                sp = jnp.where(strict, sp, 0.0)
            hi = sp.astype(BF16)
            lo = (sp - hi.astype(F32)).astype(BF16)
            csums.append(_dot(tri, jnp.concatenate([hi, lo], axis=0)))
        out = []
        for z, csum, cs, (rsum, acc) in zip(zs, csums, heads, carry):
            w = jnp.exp2(z - csum - rsum)
            if diag:
                w = jnp.where(strict, w, 0.0)
            acc = acc + _dot(vt_scr[kb, cs, :], w.astype(BF16))
            rsum = rsum + csum[0:1, :]
            out.append((rsum, acc))
        return tuple(out)

    init = tuple((jnp.zeros((1, tq), F32), jnp.zeros((d, tq), F32)) for _ in range(g))
    carry = block(qi, init, True)

    def body(it, carry):
        return block(qi - 1 - it, carry, False)

    carry = lax.fori_loop(0, qi, body, carry)
    for cs, (_, acc) in zip(heads, carry):
        o_ref[:, cs] = acc.T.astype(o_ref.dtype)


def sb_attention(qkv, batch, seq, n_heads, q_col, k_col, v_col, g=SB_HEADS_PER_STEP):
    tq = ATTN_TILE
    nq = seq // tq
    d = HEAD_DIM
    assert n_heads % g == 0 and q_col % g == 0 and k_col % g == 0 and v_col % g == 0
    return pl.pallas_call(
        functools.partial(_sb_kernel, tq=tq, scale=1.0 / math.sqrt(d), g=g),
        grid=(batch, n_heads // g, nq),
        in_specs=[
            pl.BlockSpec((tq, g * d), lambda b, h, i: (b * nq + i, q_col // g + h)),
            pl.BlockSpec((seq, g * d), lambda b, h, i: (b, k_col // g + h)),
            pl.BlockSpec((seq, g * d), lambda b, h, i: (b, v_col // g + h)),
        ],
        out_specs=pl.BlockSpec((tq, g * d), lambda b, h, i: (b * nq + i, h)),
        out_shape=jax.ShapeDtypeStruct((batch * seq, n_heads * d), BF16),
        scratch_shapes=[pltpu.VMEM((nq, g * d, tq), BF16)],
        compiler_params=_params("parallel", "parallel", "arbitrary"),
        name="sb_attention",
    )(qkv, qkv, qkv)


def _online_step_t(s, vt, m, l, acc):
    m_new = jnp.maximum(m, jnp.max(s, axis=0, keepdims=True))
    alpha = jnp.exp2(m - m_new)
    p = jnp.exp2(s - m_new)
    l = alpha * l + jnp.sum(p, axis=0, keepdims=True)
    acc = alpha * acc + _dot(vt, p.astype(BF16))
    return m_new, l, acc


def _chunk_causal(tq):
    row = lax.broadcasted_iota(jnp.int32, (tq, tq), 0)
    col = lax.broadcasted_iota(jnp.int32, (tq, tq), 1)
    return row, col, (col // CHUNK) <= (row // CHUNK)


def _diff_kernel(q_ref, k_ref, v_ref, lam_ref, g_ref, o_ref, vt_scr, *, tq, scale, lam_init, g):
    hg = pl.program_id(1)
    qi = pl.program_id(2)
    d = HEAD_DIM
    dv = 2 * HEAD_DIM
    qk_cols = [slice(c * d, (c + 1) * d) for c in range(2 * g)]
    v_cols = [slice(hh * dv, (hh + 1) * dv) for hh in range(g)]
    qs = [q_ref[:, cs] for cs in qk_cols]

    @pl.when(qi == 0)
    def _():
        _transpose_value_tiles(v_ref, vt_scr, tq)

    key, qry, _ = _chunk_causal(tq)
    visible = (key // CHUNK) <= (qry // CHUNK)
    dist = (qry - key).astype(F32)
    nbias_diag, nbias_off, slopes = [], [], []
    for hh in range(g):
        slope = lax.bitcast_convert_type(
            jnp.full((1, 1), (126 - (hg * g + hh)) * (1 << 23), jnp.int32), F32)
        slope = slope * LOG2E
        slopes.append(slope)
        nbias_diag.append(-slope * jnp.abs(dist))
        nbias_off.append(-slope * dist)

    def block(kb, carry, diag):
        start = pl.multiple_of(kb * tq, tq)
        ss = [_dot_nt(k_ref[pl.ds(start, tq), cs], q) * (scale * LOG2E) for q, cs in zip(qs, qk_cols)]
        for c in range(2 * g):
            hh = c // 2
            if diag:
                ss[c] = jnp.where(visible, ss[c] + nbias_diag[hh], NEG)
            else:
                ss[c] = ss[c] + (nbias_off[hh] - slopes[hh] * ((qi - kb) * tq).astype(F32))
        return tuple(_online_step_t(ss[c], vt_scr[kb, v_cols[c // 2], :], *carry[c])
                     for c in range(2 * g))

    def body(kb, carry):
        return block(kb, carry, False)

    init = tuple((jnp.full((1, tq), NEG, F32), jnp.zeros((1, tq), F32), jnp.zeros((dv, tq), F32))
                 for _ in range(2 * g))
    carry = lax.fori_loop(0, qi, body, init)
    carry = block(qi, carry, True)

    lf = lam_ref[...]
    lam = (jnp.exp(jnp.sum(lf[0:1, :] * lf[1:2, :], axis=-1, keepdims=True))
           - jnp.exp(jnp.sum(lf[2:3, :] * lf[3:4, :], axis=-1, keepdims=True)) + lam_init)
    for hh in range(g):
        (_, l1, a1), (_, l2, a2) = carry[2 * hh], carry[2 * hh + 1]
        o = (a1 / l1 - lam * (a2 / l2)).T
        o_ref[:, v_cols[hh]] = (_rms(o, g_ref[...]) * (1.0 - lam_init)).astype(o_ref.dtype)


def diff_attention(qkv, diff_lambda, subln_g, batch, seq, n_heads, q_col, k_col, v_col, lam_init,
                   g=DIFF_HEADS_PER_STEP):
    tq = ATTN_TILE
    nq = seq // tq
    dv = 2 * HEAD_DIM
    assert n_heads == 8, "ALiBi slopes are built as exact powers of two"
    assert n_heads % g == 0 and q_col % g == 0 and k_col % g == 0 and v_col % g == 0
    return pl.pallas_call(
        functools.partial(_diff_kernel, tq=tq, scale=1.0 / math.sqrt(HEAD_DIM), lam_init=lam_init, g=g),
        grid=(batch, n_heads // g, nq),
        in_specs=[
            pl.BlockSpec((tq, g * dv), lambda b, h, i: (b * nq + i, q_col // g + h)),
            pl.BlockSpec((seq, g * dv), lambda b, h, i: (b, k_col // g + h)),
            pl.BlockSpec((seq, g * dv), lambda b, h, i: (b, v_col // g + h)),
            pl.BlockSpec((4, HEAD_DIM), lambda b, h, i: (0, 0)),
            pl.BlockSpec((1, dv), lambda b, h, i: (0, 0)),
        ],
        out_specs=pl.BlockSpec((tq, g * dv), lambda b, h, i: (b * nq + i, h)),
        out_shape=jax.ShapeDtypeStruct((batch * seq, n_heads * dv), BF16),
        scratch_shapes=[pltpu.VMEM((nq, g * dv, tq), BF16)],
        compiler_params=_params("parallel", "parallel", "arbitrary"),
        name="diff_attention",
    )(qkv, qkv, qkv, diff_lambda, subln_g.reshape(1, dv))


def _chunk_kernel(q_ref, k_ref, v_ref, tab_ref, o_ref, *, tq, scale, n_win, g):
    qi = pl.program_id(2)
    d = HEAD_DIM
    heads = [slice(hh * d, (hh + 1) * d) for hh in range(g)]
    kbs = [qi - (n_win - 1) + w for w in range(n_win)]
    starts = [pl.multiple_of(jnp.maximum(kb, 0) * tq, tq) for kb in kbs]
    scores = []
    for hh, cs in enumerate(heads):
        q = q_ref[:, cs]
        s_blocks = []
        for w in range(n_win):
            s = (_dot_nt(q, k_ref[pl.ds(starts[w], tq), cs]) * (scale * LOG2E)
                 + tab_ref[hh, :, w * tq:(w + 1) * tq])
            if w < n_win - 1:
                s = jnp.where(kbs[w] >= 0, s, NEG)
            s_blocks.append(s)
        scores.append(s_blocks)
    for cs, s_blocks in zip(heads, scores):
        m = s_blocks[0].max(axis=-1, keepdims=True)
        for s in s_blocks[1:]:
            m = jnp.maximum(m, s.max(axis=-1, keepdims=True))
        l = jnp.zeros_like(m)
        acc = jnp.zeros((tq, d), F32)
        for w, s in enumerate(s_blocks):
            p = jnp.exp2(s - m)
            l = l + jnp.sum(p, axis=-1, keepdims=True)
            acc = acc + _dot(p.astype(BF16), v_ref[pl.ds(starts[w], tq), cs])
        o_ref[:, cs] = (acc / l).astype(o_ref.dtype)


def chunk_bias_table(rel_bias, tq):
    left = LEFT_CHUNKS * CHUNK
    n_heads = rel_bias.shape[0]
    width = tq + left
    period = 2 * left
    assert left >= REL_CLIP and tq <= left
    rb = rel_bias.astype(F32)
    hi = jnp.broadcast_to(rb[:, -1:], (n_heads, left - REL_CLIP))
    mid = rb[:, ::-1]
    lo = jnp.broadcast_to(rb[:, :1], (n_heads, tq - REL_CLIP - 1))
    neg_d = jnp.broadcast_to(rb[:, -1:], (n_heads, period - width))
    diag = jnp.concatenate([hi, mid, lo, neg_d], axis=1)
    flat = jnp.tile(diag, (1, tq))[:, :tq * (period - 1)]
    bias = flat.reshape(n_heads, tq, period - 1)[:, :, :width]
    i = jnp.arange(tq)[:, None]
    j = jnp.arange(width)[None, :]
    qc, kc = i // CHUNK, j // CHUNK
    allowed = (kc >= qc) & (kc <= qc + LEFT_CHUNKS)
    return jnp.where(allowed[None], bias * LOG2E, NEG)


def chunk_attention(qkv, rel_bias, batch, seq, n_heads, q_col, k_col, v_col, g=CHUNK_HEADS_PER_STEP):
    tq = ATTN_TILE
    nq = seq // tq
    d = HEAD_DIM
    left = LEFT_CHUNKS * CHUNK
    assert left % tq == 0
    assert n_heads % g == 0 and q_col % g == 0 and k_col % g == 0 and v_col % g == 0
    n_win = left // tq + 1
    table = chunk_bias_table(rel_bias, tq)
    return pl.pallas_call(
        functools.partial(_chunk_kernel, tq=tq, scale=1.0 / math.sqrt(d), n_win=n_win, g=g),
        grid=(n_heads // g, batch, nq),
        in_specs=[
            pl.BlockSpec((tq, g * d), lambda h, b, i: (b * nq + i, q_col // g + h)),
            pl.BlockSpec((seq, g * d), lambda h, b, i: (b, k_col // g + h)),
            pl.BlockSpec((seq, g * d), lambda h, b, i: (b, v_col // g + h)),
            pl.BlockSpec((g, tq, tq + left), lambda h, b, i: (h, 0, 0)),
        ],
        out_specs=pl.BlockSpec((tq, g * d), lambda h, b, i: (b * nq + i, h)),
        out_shape=jax.ShapeDtypeStruct((batch * seq, n_heads * d), BF16),
        compiler_params=_params("parallel", "parallel", "arbitrary"),
        name="chunk_attention",
    )(qkv, qkv, qkv, table)


def rope_tables(seq):
    half = QK_ROPE // 2
    pos = jnp.arange(seq, dtype=F32)
    inv_freq = ROPE_THETA ** (-jnp.arange(0, QK_ROPE, 2, dtype=F32) / QK_ROPE)
    ang = pos[:, None] * inv_freq[None, :]
    cos, sin = jnp.cos(ang), jnp.sin(ang)
    z = lambda n: jnp.zeros((seq, n), F32)
    cos_t = jnp.concatenate([cos, cos, z(LANES - 2 * half)], axis=1)
    sin_a = jnp.concatenate([-sin, z(LANES - half)], axis=1)
    sin_b = jnp.concatenate([z(half), sin, z(LANES - 2 * half)], axis=1)
    return cos_t, sin_a, sin_b


def _rope(x, cos_t, sin_a, sin_b):
    half = QK_ROPE // 2
    return x * cos_t + pltpu.roll(x, LANES - half, 1) * sin_a + pltpu.roll(x, half, 1) * sin_b


def _mla_q_kernel(c_ref, g_ref, w_ref, cos_ref, sa_ref, sb_ref, o_ref, an_ref, *, hp):
    @pl.when(pl.program_id(1) == 0)
    def _():
        an_ref[...] = _rms(c_ref[...], g_ref[...]).astype(an_ref.dtype)

    hw = QK_NOPE + LANES
    res = _dot(an_ref[...], w_ref[...])
    cos_t, sin_a, sin_b = cos_ref[...], sa_ref[...], sb_ref[...]
    for hh in range(hp):
        c0 = hh * hw
        o_ref[:, c0:c0 + QK_NOPE] = res[:, c0:c0 + QK_NOPE].astype(o_ref.dtype)
        o_ref[:, c0 + QK_NOPE:c0 + hw] = _rope(res[:, c0 + QK_NOPE:c0 + hw],
                                               cos_t, sin_a, sin_b).astype(o_ref.dtype)


def mla_q(lat, g, w_q, tables, seq, n_heads, tm=512, hp=MLA_PROJ_HEADS_PER_STEP):
    t = lat.shape[0]
    kq = g.shape[0]
    hw = QK_NOPE + LANES
    npos = seq // tm
    tab_spec = pl.BlockSpec((tm, LANES), lambda i, j: (i % npos, 0))
    return pl.pallas_call(
        functools.partial(_mla_q_kernel, hp=hp),
        grid=(t // tm, n_heads // hp),
        in_specs=[
            pl.BlockSpec((tm, kq), lambda i, j: (i, 0)),
            pl.BlockSpec((1, kq), lambda i, j: (0, 0)),
            pl.BlockSpec((kq, hp * hw), lambda i, j: (0, j)),
            tab_spec, tab_spec, tab_spec,
        ],
        out_specs=pl.BlockSpec((tm, hp * hw), lambda i, j: (i, j)),
        out_shape=jax.ShapeDtypeStruct((t, n_heads * hw), BF16),
        scratch_shapes=[pltpu.VMEM((tm, kq), BF16)],
        compiler_params=_params("parallel", "arbitrary"),
        name="mla_q",
    )(lat, g.reshape(1, kq), w_q, *tables)


def _mla_kv_kernel(c_ref, kr_ref, g_ref, w_ref, cos_ref, sa_ref, sb_ref, k_ref, vt_ref,
                   an_ref, kr_scr, *, hp, tm, tk):
    @pl.when(pl.program_id(1) == 0)
    def _():
        an_ref[...] = _rms(c_ref[...], g_ref[...]).astype(an_ref.dtype)
        kr_scr[...] = _rope(kr_ref[...], cos_ref[...], sa_ref[...], sb_ref[...]).astype(kr_scr.dtype)

    hw = QK_NOPE + LANES
    wv = QK_NOPE + V_MLA
    res = _dot(an_ref[...], w_ref[...])
    kr = kr_scr[...]
    for hh in range(hp):
        k_ref[:, hh * hw:hh * hw + QK_NOPE] = res[:, hh * wv:hh * wv + QK_NOPE].astype(k_ref.dtype)
        k_ref[:, hh * hw + QK_NOPE:(hh + 1) * hw] = kr
        v = res[:, hh * wv + QK_NOPE:(hh + 1) * wv]
        for kt in range(tm // tk):
            vt_ref[kt, hh * V_MLA:(hh + 1) * V_MLA, :] = v[kt * tk:(kt + 1) * tk, :].T.astype(vt_ref.dtype)


def mla_kv(lat, g, w_kv, tables, seq, n_heads, ckv_col, kr_col, tm=512, hp=MLA_PROJ_HEADS_PER_STEP):
    t = lat.shape[0]
    kkv = g.shape[0]
    hw = QK_NOPE + LANES
    tk = ATTN_TILE
    assert tm % tk == 0
    npos = seq // tm
    tab_spec = pl.BlockSpec((tm, LANES), lambda i, j: (i % npos, 0))
    return pl.pallas_call(
        functools.partial(_mla_kv_kernel, hp=hp, tm=tm, tk=tk),
        grid=(t // tm, n_heads // hp),
        in_specs=[
            pl.BlockSpec((tm, kkv), lambda i, j: (i, ckv_col)),
            pl.BlockSpec((tm, LANES), lambda i, j: (i, kr_col)),
            pl.BlockSpec((1, kkv), lambda i, j: (0, 0)),
            pl.BlockSpec((kkv, hp * (QK_NOPE + V_MLA)), lambda i, j: (0, j)),
            tab_spec, tab_spec, tab_spec,
        ],
        out_specs=[pl.BlockSpec((tm, hp * hw), lambda i, j: (i, j)),
                   pl.BlockSpec((tm // tk, hp * V_MLA, tk), lambda i, j: (i, j, 0))],
        out_shape=[jax.ShapeDtypeStruct((t, n_heads * hw), BF16),
                   jax.ShapeDtypeStruct((t // tk, n_heads * V_MLA, tk), BF16)],
        scratch_shapes=[pltpu.VMEM((tm, kkv), BF16), pltpu.VMEM((tm, LANES), BF16)],
        compiler_params=_params("parallel", "arbitrary"),
        name="mla_kv",
    )(lat, lat, g.reshape(1, kkv), w_kv, *tables)


def _mla_attn_kernel(q_ref, k_ref, vt_ref, o_ref, *, tq, scale, g):
    qi = pl.program_id(2)
    hw = QK_NOPE + LANES
    dv = V_MLA
    qk_cols = [slice(hh * hw, (hh + 1) * hw) for hh in range(g)]
    v_cols = [slice(hh * dv, (hh + 1) * dv) for hh in range(g)]
    qs = [q_ref[:, cs] for cs in qk_cols]
    key, qry, _ = _chunk_causal(tq)
    visible = (key // CHUNK) <= (qry // CHUNK)

    def block(kb, carry, diag):
        start = pl.multiple_of(kb * tq, tq)
        ss = [_dot_nt(k_ref[pl.ds(start, tq), cs], q) * (scale * LOG2E) for q, cs in zip(qs, qk_cols)]
        if diag:
            ss = [jnp.where(visible, s, NEG) for s in ss]
        return tuple(_online_step_t(s, vt_ref[kb, vs, :], *st)
                     for s, vs, st in zip(ss, v_cols, carry))

    def body(kb, carry):
        return block(kb, carry, False)

    init = tuple((jnp.full((1, tq), NEG, F32), jnp.zeros((1, tq), F32), jnp.zeros((dv, tq), F32))
                 for _ in range(g))
    carry = lax.fori_loop(0, qi, body, init)
    carry = block(qi, carry, True)
    for vs, (_, l, acc) in zip(v_cols, carry):
        o_ref[:, vs] = (acc / l).T.astype(o_ref.dtype)


def mla_attention(q, k, vt, batch, seq, n_heads, g=MLA_HEADS_PER_STEP):
    tq = ATTN_TILE
    nq = seq // tq
    hw = QK_NOPE + LANES
    assert n_heads % g == 0
    return pl.pallas_call(
        functools.partial(_mla_attn_kernel, tq=tq, scale=1.0 / math.sqrt(QK_NOPE + QK_ROPE), g=g),
        grid=(batch, n_heads // g, nq),
        in_specs=[
            pl.BlockSpec((tq, g * hw), lambda b, h, i: (b * nq + i, h)),
            pl.BlockSpec((seq, g * hw), lambda b, h, i: (b, h), pipeline_mode=pl.Buffered(1)),
            pl.BlockSpec((nq, g * V_MLA, tq), lambda b, h, i: (b, h, 0), pipeline_mode=pl.Buffered(1)),
        ],
        out_specs=pl.BlockSpec((tq, g * V_MLA), lambda b, h, i: (b * nq + i, h)),
        out_shape=jax.ShapeDtypeStruct((batch * seq, n_heads * V_MLA), BF16),
        compiler_params=_params("parallel", "parallel", "arbitrary"),
        name="mla_attention",
    )(q, k, vt)


def _even_mixer(hn, w_in, w_out, i, diff_lambda, subln_g, batch, seq, layer):
    d_model = hn.shape[1]
    w_sb = d_model // 2
    n_sb = w_sb // HEAD_DIM
    n_diff = w_sb // (2 * HEAD_DIM)
    cb = w_sb // LANES
    qkv = matmul([(hn, w_in, i, 0)], w_in.shape[2], BF16, tm=2048, tn=512, name="even_in_proj",
                 a_single_buffer=True)
    a = sb_attention(qkv, batch, seq, n_sb, 0, cb, 2 * cb)
    lam_init = 0.8 - 0.6 * math.exp(-0.3 * layer)
    bo = diff_attention(qkv, diff_lambda, subln_g, batch, seq, n_diff,
                        3 * cb // 2, 4 * cb // 2, 5 * cb // 2, lam_init)
    return matmul([(a, w_out, i, 0), (bo, w_out, i, 1)], d_model, F32, tm=1024, tn=512,
                  name="mix_out_proj")


def _odd_mixer(hn, w_in, w_out, i, rel_bias, q_norm_g, w_uq, kv_norm_g, w_ukv, batch, seq):
    d_model = hn.shape[1]
    w_ch = d_model // 2
    n_ch = w_ch // HEAD_DIM
    n_mla = w_ch // HEAD_DIM
    cb = w_ch // LANES
    q_lora, kv_lora = q_norm_g.shape[0], kv_norm_g.shape[0]
    n_attn = 3 * w_ch
    n_lat = q_lora + kv_lora + QK_ROPE
    lat_pad = -n_lat % LANES
    w_in_t = jnp.swapaxes(w_in, 1, 2)
    w_lat_t = row_slab_bf16(w_in_t, i, n_attn, n_lat + lat_pad)
    qkv = matmul([(hn, w_in_t, i, 0)], n_attn, BF16, tm=2048, tn=512, name="odd_in_proj", w_t=True,
                 a_single_buffer=True)
    lat = matmul([(hn, w_lat_t, 0, 0)], n_lat + lat_pad, F32, tm=512, tn=n_lat + lat_pad,
                 name="odd_lat_proj", w_t=True)
    c = chunk_attention(qkv, rel_bias, batch, seq, n_ch, 0, cb, 2 * cb)

    tables = rope_tables(seq)
    hw = QK_NOPE + LANES
    wq = w_uq.reshape(q_lora, n_mla, QK_NOPE + QK_ROPE)
    wq = jnp.pad(wq, ((0, 0), (0, 0), (0, hw - QK_NOPE - QK_ROPE))).reshape(q_lora, n_mla * hw).astype(BF16)
    q = mla_q(lat, q_norm_g, wq, tables, seq, n_mla)
    assert q_lora % kv_lora == 0 and (q_lora + kv_lora) % LANES == 0
    k, v = mla_kv(lat, kv_norm_g, w_ukv.astype(BF16), tables, seq, n_mla,
                  q_lora // kv_lora, (q_lora + kv_lora) // LANES)
    dm = mla_attention(q, k, v, batch, seq, n_mla)
    return matmul([(c, w_out, i, 0), (dm, w_out, i, 1)], d_model, F32, tm=1024, tn=512,
                  name="mix_out_proj")


def _ffn(h, w_in, conv_w, conv_b, w_out, layer, seq):
    g, w_out16 = ffn_in(h, w_in, conv_w, conv_b, w_out, layer, seq)
    return matmul([(g, w_out16[None], 0, 0)], w_out.shape[2], F32, tm=512, tn=256,
                  name="ffn_out_proj")


def kernel(x, norm_g, even_w_in, even_w_out, diff_lambda, diff_subln_g, odd_w_in, odd_w_out,
           ch_rel_bias, mla_q_norm_g, mla_w_uq, mla_kv_norm_g, mla_w_ukv, ffn_w_in, ffn_conv_w,
           ffn_conv_b, ffn_w_out):
    batch, seq, d_model = x.shape
    depth = norm_g.shape[0]
    xf = x.reshape(batch * seq, d_model)
    hn = norm_cast(xf, norm_g[0, 0])
    for layer in range(depth):
        g = norm_g[layer]
        i = layer // 2
        if layer % 2 == 0:
            mix = _even_mixer(hn, even_w_in, even_w_out, i, diff_lambda[i], diff_subln_g[i],
                              batch, seq, layer)
        else:
            mix = _odd_mixer(hn, odd_w_in, odd_w_out, i, ch_rel_bias[i], mla_q_norm_g[i],
                             mla_w_uq[i], mla_kv_norm_g[i], mla_w_ukv[i], batch, seq)
        xf, h2 = resid_norm(xf, mix, g[1], g[2])
        f = _ffn(h2, ffn_w_in, ffn_conv_w, ffn_conv_b, ffn_w_out, layer, seq)
        if layer + 1 < depth:
            xf, hn = resid_norm(xf, f, g[3], norm_g[layer + 1, 0])
        else:
            xf = resid(xf, f, g[3])
    return xf.reshape(batch, seq, d_model)
```

```python
import functools
import math

import jax
import jax.numpy as jnp
from jax import lax
from jax.experimental import pallas as pl
from jax.experimental.pallas import tpu as pltpu

F32 = jnp.float32
BF16 = jnp.bfloat16

CHUNK = 64
HEAD_DIM = 128
LEFT_CHUNKS = 8
REL_CLIP = 128
QK_NOPE = 128
QK_ROPE = 64
V_MLA = 128
ROPE_THETA = 10000.0
CONV_W = 3
EPS = 1e-6
NEG = -1e30
LOG2E = math.log2(math.e)

LANES = 128
BF16_SUBLANES = 16
VMEM_LIMIT = 52 * 1024 * 1024

ATTN_TILE = 256
SB_HEADS_PER_STEP = 8
MLA_HEADS_PER_STEP = 8
DIFF_HEADS_PER_STEP = 2
CHUNK_HEADS_PER_STEP = 4
MLA_PROJ_HEADS_PER_STEP = 4


def _params(*sem):
    return pltpu.CompilerParams(dimension_semantics=sem, vmem_limit_bytes=VMEM_LIMIT)


def _rms(x, g):
    ms = jnp.mean(x * x, axis=-1, keepdims=True)
    return x * lax.rsqrt(ms + EPS) * g


def _dot(a, b):
    return jnp.dot(a, b, preferred_element_type=F32)


def _dot_nt(a, b):
    return lax.dot_general(a, b, (((1,), (1,)), ((), ())), preferred_element_type=F32)


def _norm_cast_kernel(x_ref, g_ref, h_ref):
    h_ref[...] = _rms(x_ref[...], g_ref[...]).astype(h_ref.dtype)


def _resid_norm_kernel(x_ref, y_ref, g1_ref, g2_ref, xo_ref, h_ref):
    xn = x_ref[...] + _rms(y_ref[...], g1_ref[...])
    xo_ref[...] = xn
    h_ref[...] = _rms(xn, g2_ref[...]).astype(h_ref.dtype)


def _resid_kernel(x_ref, y_ref, g_ref, xo_ref):
    xo_ref[...] = x_ref[...] + _rms(y_ref[...], g_ref[...])


def _row_spec(tr, d):
    return pl.BlockSpec((tr, d), lambda i: (i, 0))


def _vec_spec(d):
    return pl.BlockSpec((1, d), lambda i: (0, 0))


def norm_cast(x, g, tr=256):
    t, d = x.shape
    return pl.pallas_call(
        _norm_cast_kernel,
        grid=(t // tr,),
        in_specs=[_row_spec(tr, d), _vec_spec(d)],
        out_specs=_row_spec(tr, d),
        out_shape=jax.ShapeDtypeStruct((t, d), BF16),
        compiler_params=_params("parallel"),
        name="norm_cast",
    )(x, g.reshape(1, d))


def resid_norm(x, y, g1, g2, tr=256):
    t, d = x.shape
    return pl.pallas_call(
        _resid_norm_kernel,
        grid=(t // tr,),
        in_specs=[_row_spec(tr, d), _row_spec(tr, d), _vec_spec(d), _vec_spec(d)],
        out_specs=[_row_spec(tr, d), _row_spec(tr, d)],
        out_shape=[jax.ShapeDtypeStruct((t, d), F32), jax.ShapeDtypeStruct((t, d), BF16)],
        compiler_params=_params("parallel"),
        name="resid_norm",
    )(x, y, g1.reshape(1, d), g2.reshape(1, d))


def resid(x, y, g, tr=256):
    t, d = x.shape
    return pl.pallas_call(
        _resid_kernel,
        grid=(t // tr,),
        in_specs=[_row_spec(tr, d), _row_spec(tr, d), _vec_spec(d)],
        out_specs=_row_spec(tr, d),
        out_shape=jax.ShapeDtypeStruct((t, d), F32),
        compiler_params=_params("parallel"),
        name="resid",
    )(x, y, g.reshape(1, d))


def _matmul_kernel(*refs, n_pairs, w_t):
    o_ref = refs[2 * n_pairs]
    dot = _dot_nt if w_t else _dot
    acc = dot(refs[0][...], refs[n_pairs][...].astype(BF16))
    for p in range(1, n_pairs):
        acc = acc + dot(refs[p][...], refs[n_pairs + p][...].astype(BF16))
    o_ref[...] = acc.astype(o_ref.dtype)


def matmul(pairs, n, out_dtype, tm, tn, name, w_t=False, a_single_buffer=False):
    m = pairs[0][0].shape[0]
    n_pairs = len(pairs)
    a_specs, w_specs, args_a, args_w = [], [], [], []
    for a, w, layer, rb in pairs:
        k = a.shape[1]
        mode = dict(pipeline_mode=pl.Buffered(1)) if a_single_buffer else {}
        a_specs.append(pl.BlockSpec((tm, k), lambda i, j: (i, 0), **mode))
        if w_t:
            w_specs.append(pl.BlockSpec((None, tn, k), lambda i, j, layer=layer, rb=rb: (layer, j, rb)))
        else:
            w_specs.append(pl.BlockSpec((None, k, tn), lambda i, j, layer=layer, rb=rb: (layer, rb, j)))
        args_a.append(a)
        args_w.append(w)
    return pl.pallas_call(
        functools.partial(_matmul_kernel, n_pairs=n_pairs, w_t=w_t),
        grid=(m // tm, n // tn),
        in_specs=a_specs + w_specs,
        out_specs=pl.BlockSpec((tm, tn), lambda i, j: (i, j)),
        out_shape=jax.ShapeDtypeStruct((m, n), out_dtype),
        compiler_params=_params("parallel", "arbitrary"),
        name=name,
    )(*args_a, *args_w)


def _row_slab_kernel(w_ref, o_ref, *, row0, n_rows):
    row = row0 + pl.program_id(0) * LANES + lax.broadcasted_iota(jnp.int32, w_ref.shape, 0)
    o_ref[...] = jnp.where(row < n_rows, w_ref[...], 0.0).astype(o_ref.dtype)


def row_slab_bf16(w_t, layer, row0, height):
    _, n_rows, k = w_t.shape
    assert row0 % LANES == 0 and height % LANES == 0
    return pl.pallas_call(
        functools.partial(_row_slab_kernel, row0=row0, n_rows=n_rows),
        grid=(height // LANES,),
        in_specs=[pl.BlockSpec((None, LANES, k), lambda j: (layer, row0 // LANES + j, 0))],
        out_specs=pl.BlockSpec((None, LANES, k), lambda j: (0, j, 0)),
        out_shape=jax.ShapeDtypeStruct((1, height, k), BF16),
        compiler_params=_params("parallel"),
        name="row_slab_bf16",
    )(w_t)


def _gelu_tanh(x):
    c = math.sqrt(2.0 / math.pi)
    return x * (0.5 * (1.0 + jnp.tanh(c * (x + 0.044715 * (x * x * x)))))


def _ffn_in_kernel(a_ref, ah_ref, wg_ref, wv_ref, cwg_ref, cwv_ref, cbg_ref, cbv_ref, wo_ref,
                   o_ref, wo16_ref, *, tm, seq):
    i = pl.program_id(0)
    wo16_ref[...] = wo_ref[...].astype(wo16_ref.dtype)
    tn = o_ref.shape[1]
    a = a_ref[...]
    ah = ah_ref[...]
    seq_start = (i * tm) % seq == 0
    row8 = lax.broadcasted_iota(jnp.int32, (8, tn), 0)

    def conv(w_ref, cw_ref, cb_ref):
        w = w_ref[...].astype(BF16)
        u = _dot(a, w)
        uh = _dot(ah, w)
        uh = jnp.where(seq_start, 0.0, uh)
        p1 = uh[BF16_SUBLANES - 1:BF16_SUBLANES, :]
        p2 = uh[BF16_SUBLANES - 2:BF16_SUBLANES - 1, :]
        r1 = pltpu.roll(u, 1, 0)
        r2 = pltpu.roll(u, 2, 0)
        h1 = jnp.where(row8 == 0, p1, r1[:8, :])
        h2 = jnp.where(row8 == 0, p2, jnp.where(row8 == 1, p1, r2[:8, :]))
        u1 = jnp.concatenate([h1, r1[8:, :]], axis=0)
        u2 = jnp.concatenate([h2, r2[8:, :]], axis=0)
        cw = cw_ref[...]
        return cb_ref[...] + (cw[0:1, :] * u2 + cw[1:2, :] * u1 + cw[2:3, :] * u)

    gate = conv(wg_ref, cwg_ref, cbg_ref)
    val = conv(wv_ref, cwv_ref, cbv_ref)
    o_ref[...] = (_gelu_tanh(gate) * val).astype(o_ref.dtype)


def ffn_in(h, w_in, conv_w, conv_b, w_out, layer, seq, tm=2048, tn=256):
    t, k = h.shape
    f = w_in.shape[2] // 2
    d_out = w_out.shape[2]
    nf = f // tn
    n_steps = (t // tm) * nf
    assert f % (n_steps * BF16_SUBLANES) == 0
    slab = f // n_steps
    halo = BF16_SUBLANES
    hb = tm // halo
    conv_b = conv_b.reshape(conv_b.shape[0], 1, 2 * f)
    return pl.pallas_call(
        functools.partial(_ffn_in_kernel, tm=tm, seq=seq),
        grid=(t // tm, nf),
        in_specs=[
            pl.BlockSpec((tm, k), lambda i, j: (i, 0), pipeline_mode=pl.Buffered(1)),
            pl.BlockSpec((halo, k), lambda i, j: (jnp.maximum(i * hb - 1, 0), 0)),
            pl.BlockSpec((None, k, tn), lambda i, j: (layer, 0, j)),
            pl.BlockSpec((None, k, tn), lambda i, j: (layer, 0, j + nf)),
            pl.BlockSpec((None, CONV_W, tn), lambda i, j: (layer, 0, j)),
            pl.BlockSpec((None, CONV_W, tn), lambda i, j: (layer, 0, j + nf)),
            pl.BlockSpec((None, 1, tn), lambda i, j: (layer, 0, j)),
            pl.BlockSpec((None, 1, tn), lambda i, j: (layer, 0, j + nf)),
            pl.BlockSpec((None, slab, d_out), lambda i, j: (layer, i * nf + j, 0)),
        ],
        out_specs=[pl.BlockSpec((tm, tn), lambda i, j: (i, j)),
                   pl.BlockSpec((slab, d_out), lambda i, j: (i * nf + j, 0))],
        out_shape=[jax.ShapeDtypeStruct((t, f), BF16),
                   jax.ShapeDtypeStruct((f, d_out), BF16)],
        compiler_params=_params("parallel", "arbitrary"),
        name="ffn_in",
    )(h, h, w_in, w_in, conv_w, conv_w, conv_b, conv_b, w_out)


def _transpose_value_tiles(v_ref, vt_scr, tq):
    def transpose_tile(kb, c):
        start = pl.multiple_of(kb * tq, tq)
        vt_scr[kb] = v_ref[pl.ds(start, tq), :].astype(F32).T.astype(vt_scr.dtype)
        return c
    lax.fori_loop(0, vt_scr.shape[0], transpose_tile, 0)


def _sb_kernel(q_ref, k_ref, v_ref, o_ref, vt_scr, *, tq, scale, g):
    qi = pl.program_id(2)
    d = HEAD_DIM
    heads = [slice(hh * d, (hh + 1) * d) for hh in range(g)]
    qs = [q_ref[:, cs] for cs in heads]

    @pl.when(qi == 0)
    def _():
        _transpose_value_tiles(v_ref, vt_scr, tq)

    key = lax.broadcasted_iota(jnp.int32, (tq, tq), 0)
    qry = lax.broadcasted_iota(jnp.int32, (tq, tq), 1)
    strict = key < qry
    r2 = lax.broadcasted_iota(jnp.int32, (tq, 2 * tq), 0)
    c2 = lax.broadcasted_iota(jnp.int32, (tq, 2 * tq), 1)
    tri = (jnp.where(c2 >= tq, c2 - tq, c2) >= r2).astype(BF16)

    def block(kb, carry, diag):
        start = pl.multiple_of(kb * tq, tq)
        zs = [_dot_nt(k_ref[pl.ds(start, tq), cs], q) * (scale * LOG2E) for q, cs in zip(qs, heads)]
        csums = []
        for z in zs:
            neg_abs = lax.bitcast_convert_type(
                lax.bitcast_convert_type(z, jnp.uint32) | jnp.uint32(0x80000000), F32)
            sp = jnp.maximum(z, 0.0) + jnp.log2(1.0 + jnp.exp2(neg_abs))
            if diag:
                sp = jnp.where(strict, sp, 0.0)
            hi = sp.astype(BF16)
            lo = (sp - hi.astype(F32)).astype(BF16)
            csums.append(_dot(tri, jnp.concatenate([hi, lo], axis=0)))
        out = []
        for z, csum, cs, (rsum, acc) in zip(zs, csums, heads, carry):
            w = jnp.exp2(z - csum - rsum)
            if diag:
                w = jnp.where(strict, w, 0.0)
            acc = acc + _dot(vt_scr[kb, cs, :], w.astype(BF16))
            rsum = rsum + csum[0:1, :]
            out.append((rsum, acc))
        return tuple(out)

    init = tuple((jnp.zeros((1, tq), F32), jnp.zeros((d, tq), F32)) for _ in range(g))
    carry = block(qi, init, True)

    def body(it, carry):
        return block(qi - 1 - it, carry, False)

    carry = lax.fori_loop(0, qi, body, carry)
    for cs, (_, acc) in zip(heads, carry):
        o_ref[:, cs] = acc.T.astype(o_ref.dtype)


def sb_attention(qkv, batch, seq, n_heads, q_col, k_col, v_col, g=SB_HEADS_PER_STEP):
    tq = ATTN_TILE
    nq = seq // tq
    d = HEAD_DIM
    assert n_heads % g == 0 and q_col % g == 0 and k_col % g == 0 and v_col % g == 0
    return pl.pallas_call(
        functools.partial(_sb_kernel, tq=tq, scale=1.0 / math.sqrt(d), g=g),
        grid=(batch, n_heads // g, nq),
        in_specs=[
            pl.BlockSpec((tq, g * d), lambda b, h, i: (b * nq + i, q_col // g + h)),
            pl.BlockSpec((seq, g * d), lambda b, h, i: (b, k_col // g + h)),
            pl.BlockSpec((seq, g * d), lambda b, h, i: (b, v_col // g + h)),
        ],
        out_specs=pl.BlockSpec((tq, g * d), lambda b, h, i: (b * nq + i, h)),
        out_shape=jax.ShapeDtypeStruct((batch * seq, n_heads * d), BF16),
        scratch_shapes=[pltpu.VMEM((nq, g * d, tq), BF16)],
        compiler_params=_params("parallel", "parallel", "arbitrary"),
        name="sb_attention",
    )(qkv, qkv, qkv)


def _online_step_t(s, vt, m, l, acc):
    m_new = jnp.maximum(m, jnp.max(s, axis=0, keepdims=True))
    alpha = jnp.exp2(m - m_new)
    p = jnp.exp2(s - m_new)
    l = alpha * l + jnp.sum(p, axis=0, keepdims=True)
    acc = alpha * acc + _dot(vt, p.astype(BF16))
    return m_new, l, acc


def _chunk_causal(tq):
    row = lax.broadcasted_iota(jnp.int32, (tq, tq), 0)
    col = lax.broadcasted_iota(jnp.int32, (tq, tq), 1)
    return row, col, (col // CHUNK) <= (row // CHUNK)


def _diff_kernel(q_ref, k_ref, v_ref, lam_ref, g_ref, o_ref, vt_scr, *, tq, scale, lam_init, g):
    hg = pl.program_id(1)
    qi = pl.program_id(2)
    d = HEAD_DIM
    dv = 2 * HEAD_DIM
    qk_cols = [slice(c * d, (c + 1) * d) for c in range(2 * g)]
    v_cols = [slice(hh * dv, (hh + 1) * dv) for hh in range(g)]
    qs = [q_ref[:, cs] for cs in qk_cols]

    @pl.when(qi == 0)
    def _():
        _transpose_value_tiles(v_ref, vt_scr, tq)

    key, qry, _ = _chunk_causal(tq)
    visible = (key // CHUNK) <= (qry // CHUNK)
    dist = (qry - key).astype(F32)
    nbias_diag, nbias_off, slopes = [], [], []
    for hh in range(g):
        slope = lax.bitcast_convert_type(
            jnp.full((1, 1), (126 - (hg * g + hh)) * (1 << 23), jnp.int32), F32)
        slope = slope * LOG2E
        slopes.append(slope)
        nbias_diag.append(-slope * jnp.abs(dist))
        nbias_off.append(-slope * dist)

    def scores(kb, diag):
        start = pl.multiple_of(kb * tq, tq)
        ss = [_dot_nt(k_ref[pl.ds(start, tq), cs], q) * (scale * LOG2E) for q, cs in zip(qs, qk_cols)]
        for c in range(2 * g):
            hh = c // 2
            if diag:
                ss[c] = jnp.where(visible, ss[c] + nbias_diag[hh], NEG)
            else:
                ss[c] = ss[c] + (nbias_off[hh] - slopes[hh] * ((qi - kb) * tq).astype(F32))
        return ss

    def update(kb, ss, carry):
        return tuple(_online_step_t(ss[c], vt_scr[kb, v_cols[c // 2], :], *carry[c])
                     for c in range(2 * g))

    def pair(kb, carry, diag_second):
        ss_a = scores(kb, False)
        ss_b = scores(kb + 1, diag_second)
        return update(kb + 1, ss_b, update(kb, ss_a, carry))

    init = tuple((jnp.full((1, tq), NEG, F32), jnp.zeros((1, tq), F32), jnp.zeros((dv, tq), F32))
                 for _ in range(2 * g))
    carry = lax.fori_loop(0, qi // 2, lambda p, c: pair(2 * p, c, False), init)
    carry = lax.cond(qi % 2 == 1,
                     lambda c: pair(qi - 1, c, True),
                     lambda c: update(qi, scores(qi, True), c),
                     carry)

    lf = lam_ref[...]
    lam = (jnp.exp(jnp.sum(lf[0:1, :] * lf[1:2, :], axis=-1, keepdims=True))
           - jnp.exp(jnp.sum(lf[2:3, :] * lf[3:4, :], axis=-1, keepdims=True)) + lam_init)
    for hh in range(g):
        (_, l1, a1), (_, l2, a2) = carry[2 * hh], carry[2 * hh + 1]
        o = (a1 / l1 - lam * (a2 / l2)).T
        o_ref[:, v_cols[hh]] = (_rms(o, g_ref[...]) * (1.0 - lam_init)).astype(o_ref.dtype)


def diff_attention(qkv, diff_lambda, subln_g, batch, seq, n_heads, q_col, k_col, v_col, lam_init,
                   g=DIFF_HEADS_PER_STEP):
    tq = ATTN_TILE
    nq = seq // tq
    dv = 2 * HEAD_DIM
    assert n_heads == 8, "ALiBi slopes are built as exact powers of two"
    assert n_heads % g == 0 and q_col % g == 0 and k_col % g == 0 and v_col % g == 0
    return pl.pallas_call(
        functools.partial(_diff_kernel, tq=tq, scale=1.0 / math.sqrt(HEAD_DIM), lam_init=lam_init, g=g),
        grid=(batch, n_heads // g, nq),
        in_specs=[
            pl.BlockSpec((tq, g * dv), lambda b, h, i: (b * nq + i, q_col // g + h)),
            pl.BlockSpec((seq, g * dv), lambda b, h, i: (b, k_col // g + h)),
            pl.BlockSpec((seq, g * dv), lambda b, h, i: (b, v_col // g + h)),
            pl.BlockSpec((4, HEAD_DIM), lambda b, h, i: (0, 0)),
            pl.BlockSpec((1, dv), lambda b, h, i: (0, 0)),
        ],
        out_specs=pl.BlockSpec((tq, g * dv), lambda b, h, i: (b * nq + i, h)),
        out_shape=jax.ShapeDtypeStruct((batch * seq, n_heads * dv), BF16),
        scratch_shapes=[pltpu.VMEM((nq, g * dv, tq), BF16)],
        compiler_params=_params("parallel", "parallel", "arbitrary"),
        name="diff_attention",
    )(qkv, qkv, qkv, diff_lambda, subln_g.reshape(1, dv))


def _chunk_kernel(q_ref, k_ref, v_ref, tab_ref, o_ref, *, tq, scale, n_win, g):
    qi = pl.program_id(2)
    d = HEAD_DIM
    heads = [slice(hh * d, (hh + 1) * d) for hh in range(g)]
    kbs = [qi - (n_win - 1) + w for w in range(n_win)]
    starts = [pl.multiple_of(jnp.maximum(kb, 0) * tq, tq) for kb in kbs]
    scores = []
    for hh, cs in enumerate(heads):
        q = q_ref[:, cs]
        s_blocks = []
        for w in range(n_win):
            s = (_dot_nt(q, k_ref[pl.ds(starts[w], tq), cs]) * (scale * LOG2E)
                 + tab_ref[hh, :, w * tq:(w + 1) * tq])
            if w < n_win - 1:
                s = jnp.where(kbs[w] >= 0, s, NEG)
            s_blocks.append(s)
        scores.append(s_blocks)
    for cs, s_blocks in zip(heads, scores):
        m = s_blocks[0].max(axis=-1, keepdims=True)
        for s in s_blocks[1:]:
            m = jnp.maximum(m, s.max(axis=-1, keepdims=True))
        l = jnp.zeros_like(m)
        acc = jnp.zeros((tq, d), F32)
        for w, s in enumerate(s_blocks):
            p = jnp.exp2(s - m)
            l = l + jnp.sum(p, axis=-1, keepdims=True)
            acc = acc + _dot(p.astype(BF16), v_ref[pl.ds(starts[w], tq), cs])
        o_ref[:, cs] = (acc / l).astype(o_ref.dtype)


def chunk_bias_table(rel_bias, tq):
    left = LEFT_CHUNKS * CHUNK
    n_heads = rel_bias.shape[0]
    width = tq + left
    period = 2 * left
    assert left >= REL_CLIP and tq <= left
    rb = rel_bias.astype(F32)
    hi = jnp.broadcast_to(rb[:, -1:], (n_heads, left - REL_CLIP))
    mid = rb[:, ::-1]
    lo = jnp.broadcast_to(rb[:, :1], (n_heads, tq - REL_CLIP - 1))
    neg_d = jnp.broadcast_to(rb[:, -1:], (n_heads, period - width))
    diag = jnp.concatenate([hi, mid, lo, neg_d], axis=1)
    flat = jnp.tile(diag, (1, tq))[:, :tq * (period - 1)]
    bias = flat.reshape(n_heads, tq, period - 1)[:, :, :width]
    i = jnp.arange(tq)[:, None]
    j = jnp.arange(width)[None, :]
    qc, kc = i // CHUNK, j // CHUNK
    allowed = (kc >= qc) & (kc <= qc + LEFT_CHUNKS)
    return jnp.where(allowed[None], bias * LOG2E, NEG)


def chunk_attention(qkv, rel_bias, batch, seq, n_heads, q_col, k_col, v_col, g=CHUNK_HEADS_PER_STEP):
    tq = ATTN_TILE
    nq = seq // tq
    d = HEAD_DIM
    left = LEFT_CHUNKS * CHUNK
    assert left % tq == 0
    assert n_heads % g == 0 and q_col % g == 0 and k_col % g == 0 and v_col % g == 0
    n_win = left // tq + 1
    table = chunk_bias_table(rel_bias, tq)
    return pl.pallas_call(
        functools.partial(_chunk_kernel, tq=tq, scale=1.0 / math.sqrt(d), n_win=n_win, g=g),
        grid=(n_heads // g, batch, nq),
        in_specs=[
            pl.BlockSpec((tq, g * d), lambda h, b, i: (b * nq + i, q_col // g + h)),
            pl.BlockSpec((seq, g * d), lambda h, b, i: (b, k_col // g + h)),
            pl.BlockSpec((seq, g * d), lambda h, b, i: (b, v_col // g + h)),
            pl.BlockSpec((g, tq, tq + left), lambda h, b, i: (h, 0, 0)),
        ],
        out_specs=pl.BlockSpec((tq, g * d), lambda h, b, i: (b * nq + i, h)),
        out_shape=jax.ShapeDtypeStruct((batch * seq, n_heads * d), BF16),
        compiler_params=_params("parallel", "parallel", "arbitrary"),
        name="chunk_attention",
    )(qkv, qkv, qkv, table)


def rope_tables(seq):
    half = QK_ROPE // 2
    pos = jnp.arange(seq, dtype=F32)
    inv_freq = ROPE_THETA ** (-jnp.arange(0, QK_ROPE, 2, dtype=F32) / QK_ROPE)
    ang = pos[:, None] * inv_freq[None, :]
    cos, sin = jnp.cos(ang), jnp.sin(ang)
    z = lambda n: jnp.zeros((seq, n), F32)
    cos_t = jnp.concatenate([cos, cos, z(LANES - 2 * half)], axis=1)
    sin_a = jnp.concatenate([-sin, z(LANES - half)], axis=1)
    sin_b = jnp.concatenate([z(half), sin, z(LANES - 2 * half)], axis=1)
    return cos_t, sin_a, sin_b


def _rope(x, cos_t, sin_a, sin_b):
    half = QK_ROPE // 2
    return x * cos_t + pltpu.roll(x, LANES - half, 1) * sin_a + pltpu.roll(x, half, 1) * sin_b


def _mla_q_kernel(c_ref, g_ref, w_ref, cos_ref, sa_ref, sb_ref, o_ref, an_ref, *, hp):
    @pl.when(pl.program_id(1) == 0)
    def _():
        an_ref[...] = _rms(c_ref[...], g_ref[...]).astype(an_ref.dtype)

    hw = QK_NOPE + LANES
    res = _dot(an_ref[...], w_ref[...])
    cos_t, sin_a, sin_b = cos_ref[...], sa_ref[...], sb_ref[...]
    for hh in range(hp):
        c0 = hh * hw
        o_ref[:, c0:c0 + QK_NOPE] = res[:, c0:c0 + QK_NOPE].astype(o_ref.dtype)
        o_ref[:, c0 + QK_NOPE:c0 + hw] = _rope(res[:, c0 + QK_NOPE:c0 + hw],
                                               cos_t, sin_a, sin_b).astype(o_ref.dtype)


def mla_q(lat, g, w_q, tables, seq, n_heads, tm=512, hp=MLA_PROJ_HEADS_PER_STEP):
    t = lat.shape[0]
    kq = g.shape[0]
    hw = QK_NOPE + LANES
    npos = seq // tm
    tab_spec = pl.BlockSpec((tm, LANES), lambda i, j: (i % npos, 0))
    return pl.pallas_call(
        functools.partial(_mla_q_kernel, hp=hp),
        grid=(t // tm, n_heads // hp),
        in_specs=[
            pl.BlockSpec((tm, kq), lambda i, j: (i, 0)),
            pl.BlockSpec((1, kq), lambda i, j: (0, 0)),
            pl.BlockSpec((kq, hp * hw), lambda i, j: (0, j)),
            tab_spec, tab_spec, tab_spec,
        ],
        out_specs=pl.BlockSpec((tm, hp * hw), lambda i, j: (i, j)),
        out_shape=jax.ShapeDtypeStruct((t, n_heads * hw), BF16),
        scratch_shapes=[pltpu.VMEM((tm, kq), BF16)],
        compiler_params=_params("parallel", "arbitrary"),
        name="mla_q",
    )(lat, g.reshape(1, kq), w_q, *tables)


def _mla_kv_kernel(c_ref, kr_ref, g_ref, w_ref, cos_ref, sa_ref, sb_ref, k_ref, vt_ref,
                   an_ref, kr_scr, *, hp, tm, tk):
    @pl.when(pl.program_id(1) == 0)
    def _():
        an_ref[...] = _rms(c_ref[...], g_ref[...]).astype(an_ref.dtype)
        kr_scr[...] = _rope(kr_ref[...], cos_ref[...], sa_ref[...], sb_ref[...]).astype(kr_scr.dtype)

    hw = QK_NOPE + LANES
    wv = QK_NOPE + V_MLA
    res = _dot(an_ref[...], w_ref[...])
    kr = kr_scr[...]
    for hh in range(hp):
        k_ref[:, hh * hw:hh * hw + QK_NOPE] = res[:, hh * wv:hh * wv + QK_NOPE].astype(k_ref.dtype)
        k_ref[:, hh * hw + QK_NOPE:(hh + 1) * hw] = kr
        v = res[:, hh * wv + QK_NOPE:(hh + 1) * wv]
        for kt in range(tm // tk):
            vt_ref[kt, hh * V_MLA:(hh + 1) * V_MLA, :] = v[kt * tk:(kt + 1) * tk, :].T.astype(vt_ref.dtype)


def mla_kv(lat, g, w_kv, tables, seq, n_heads, ckv_col, kr_col, tm=512, hp=MLA_PROJ_HEADS_PER_STEP):
    t = lat.shape[0]
    kkv = g.shape[0]
    hw = QK_NOPE + LANES
    tk = ATTN_TILE
    assert tm % tk == 0
    npos = seq // tm
    tab_spec = pl.BlockSpec((tm, LANES), lambda i, j: (i % npos, 0))
    return pl.pallas_call(
        functools.partial(_mla_kv_kernel, hp=hp, tm=tm, tk=tk),
        grid=(t // tm, n_heads // hp),
        in_specs=[
            pl.BlockSpec((tm, kkv), lambda i, j: (i, ckv_col)),
            pl.BlockSpec((tm, LANES), lambda i, j: (i, kr_col)),
            pl.BlockSpec((1, kkv), lambda i, j: (0, 0)),
            pl.BlockSpec((kkv, hp * (QK_NOPE + V_MLA)), lambda i, j: (0, j)),
            tab_spec, tab_spec, tab_spec,
        ],
        out_specs=[pl.BlockSpec((tm, hp * hw), lambda i, j: (i, j)),
                   pl.BlockSpec((tm // tk, hp * V_MLA, tk), lambda i, j: (i, j, 0))],
        out_shape=[jax.ShapeDtypeStruct((t, n_heads * hw), BF16),
                   jax.ShapeDtypeStruct((t // tk, n_heads * V_MLA, tk), BF16)],
        scratch_shapes=[pltpu.VMEM((tm, kkv), BF16), pltpu.VMEM((tm, LANES), BF16)],
        compiler_params=_params("parallel", "arbitrary"),
        name="mla_kv",
    )(lat, lat, g.reshape(1, kkv), w_kv, *tables)


def _mla_attn_kernel(q_ref, k_ref, vt_ref, o_ref, *, tq, scale, g):
    qi = pl.program_id(2)
    hw = QK_NOPE + LANES
    dv = V_MLA
    qk_cols = [slice(hh * hw, (hh + 1) * hw) for hh in range(g)]
    v_cols = [slice(hh * dv, (hh + 1) * dv) for hh in range(g)]
    qs = [q_ref[:, cs] for cs in qk_cols]
    key, qry, _ = _chunk_causal(tq)
    visible = (key // CHUNK) <= (qry // CHUNK)

    def scores(kb, diag):
        start = pl.multiple_of(kb * tq, tq)
        ss = [_dot_nt(k_ref[pl.ds(start, tq), cs], q) * (scale * LOG2E) for q, cs in zip(qs, qk_cols)]
        if diag:
            ss = [jnp.where(visible, s, NEG) for s in ss]
        return ss

    def update(kb, ss, carry):
        return tuple(_online_step_t(s, vt_ref[kb, vs, :], *st)
                     for s, vs, st in zip(ss, v_cols, carry))

    def pair(kb, carry, diag_second):
        ss_a = scores(kb, False)
        ss_b = scores(kb + 1, diag_second)
        return update(kb + 1, ss_b, update(kb, ss_a, carry))

    init = tuple((jnp.full((1, tq), NEG, F32), jnp.zeros((1, tq), F32), jnp.zeros((dv, tq), F32))
                 for _ in range(g))
    carry = lax.fori_loop(0, qi // 2, lambda p, c: pair(2 * p, c, False), init)
    carry = lax.cond(qi % 2 == 1,
                     lambda c: pair(qi - 1, c, True),
                     lambda c: update(qi, scores(qi, True), c),
                     carry)
    for vs, (_, l, acc) in zip(v_cols, carry):
        o_ref[:, vs] = (acc / l).T.astype(o_ref.dtype)


def mla_attention(q, k, vt, batch, seq, n_heads, g=MLA_HEADS_PER_STEP):
    tq = ATTN_TILE
    nq = seq // tq
    hw = QK_NOPE + LANES
    assert n_heads % g == 0
    return pl.pallas_call(
        functools.partial(_mla_attn_kernel, tq=tq, scale=1.0 / math.sqrt(QK_NOPE + QK_ROPE), g=g),
        grid=(batch, n_heads // g, nq),
        in_specs=[
            pl.BlockSpec((tq, g * hw), lambda b, h, i: (b * nq + i, h)),
            pl.BlockSpec((seq, g * hw), lambda b, h, i: (b, h), pipeline_mode=pl.Buffered(1)),
            pl.BlockSpec((nq, g * V_MLA, tq), lambda b, h, i: (b, h, 0), pipeline_mode=pl.Buffered(1)),
        ],
        out_specs=pl.BlockSpec((tq, g * V_MLA), lambda b, h, i: (b * nq + i, h)),
        out_shape=jax.ShapeDtypeStruct((batch * seq, n_heads * V_MLA), BF16),
        compiler_params=_params("parallel", "parallel", "arbitrary"),
        name="mla_attention",
    )(q, k, vt)


def _even_mixer(hn, w_in, w_out, i, diff_lambda, subln_g, batch, seq, layer):
    d_model = hn.shape[1]
    w_sb = d_model // 2
    n_sb = w_sb // HEAD_DIM
    n_diff = w_sb // (2 * HEAD_DIM)
    cb = w_sb // LANES
    qkv = matmul([(hn, w_in, i, 0)], w_in.shape[2], BF16, tm=2048, tn=512, name="even_in_proj",
                 a_single_buffer=True)
    a = sb_attention(qkv, batch, seq, n_sb, 0, cb, 2 * cb)
    lam_init = 0.8 - 0.6 * math.exp(-0.3 * layer)
    bo = diff_attention(qkv, diff_lambda, subln_g, batch, seq, n_diff,
                        3 * cb // 2, 4 * cb // 2, 5 * cb // 2, lam_init)
    return matmul([(a, w_out, i, 0), (bo, w_out, i, 1)], d_model, F32, tm=1024, tn=512,
                  name="mix_out_proj")


def _odd_mixer(hn, w_in, w_out, i, rel_bias, q_norm_g, w_uq, kv_norm_g, w_ukv, batch, seq):
    d_model = hn.shape[1]
    w_ch = d_model // 2
    n_ch = w_ch // HEAD_DIM
    n_mla = w_ch // HEAD_DIM
    cb = w_ch // LANES
    q_lora, kv_lora = q_norm_g.shape[0], kv_norm_g.shape[0]
    n_attn = 3 * w_ch
    n_lat = q_lora + kv_lora + QK_ROPE
    lat_pad = -n_lat % LANES
    w_in_t = jnp.swapaxes(w_in, 1, 2)
    w_lat_t = row_slab_bf16(w_in_t, i, n_attn, n_lat + lat_pad)
    qkv = matmul([(hn, w_in_t, i, 0)], n_attn, BF16, tm=2048, tn=512, name="odd_in_proj", w_t=True,
                 a_single_buffer=True)
    lat = matmul([(hn, w_lat_t, 0, 0)], n_lat + lat_pad, F32, tm=512, tn=n_lat + lat_pad,
                 name="odd_lat_proj", w_t=True)
    c = chunk_attention(qkv, rel_bias, batch, seq, n_ch, 0, cb, 2 * cb)

    tables = rope_tables(seq)
    hw = QK_NOPE + LANES
    wq = w_uq.reshape(q_lora, n_mla, QK_NOPE + QK_ROPE)
    wq = jnp.pad(wq, ((0, 0), (0, 0), (0, hw - QK_NOPE - QK_ROPE))).reshape(q_lora, n_mla * hw).astype(BF16)
    q = mla_q(lat, q_norm_g, wq, tables, seq, n_mla)
    assert q_lora % kv_lora == 0 and (q_lora + kv_lora) % LANES == 0
    k, v = mla_kv(lat, kv_norm_g, w_ukv.astype(BF16), tables, seq, n_mla,
                  q_lora // kv_lora, (q_lora + kv_lora) // LANES)
    dm = mla_attention(q, k, v, batch, seq, n_mla)
    return matmul([(c, w_out, i, 0), (dm, w_out, i, 1)], d_model, F32, tm=1024, tn=512,
                  name="mix_out_proj")


def _ffn(h, w_in, conv_w, conv_b, w_out, layer, seq):
    g, w_out16 = ffn_in(h, w_in, conv_w, conv_b, w_out, layer, seq)
    return matmul([(g, w_out16[None], 0, 0)], w_out.shape[2], F32, tm=512, tn=256,
                  name="ffn_out_proj")


def kernel(x, norm_g, even_w_in, even_w_out, diff_lambda, diff_subln_g, odd_w_in, odd_w_out,
           ch_rel_bias, mla_q_norm_g, mla_w_uq, mla_kv_norm_g, mla_w_ukv, ffn_w_in, ffn_conv_w,
           ffn_conv_b, ffn_w_out):
    batch, seq, d_model = x.shape
    depth = norm_g.shape[0]
    xf = x.reshape(batch * seq, d_model)
    hn = norm_cast(xf, norm_g[0, 0])
    for layer in range(depth):
        g = norm_g[layer]
        i = layer // 2
        if layer % 2 == 0:
            mix = _even_mixer(hn, even_w_in, even_w_out, i, diff_lambda[i], diff_subln_g[i],
                              batch, seq, layer)
        else:
            mix = _odd_mixer(hn, odd_w_in, odd_w_out, i, ch_rel_bias[i], mla_q_norm_g[i],
                             mla_w_uq[i], mla_kv_norm_g[i], mla_w_ukv[i], batch, seq)
        xf, h2 = resid_norm(xf, mix, g[1], g[2])
        f = _ffn(h2, ffn_w_in, ffn_conv_w, ffn_conv_b, ffn_w_out, layer, seq)
        if layer + 1 < depth:
            xf, hn = resid_norm(xf, f, g[3], norm_g[layer + 1, 0])
        else:
            xf = resid(xf, f, g[3])
    return xf.reshape(batch, seq, d_model)
```

```python
import functools
import math

import jax
import jax.numpy as jnp
from jax import lax
from jax.experimental import pallas as pl
from jax.experimental.pallas import tpu as pltpu

F32 = jnp.float32
BF16 = jnp.bfloat16

CHUNK = 64
HEAD_DIM = 128
LEFT_CHUNKS = 8
REL_CLIP = 128
QK_NOPE = 128
QK_ROPE = 64
V_MLA = 128
ROPE_THETA = 10000.0
CONV_W = 3
EPS = 1e-6
NEG = -1e30
LOG2E = math.log2(math.e)

LANES = 128
BF16_SUBLANES = 16
VMEM_LIMIT = 52 * 1024 * 1024

ATTN_TILE = 256
SB_HEADS_PER_STEP = 8
MLA_HEADS_PER_STEP = 8
DIFF_HEADS_PER_STEP = 4
CHUNK_HEADS_PER_STEP = 4
MLA_PROJ_HEADS_PER_STEP = 4


def _params(*sem):
    return pltpu.CompilerParams(dimension_semantics=sem, vmem_limit_bytes=VMEM_LIMIT)


def _rms(x, g):
    ms = jnp.mean(x * x, axis=-1, keepdims=True)
    return x * lax.rsqrt(ms + EPS) * g


def _dot(a, b):
    return jnp.dot(a, b, preferred_element_type=F32)


def _dot_nt(a, b):
    return lax.dot_general(a, b, (((1,), (1,)), ((), ())), preferred_element_type=F32)


def _norm_cast_kernel(x_ref, g_ref, h_ref):
    h_ref[...] = _rms(x_ref[...], g_ref[...]).astype(h_ref.dtype)


def _resid_norm_kernel(x_ref, y_ref, g1_ref, g2_ref, xo_ref, h_ref):
    xn = x_ref[...] + _rms(y_ref[...], g1_ref[...])
    xo_ref[...] = xn
    h_ref[...] = _rms(xn, g2_ref[...]).astype(h_ref.dtype)


def _resid_kernel(x_ref, y_ref, g_ref, xo_ref):
    xo_ref[...] = x_ref[...] + _rms(y_ref[...], g_ref[...])


def _row_spec(tr, d):
    return pl.BlockSpec((tr, d), lambda i: (i, 0))


def _vec_spec(d):
    return pl.BlockSpec((1, d), lambda i: (0, 0))


def norm_cast(x, g, tr=256):
    t, d = x.shape
    return pl.pallas_call(
        _norm_cast_kernel,
        grid=(t // tr,),
        in_specs=[_row_spec(tr, d), _vec_spec(d)],
        out_specs=_row_spec(tr, d),
        out_shape=jax.ShapeDtypeStruct((t, d), BF16),
        compiler_params=_params("parallel"),
        name="norm_cast",
    )(x, g.reshape(1, d))


def resid_norm(x, y, g1, g2, tr=256):
    t, d = x.shape
    return pl.pallas_call(
        _resid_norm_kernel,
        grid=(t // tr,),
        in_specs=[_row_spec(tr, d), _row_spec(tr, d), _vec_spec(d), _vec_spec(d)],
        out_specs=[_row_spec(tr, d), _row_spec(tr, d)],
        out_shape=[jax.ShapeDtypeStruct((t, d), F32), jax.ShapeDtypeStruct((t, d), BF16)],
        compiler_params=_params("parallel"),
        name="resid_norm",
    )(x, y, g1.reshape(1, d), g2.reshape(1, d))


def resid(x, y, g, tr=256):
    t, d = x.shape
    return pl.pallas_call(
        _resid_kernel,
        grid=(t // tr,),
        in_specs=[_row_spec(tr, d), _row_spec(tr, d), _vec_spec(d)],
        out_specs=_row_spec(tr, d),
        out_shape=jax.ShapeDtypeStruct((t, d), F32),
        compiler_params=_params("parallel"),
        name="resid",
    )(x, y, g.reshape(1, d))


def _matmul_kernel(*refs, n_pairs, w_t):
    o_ref = refs[2 * n_pairs]
    dot = _dot_nt if w_t else _dot
    acc = dot(refs[0][...], refs[n_pairs][...].astype(BF16))
    for p in range(1, n_pairs):
        acc = acc + dot(refs[p][...], refs[n_pairs + p][...].astype(BF16))
    o_ref[...] = acc.astype(o_ref.dtype)


def matmul(pairs, n, out_dtype, tm, tn, name, w_t=False, a_single_buffer=False):
    m = pairs[0][0].shape[0]
    n_pairs = len(pairs)
    a_specs, w_specs, args_a, args_w = [], [], [], []
    for a, w, layer, rb in pairs:
        k = a.shape[1]
        mode = dict(pipeline_mode=pl.Buffered(1)) if a_single_buffer else {}
        a_specs.append(pl.BlockSpec((tm, k), lambda i, j: (i, 0), **mode))
        if w_t:
            w_specs.append(pl.BlockSpec((None, tn, k), lambda i, j, layer=layer, rb=rb: (layer, j, rb)))
        else:
            w_specs.append(pl.BlockSpec((None, k, tn), lambda i, j, layer=layer, rb=rb: (layer, rb, j)))
        args_a.append(a)
        args_w.append(w)
    return pl.pallas_call(
        functools.partial(_matmul_kernel, n_pairs=n_pairs, w_t=w_t),
        grid=(m // tm, n // tn),
        in_specs=a_specs + w_specs,
        out_specs=pl.BlockSpec((tm, tn), lambda i, j: (i, j)),
        out_shape=jax.ShapeDtypeStruct((m, n), out_dtype),
        compiler_params=_params("parallel", "arbitrary"),
        name=name,
    )(*args_a, *args_w)


def _row_slab_kernel(w_ref, o_ref, *, row0, n_rows):
    row = row0 + pl.program_id(0) * LANES + lax.broadcasted_iota(jnp.int32, w_ref.shape, 0)
    o_ref[...] = jnp.where(row < n_rows, w_ref[...], 0.0).astype(o_ref.dtype)


def row_slab_bf16(w_t, layer, row0, height):
    _, n_rows, k = w_t.shape
    assert row0 % LANES == 0 and height % LANES == 0
    return pl.pallas_call(
        functools.partial(_row_slab_kernel, row0=row0, n_rows=n_rows),
        grid=(height // LANES,),
        in_specs=[pl.BlockSpec((None, LANES, k), lambda j: (layer, row0 // LANES + j, 0))],
        out_specs=pl.BlockSpec((None, LANES, k), lambda j: (0, j, 0)),
        out_shape=jax.ShapeDtypeStruct((1, height, k), BF16),
        compiler_params=_params("parallel"),
        name="row_slab_bf16",
    )(w_t)


def _gelu_tanh(x):
    c = math.sqrt(2.0 / math.pi)
    return x * (0.5 * (1.0 + jnp.tanh(c * (x + 0.044715 * (x * x * x)))))


def _ffn_in_kernel(a_ref, ah_ref, wg_ref, wv_ref, cwg_ref, cwv_ref, cbg_ref, cbv_ref, wo_ref,
                   o_ref, wo16_ref, *, tm, seq):
    i = pl.program_id(0)
    wo16_ref[...] = wo_ref[...].astype(wo16_ref.dtype)
    tn = o_ref.shape[1]
    a = a_ref[...]
    ah = ah_ref[...]
    seq_start = (i * tm) % seq == 0
    row8 = lax.broadcasted_iota(jnp.int32, (8, tn), 0)

    def conv(w_ref, cw_ref, cb_ref):
        w = w_ref[...].astype(BF16)
        u = _dot(a, w)
        uh = _dot(ah, w)
        uh = jnp.where(seq_start, 0.0, uh)
        p1 = uh[BF16_SUBLANES - 1:BF16_SUBLANES, :]
        p2 = uh[BF16_SUBLANES - 2:BF16_SUBLANES - 1, :]
        r1 = pltpu.roll(u, 1, 0)
        r2 = pltpu.roll(u, 2, 0)
        h1 = jnp.where(row8 == 0, p1, r1[:8, :])
        h2 = jnp.where(row8 == 0, p2, jnp.where(row8 == 1, p1, r2[:8, :]))
        u1 = jnp.concatenate([h1, r1[8:, :]], axis=0)
        u2 = jnp.concatenate([h2, r2[8:, :]], axis=0)
        cw = cw_ref[...]
        return cb_ref[...] + (cw[0:1, :] * u2 + cw[1:2, :] * u1 + cw[2:3, :] * u)

    gate = conv(wg_ref, cwg_ref, cbg_ref)
    val = conv(wv_ref, cwv_ref, cbv_ref)
    o_ref[...] = (_gelu_tanh(gate) * val).astype(o_ref.dtype)


def ffn_in(h, w_in, conv_w, conv_b, w_out, layer, seq, tm=2048, tn=256):
    t, k = h.shape
    f = w_in.shape[2] // 2
    d_out = w_out.shape[2]
    nf = f // tn
    n_steps = (t // tm) * nf
    assert f % (n_steps * BF16_SUBLANES) == 0
    slab = f // n_steps
    halo = BF16_SUBLANES
    hb = tm // halo
    conv_b = conv_b.reshape(conv_b.shape[0], 1, 2 * f)
    return pl.pallas_call(
        functools.partial(_ffn_in_kernel, tm=tm, seq=seq),
        grid=(t // tm, nf),
        in_specs=[
            pl.BlockSpec((tm, k), lambda i, j: (i, 0), pipeline_mode=pl.Buffered(1)),
            pl.BlockSpec((halo, k), lambda i, j: (jnp.maximum(i * hb - 1, 0), 0)),
            pl.BlockSpec((None, k, tn), lambda i, j: (layer, 0, j)),
            pl.BlockSpec((None, k, tn), lambda i, j: (layer, 0, j + nf)),
            pl.BlockSpec((None, CONV_W, tn), lambda i, j: (layer, 0, j)),
            pl.BlockSpec((None, CONV_W, tn), lambda i, j: (layer, 0, j + nf)),
            pl.BlockSpec((None, 1, tn), lambda i, j: (layer, 0, j)),
            pl.BlockSpec((None, 1, tn), lambda i, j: (layer, 0, j + nf)),
            pl.BlockSpec((None, slab, d_out), lambda i, j: (layer, i * nf + j, 0)),
        ],
        out_specs=[pl.BlockSpec((tm, tn), lambda i, j: (i, j)),
                   pl.BlockSpec((slab, d_out), lambda i, j: (i * nf + j, 0))],
        out_shape=[jax.ShapeDtypeStruct((t, f), BF16),
                   jax.ShapeDtypeStruct((f, d_out), BF16)],
        compiler_params=_params("parallel", "arbitrary"),
        name="ffn_in",
    )(h, h, w_in, w_in, conv_w, conv_w, conv_b, conv_b, w_out)


def _transpose_value_tiles(v_ref, vt_scr, tq):
    def transpose_tile(kb, c):
        start = pl.multiple_of(kb * tq, tq)
        vt_scr[kb] = v_ref[pl.ds(start, tq), :].astype(F32).T.astype(vt_scr.dtype)
        return c
    lax.fori_loop(0, vt_scr.shape[0], transpose_tile, 0)


def _sb_kernel(q_ref, k_ref, v_ref, o_ref, vt_scr, *, tq, scale, g):
    qi = pl.program_id(2)
    d = HEAD_DIM
    heads = [slice(hh * d, (hh + 1) * d) for hh in range(g)]
    qs = [q_ref[:, cs] for cs in heads]

    @pl.when(qi == 0)
    def _():
        _transpose_value_tiles(v_ref, vt_scr, tq)

    key = lax.broadcasted_iota(jnp.int32, (tq, tq), 0)
    qry = lax.broadcasted_iota(jnp.int32, (tq, tq), 1)
    strict = key < qry
    r2 = lax.broadcasted_iota(jnp.int32, (tq, 2 * tq), 0)
    c2 = lax.broadcasted_iota(jnp.int32, (tq, 2 * tq), 1)
    tri = (jnp.where(c2 >= tq, c2 - tq, c2) >= r2).astype(BF16)

    def block(kb, carry, diag):
        start = pl.multiple_of(kb * tq, tq)
        zs = [_dot_nt(k_ref[pl.ds(start, tq), cs], q) * (scale * LOG2E) for q, cs in zip(qs, heads)]
        csums = []
        for z in zs:
            neg_abs = lax.bitcast_convert_type(
                lax.bitcast_convert_type(z, jnp.uint32) | jnp.uint32(0x80000000), F32)
            sp = jnp.maximum(z, 0.0) + jnp.log2(1.0 + jnp.exp2(neg_abs))
            if diag:
                sp = jnp.where(strict, sp, 0.0)
            hi = sp.astype(BF16)
            lo = (sp - hi.astype(F32)).astype(BF16)
            csums.append(_dot(tri, jnp.concatenate([hi, lo], axis=0)))
        out = []
        for z, csum, cs, (rsum, acc) in zip(zs, csums, heads, carry):
            w = jnp.exp2(z - csum - rsum)
            if diag:
                w = jnp.where(strict, w, 0.0)
            acc = acc + _dot(vt_scr[kb, cs, :], w.astype(BF16))
            rsum = rsum + csum[0:1, :]
            out.append((rsum, acc))
        return tuple(out)

    init = tuple((jnp.zeros((1, tq), F32), jnp.zeros((d, tq), F32)) for _ in range(g))
    carry = block(qi, init, True)

    def body(it, carry):
        return block(qi - 1 - it, carry, False)

    carry = lax.fori_loop(0, qi, body, carry)
    for cs, (_, acc) in zip(heads, carry):
        o_ref[:, cs] = acc.T.astype(o_ref.dtype)


def sb_attention(qkv, batch, seq, n_heads, q_col, k_col, v_col, g=SB_HEADS_PER_STEP):
    tq = ATTN_TILE
    nq = seq // tq
    d = HEAD_DIM
    assert n_heads % g == 0 and q_col % g == 0 and k_col % g == 0 and v_col % g == 0
    return pl.pallas_call(
        functools.partial(_sb_kernel, tq=tq, scale=1.0 / math.sqrt(d), g=g),
        grid=(batch, n_heads // g, nq),
        in_specs=[
            pl.BlockSpec((tq, g * d), lambda b, h, i: (b * nq + i, q_col // g + h)),
            pl.BlockSpec((seq, g * d), lambda b, h, i: (b, k_col // g + h)),
            pl.BlockSpec((seq, g * d), lambda b, h, i: (b, v_col // g + h)),
        ],
        out_specs=pl.BlockSpec((tq, g * d), lambda b, h, i: (b * nq + i, h)),
        out_shape=jax.ShapeDtypeStruct((batch * seq, n_heads * d), BF16),
        scratch_shapes=[pltpu.VMEM((nq, g * d, tq), BF16)],
        compiler_params=_params("parallel", "parallel", "arbitrary"),
        name="sb_attention",
    )(qkv, qkv, qkv)


def _online_step_t(s, vt, m, l, acc):
    m_new = jnp.maximum(m, jnp.max(s, axis=0, keepdims=True))
    alpha = jnp.exp2(m - m_new)
    p = jnp.exp2(s - m_new)
    l = alpha * l + jnp.sum(p, axis=0, keepdims=True)
    acc = alpha * acc + _dot(vt, p.astype(BF16))
    return m_new, l, acc


def _chunk_causal(tq):
    row = lax.broadcasted_iota(jnp.int32, (tq, tq), 0)
    col = lax.broadcasted_iota(jnp.int32, (tq, tq), 1)
    return row, col, (col // CHUNK) <= (row // CHUNK)


def _diff_kernel(q_ref, k_ref, v_ref, lam_ref, g_ref, o_ref, vt_scr, *, tq, scale, lam_init, g):
    hg = pl.program_id(1)
    qi = pl.program_id(2)
    d = HEAD_DIM
    dv = 2 * HEAD_DIM
    qk_cols = [slice(c * d, (c + 1) * d) for c in range(2 * g)]
    v_cols = [slice(hh * dv, (hh + 1) * dv) for hh in range(g)]
    qs = [q_ref[:, cs] for cs in qk_cols]

    @pl.when(qi == 0)
    def _():
        _transpose_value_tiles(v_ref, vt_scr, tq)

    key, qry, _ = _chunk_causal(tq)
    visible = (key // CHUNK) <= (qry // CHUNK)
    dist = (qry - key).astype(F32)
    nbias_diag, nbias_off, slopes = [], [], []
    for hh in range(g):
        slope = lax.bitcast_convert_type(
            jnp.full((1, 1), (126 - (hg * g + hh)) * (1 << 23), jnp.int32), F32)
        slope = slope * LOG2E
        slopes.append(slope)
        nbias_diag.append(-slope * jnp.abs(dist))
        nbias_off.append(-slope * dist)

    def block(kb, carry, diag):
        start = pl.multiple_of(kb * tq, tq)
        ss = [_dot_nt(k_ref[pl.ds(start, tq), cs], q) * (scale * LOG2E) for q, cs in zip(qs, qk_cols)]
        for c in range(2 * g):
            hh = c // 2
            if diag:
                ss[c] = jnp.where(visible, ss[c] + nbias_diag[hh], NEG)
            else:
                ss[c] = ss[c] + (nbias_off[hh] - slopes[hh] * ((qi - kb) * tq).astype(F32))
        return tuple(_online_step_t(ss[c], vt_scr[kb, v_cols[c // 2], :], *carry[c])
                     for c in range(2 * g))

    def body(kb, carry):
        return block(kb, carry, False)

    init = tuple((jnp.full((1, tq), NEG, F32), jnp.zeros((1, tq), F32), jnp.zeros((dv, tq), F32))
                 for _ in range(2 * g))
    carry = lax.fori_loop(0, qi, body, init)
    carry = block(qi, carry, True)

    lf = lam_ref[...]
    lam = (jnp.exp(jnp.sum(lf[0:1, :] * lf[1:2, :], axis=-1, keepdims=True))
           - jnp.exp(jnp.sum(lf[2:3, :] * lf[3:4, :], axis=-1, keepdims=True)) + lam_init)
    for hh in range(g):
        (_, l1, a1), (_, l2, a2) = carry[2 * hh], carry[2 * hh + 1]
        o = (a1 / l1 - lam * (a2 / l2)).T
        o_ref[:, v_cols[hh]] = (_rms(o, g_ref[...]) * (1.0 - lam_init)).astype(o_ref.dtype)


def diff_attention(qkv, diff_lambda, subln_g, batch, seq, n_heads, q_col, k_col, v_col, lam_init,
                   g=DIFF_HEADS_PER_STEP):
    tq = ATTN_TILE
    nq = seq // tq
    dv = 2 * HEAD_DIM
    assert n_heads == 8, "ALiBi slopes are built as exact powers of two"
    assert n_heads % g == 0 and q_col % g == 0 and k_col % g == 0 and v_col % g == 0
    return pl.pallas_call(
        functools.partial(_diff_kernel, tq=tq, scale=1.0 / math.sqrt(HEAD_DIM), lam_init=lam_init, g=g),
        grid=(batch, n_heads // g, nq),
        in_specs=[
            pl.BlockSpec((tq, g * dv), lambda b, h, i: (b * nq + i, q_col // g + h)),
            pl.BlockSpec((seq, g * dv), lambda b, h, i: (b, k_col // g + h)),
            pl.BlockSpec((seq, g * dv), lambda b, h, i: (b, v_col // g + h)),
            pl.BlockSpec((4, HEAD_DIM), lambda b, h, i: (0, 0)),
            pl.BlockSpec((1, dv), lambda b, h, i: (0, 0)),
        ],
        out_specs=pl.BlockSpec((tq, g * dv), lambda b, h, i: (b * nq + i, h)),
        out_shape=jax.ShapeDtypeStruct((batch * seq, n_heads * dv), BF16),
        scratch_shapes=[pltpu.VMEM((nq, g * dv, tq), BF16)],
        compiler_params=_params("parallel", "parallel", "arbitrary"),
        name="diff_attention",
    )(qkv, qkv, qkv, diff_lambda, subln_g.reshape(1, dv))


def _chunk_kernel(q_ref, k_ref, v_ref, tab_ref, o_ref, *, tq, scale, n_win, g):
    qi = pl.program_id(2)
    d = HEAD_DIM
    heads = [slice(hh * d, (hh + 1) * d) for hh in range(g)]
    kbs = [qi - (n_win - 1) + w for w in range(n_win)]
    starts = [pl.multiple_of(jnp.maximum(kb, 0) * tq, tq) for kb in kbs]
    scores = []
    for hh, cs in enumerate(heads):
        q = q_ref[:, cs]
        s_blocks = []
        for w in range(n_win):
            s = (_dot_nt(q, k_ref[pl.ds(starts[w], tq), cs]) * (scale * LOG2E)
                 + tab_ref[hh, :, w * tq:(w + 1) * tq])
            if w < n_win - 1:
                s = jnp.where(kbs[w] >= 0, s, NEG)
            s_blocks.append(s)
        scores.append(s_blocks)
    for cs, s_blocks in zip(heads, scores):
        m = s_blocks[0].max(axis=-1, keepdims=True)
        for s in s_blocks[1:]:
            m = jnp.maximum(m, s.max(axis=-1, keepdims=True))
        l = jnp.zeros_like(m)
        acc = jnp.zeros((tq, d), F32)
        for w, s in enumerate(s_blocks):
            p = jnp.exp2(s - m)
            l = l + jnp.sum(p, axis=-1, keepdims=True)
            acc = acc + _dot(p.astype(BF16), v_ref[pl.ds(starts[w], tq), cs])
        o_ref[:, cs] = (acc / l).astype(o_ref.dtype)


def chunk_bias_table(rel_bias, tq):
    left = LEFT_CHUNKS * CHUNK
    n_heads = rel_bias.shape[0]
    width = tq + left
    period = 2 * left
    assert left >= REL_CLIP and tq <= left
    rb = rel_bias.astype(F32)
    hi = jnp.broadcast_to(rb[:, -1:], (n_heads, left - REL_CLIP))
    mid = rb[:, ::-1]
    lo = jnp.broadcast_to(rb[:, :1], (n_heads, tq - REL_CLIP - 1))
    neg_d = jnp.broadcast_to(rb[:, -1:], (n_heads, period - width))
    diag = jnp.concatenate([hi, mid, lo, neg_d], axis=1)
    flat = jnp.tile(diag, (1, tq))[:, :tq * (period - 1)]
    bias = flat.reshape(n_heads, tq, period - 1)[:, :, :width]
    i = jnp.arange(tq)[:, None]
    j = jnp.arange(width)[None, :]
    qc, kc = i // CHUNK, j // CHUNK
    allowed = (kc >= qc) & (kc <= qc + LEFT_CHUNKS)
    return jnp.where(allowed[None], bias * LOG2E, NEG)


def chunk_attention(qkv, rel_bias, batch, seq, n_heads, q_col, k_col, v_col, g=CHUNK_HEADS_PER_STEP):
    tq = ATTN_TILE
    nq = seq // tq
    d = HEAD_DIM
    left = LEFT_CHUNKS * CHUNK
    assert left % tq == 0
    assert n_heads % g == 0 and q_col % g == 0 and k_col % g == 0 and v_col % g == 0
    n_win = left // tq + 1
    table = chunk_bias_table(rel_bias, tq)
    return pl.pallas_call(
        functools.partial(_chunk_kernel, tq=tq, scale=1.0 / math.sqrt(d), n_win=n_win, g=g),
        grid=(n_heads // g, batch, nq),
        in_specs=[
            pl.BlockSpec((tq, g * d), lambda h, b, i: (b * nq + i, q_col // g + h)),
            pl.BlockSpec((seq, g * d), lambda h, b, i: (b, k_col // g + h)),
            pl.BlockSpec((seq, g * d), lambda h, b, i: (b, v_col // g + h)),
            pl.BlockSpec((g, tq, tq + left), lambda h, b, i: (h, 0, 0)),
        ],
        out_specs=pl.BlockSpec((tq, g * d), lambda h, b, i: (b * nq + i, h)),
        out_shape=jax.ShapeDtypeStruct((batch * seq, n_heads * d), BF16),
        compiler_params=_params("parallel", "parallel", "arbitrary"),
        name="chunk_attention",
    )(qkv, qkv, qkv, table)


def rope_tables(seq):
    half = QK_ROPE // 2
    pos = jnp.arange(seq, dtype=F32)
    inv_freq = ROPE_THETA ** (-jnp.arange(0, QK_ROPE, 2, dtype=F32) / QK_ROPE)
    ang = pos[:, None] * inv_freq[None, :]
    cos, sin = jnp.cos(ang), jnp.sin(ang)
    z = lambda n: jnp.zeros((seq, n), F32)
    cos_t = jnp.concatenate([cos, cos, z(LANES - 2 * half)], axis=1)
    sin_a = jnp.concatenate([-sin, z(LANES - half)], axis=1)
    sin_b = jnp.concatenate([z(half), sin, z(LANES - 2 * half)], axis=1)
    return cos_t, sin_a, sin_b


def _rope(x, cos_t, sin_a, sin_b):
    half = QK_ROPE // 2
    return x * cos_t + pltpu.roll(x, LANES - half, 1) * sin_a + pltpu.roll(x, half, 1) * sin_b


def _mla_q_kernel(c_ref, g_ref, w_ref, cos_ref, sa_ref, sb_ref, o_ref, an_ref, *, hp):
    @pl.when(pl.program_id(1) == 0)
    def _():
        an_ref[...] = _rms(c_ref[...], g_ref[...]).astype(an_ref.dtype)

    hw = QK_NOPE + LANES
    res = _dot(an_ref[...], w_ref[...])
    cos_t, sin_a, sin_b = cos_ref[...], sa_ref[...], sb_ref[...]
    for hh in range(hp):
        c0 = hh * hw
        o_ref[:, c0:c0 + QK_NOPE] = res[:, c0:c0 + QK_NOPE].astype(o_ref.dtype)
        o_ref[:, c0 + QK_NOPE:c0 + hw] = _rope(res[:, c0 + QK_NOPE:c0 + hw],
                                               cos_t, sin_a, sin_b).astype(o_ref.dtype)


def mla_q(lat, g, w_q, tables, seq, n_heads, tm=512, hp=MLA_PROJ_HEADS_PER_STEP):
    t = lat.shape[0]
    kq = g.shape[0]
    hw = QK_NOPE + LANES
    npos = seq // tm
    tab_spec = pl.BlockSpec((tm, LANES), lambda i, j: (i % npos, 0))
    return pl.pallas_call(
        functools.partial(_mla_q_kernel, hp=hp),
        grid=(t // tm, n_heads // hp),
        in_specs=[
            pl.BlockSpec((tm, kq), lambda i, j: (i, 0)),
            pl.BlockSpec((1, kq), lambda i, j: (0, 0)),
            pl.BlockSpec((kq, hp * hw), lambda i, j: (0, j)),
            tab_spec, tab_spec, tab_spec,
        ],
        out_specs=pl.BlockSpec((tm, hp * hw), lambda i, j: (i, j)),
        out_shape=jax.ShapeDtypeStruct((t, n_heads * hw), BF16),
        scratch_shapes=[pltpu.VMEM((tm, kq), BF16)],
        compiler_params=_params("parallel", "arbitrary"),
        name="mla_q",
    )(lat, g.reshape(1, kq), w_q, *tables)


def _mla_kv_kernel(c_ref, kr_ref, g_ref, w_ref, cos_ref, sa_ref, sb_ref, k_ref, vt_ref,
                   an_ref, kr_scr, *, hp, tm, tk):
    @pl.when(pl.program_id(1) == 0)
    def _():
        an_ref[...] = _rms(c_ref[...], g_ref[...]).astype(an_ref.dtype)
        kr_scr[...] = _rope(kr_ref[...], cos_ref[...], sa_ref[...], sb_ref[...]).astype(kr_scr.dtype)

    hw = QK_NOPE + LANES
    wv = QK_NOPE + V_MLA
    res = _dot(an_ref[...], w_ref[...])
    kr = kr_scr[...]
    for hh in range(hp):
        k_ref[:, hh * hw:hh * hw + QK_NOPE] = res[:, hh * wv:hh * wv + QK_NOPE].astype(k_ref.dtype)
        k_ref[:, hh * hw + QK_NOPE:(hh + 1) * hw] = kr
        v = res[:, hh * wv + QK_NOPE:(hh + 1) * wv]
        for kt in range(tm // tk):
            vt_ref[kt, hh * V_MLA:(hh + 1) * V_MLA, :] = v[kt * tk:(kt + 1) * tk, :].T.astype(vt_ref.dtype)


def mla_kv(lat, g, w_kv, tables, seq, n_heads, ckv_col, kr_col, tm=512, hp=MLA_PROJ_HEADS_PER_STEP):
    t = lat.shape[0]
    kkv = g.shape[0]
    hw = QK_NOPE + LANES
    tk = ATTN_TILE
    assert tm % tk == 0
    npos = seq // tm
    tab_spec = pl.BlockSpec((tm, LANES), lambda i, j: (i % npos, 0))
    return pl.pallas_call(
        functools.partial(_mla_kv_kernel, hp=hp, tm=tm, tk=tk),
        grid=(t // tm, n_heads // hp),
        in_specs=[
            pl.BlockSpec((tm, kkv), lambda i, j: (i, ckv_col)),
            pl.BlockSpec((tm, LANES), lambda i, j: (i, kr_col)),
            pl.BlockSpec((1, kkv), lambda i, j: (0, 0)),
            pl.BlockSpec((kkv, hp * (QK_NOPE + V_MLA)), lambda i, j: (0, j)),
            tab_spec, tab_spec, tab_spec,
        ],
        out_specs=[pl.BlockSpec((tm, hp * hw), lambda i, j: (i, j)),
                   pl.BlockSpec((tm // tk, hp * V_MLA, tk), lambda i, j: (i, j, 0))],
        out_shape=[jax.ShapeDtypeStruct((t, n_heads * hw), BF16),
                   jax.ShapeDtypeStruct((t // tk, n_heads * V_MLA, tk), BF16)],
        scratch_shapes=[pltpu.VMEM((tm, kkv), BF16), pltpu.VMEM((tm, LANES), BF16)],
        compiler_params=_params("parallel", "arbitrary"),
        name="mla_kv",
    )(lat, lat, g.reshape(1, kkv), w_kv, *tables)


def _mla_attn_kernel(q_ref, k_ref, vt_ref, o_ref, *, tq, scale, g):
    qi = pl.program_id(2)
    hw = QK_NOPE + LANES
    dv = V_MLA
    qk_cols = [slice(hh * hw, (hh + 1) * hw) for hh in range(g)]
    v_cols = [slice(hh * dv, (hh + 1) * dv) for hh in range(g)]
    qs = [q_ref[:, cs] for cs in qk_cols]
    key, qry, _ = _chunk_causal(tq)
    visible = (key // CHUNK) <= (qry // CHUNK)

    def scores(kb, diag):
        start = pl.multiple_of(kb * tq, tq)
        ss = [_dot_nt(k_ref[pl.ds(start, tq), cs], q) * (scale * LOG2E) for q, cs in zip(qs, qk_cols)]
        if diag:
            ss = [jnp.where(visible, s, NEG) for s in ss]
        return ss

    def update(kb, ss, carry):
        return tuple(_online_step_t(s, vt_ref[kb, vs, :], *st)
                     for s, vs, st in zip(ss, v_cols, carry))

    def pair(kb, carry, diag_second):
        ss_a = scores(kb, False)
        ss_b = scores(kb + 1, diag_second)
        return update(kb + 1, ss_b, update(kb, ss_a, carry))

    init = tuple((jnp.full((1, tq), NEG, F32), jnp.zeros((1, tq), F32), jnp.zeros((dv, tq), F32))
                 for _ in range(g))
    carry = lax.fori_loop(0, qi // 2, lambda p, c: pair(2 * p, c, False), init)
    carry = lax.cond(qi % 2 == 1,
                     lambda c: pair(qi - 1, c, True),
                     lambda c: update(qi, scores(qi, True), c),
                     carry)
    for vs, (_, l, acc) in zip(v_cols, carry):
        o_ref[:, vs] = (acc / l).T.astype(o_ref.dtype)


def mla_attention(q, k, vt, batch, seq, n_heads, g=MLA_HEADS_PER_STEP):
    tq = ATTN_TILE
    nq = seq // tq
    hw = QK_NOPE + LANES
    assert n_heads % g == 0
    return pl.pallas_call(
        functools.partial(_mla_attn_kernel, tq=tq, scale=1.0 / math.sqrt(QK_NOPE + QK_ROPE), g=g),
        grid=(batch, n_heads // g, nq),
        in_specs=[
            pl.BlockSpec((tq, g * hw), lambda b, h, i: (b * nq + i, h)),
            pl.BlockSpec((seq, g * hw), lambda b, h, i: (b, h), pipeline_mode=pl.Buffered(1)),
            pl.BlockSpec((nq, g * V_MLA, tq), lambda b, h, i: (b, h, 0), pipeline_mode=pl.Buffered(1)),
        ],
        out_specs=pl.BlockSpec((tq, g * V_MLA), lambda b, h, i: (b * nq + i, h)),
        out_shape=jax.ShapeDtypeStruct((batch * seq, n_heads * V_MLA), BF16),
        compiler_params=_params("parallel", "parallel", "arbitrary"),
        name="mla_attention",
    )(q, k, vt)


def _even_mixer(hn, w_in, w_out, i, diff_lambda, subln_g, batch, seq, layer):
    d_model = hn.shape[1]
    w_sb = d_model // 2
    n_sb = w_sb // HEAD_DIM
    n_diff = w_sb // (2 * HEAD_DIM)
    cb = w_sb // LANES
    qkv = matmul([(hn, w_in, i, 0)], w_in.shape[2], BF16, tm=2048, tn=512, name="even_in_proj",
                 a_single_buffer=True)
    a = sb_attention(qkv, batch, seq, n_sb, 0, cb, 2 * cb)
    lam_init = 0.8 - 0.6 * math.exp(-0.3 * layer)
    bo = diff_attention(qkv, diff_lambda, subln_g, batch, seq, n_diff,
                        3 * cb // 2, 4 * cb // 2, 5 * cb // 2, lam_init)
    return matmul([(a, w_out, i, 0), (bo, w_out, i, 1)], d_model, F32, tm=1024, tn=512,
                  name="mix_out_proj")


def _odd_mixer(hn, w_in, w_out, i, rel_bias, q_norm_g, w_uq, kv_norm_g, w_ukv, batch, seq):
    d_model = hn.shape[1]
    w_ch = d_model // 2
    n_ch = w_ch // HEAD_DIM
    n_mla = w_ch // HEAD_DIM
    cb = w_ch // LANES
    q_lora, kv_lora = q_norm_g.shape[0], kv_norm_g.shape[0]
    n_attn = 3 * w_ch
    n_lat = q_lora + kv_lora + QK_ROPE
    lat_pad = -n_lat % LANES
    w_in_t = jnp.swapaxes(w_in, 1, 2)
    w_lat_t = row_slab_bf16(w_in_t, i, n_attn, n_lat + lat_pad)
    qkv = matmul([(hn, w_in_t, i, 0)], n_attn, BF16, tm=2048, tn=512, name="odd_in_proj", w_t=True,
                 a_single_buffer=True)
    lat = matmul([(hn, w_lat_t, 0, 0)], n_lat + lat_pad, F32, tm=512, tn=n_lat + lat_pad,
                 name="odd_lat_proj", w_t=True)
    c = chunk_attention(qkv, rel_bias, batch, seq, n_ch, 0, cb, 2 * cb)

    tables = rope_tables(seq)
    hw = QK_NOPE + LANES
    wq = w_uq.reshape(q_lora, n_mla, QK_NOPE + QK_ROPE)
    wq = jnp.pad(wq, ((0, 0), (0, 0), (0, hw - QK_NOPE - QK_ROPE))).reshape(q_lora, n_mla * hw).astype(BF16)
    q = mla_q(lat, q_norm_g, wq, tables, seq, n_mla)
    assert q_lora % kv_lora == 0 and (q_lora + kv_lora) % LANES == 0
    k, v = mla_kv(lat, kv_norm_g, w_ukv.astype(BF16), tables, seq, n_mla,
                  q_lora // kv_lora, (q_lora + kv_lora) // LANES)
    dm = mla_attention(q, k, v, batch, seq, n_mla)
    return matmul([(c, w_out, i, 0), (dm, w_out, i, 1)], d_model, F32, tm=1024, tn=512,
                  name="mix_out_proj")


def _ffn(h, w_in, conv_w, conv_b, w_out, layer, seq):
    g, w_out16 = ffn_in(h, w_in, conv_w, conv_b, w_out, layer, seq)
    return matmul([(g, w_out16[None], 0, 0)], w_out.shape[2], F32, tm=512, tn=256,
                  name="ffn_out_proj")


def kernel(x, norm_g, even_w_in, even_w_out, diff_lambda, diff_subln_g, odd_w_in, odd_w_out,
           ch_rel_bias, mla_q_norm_g, mla_w_uq, mla_kv_norm_g, mla_w_ukv, ffn_w_in, ffn_conv_w,
           ffn_conv_b, ffn_w_out):
    batch, seq, d_model = x.shape
    depth = norm_g.shape[0]
    xf = x.reshape(batch * seq, d_model)
    hn = norm_cast(xf, norm_g[0, 0])
    for layer in range(depth):
        g = norm_g[layer]
        i = layer // 2
        if layer % 2 == 0:
            mix = _even_mixer(hn, even_w_in, even_w_out, i, diff_lambda[i], diff_subln_g[i],
                              batch, seq, layer)
        else:
            mix = _odd_mixer(hn, odd_w_in, odd_w_out, i, ch_rel_bias[i], mla_q_norm_g[i],
                             mla_w_uq[i], mla_kv_norm_g[i], mla_w_ukv[i], batch, seq)
        xf, h2 = resid_norm(xf, mix, g[1], g[2])
        f = _ffn(h2, ffn_w_in, ffn_conv_w, ffn_conv_b, ffn_w_out, layer, seq)
        if layer + 1 < depth:
            xf, hn = resid_norm(xf, f, g[3], norm_g[layer + 1, 0])
        else:
            xf = resid(xf, f, g[3])
    return xf.reshape(batch, seq, d_model)
```

```python
import functools
import math

import jax
import jax.numpy as jnp
from jax import lax
from jax.experimental import pallas as pl
from jax.experimental.pallas import tpu as pltpu

F32 = jnp.float32
BF16 = jnp.bfloat16

CHUNK = 64
HEAD_DIM = 128
LEFT_CHUNKS = 8
REL_CLIP = 128
QK_NOPE = 128
QK_ROPE = 64
V_MLA = 128
ROPE_THETA = 10000.0
CONV_W = 3
EPS = 1e-6
NEG = -1e30
LOG2E = math.log2(math.e)

LANES = 128
BF16_SUBLANES = 16
VMEM_LIMIT = 52 * 1024 * 1024

ATTN_TILE = 256
SB_HEADS_PER_STEP = 8
MLA_HEADS_PER_STEP = 8
DIFF_HEADS_PER_STEP = 4
CHUNK_HEADS_PER_STEP = 4
MLA_PROJ_HEADS_PER_STEP = 4


def _params(*sem):
    return pltpu.CompilerParams(dimension_semantics=sem, vmem_limit_bytes=VMEM_LIMIT)


def _rms(x, g):
    ms = jnp.mean(x * x, axis=-1, keepdims=True)
    return x * lax.rsqrt(ms + EPS) * g


def _dot(a, b):
    return jnp.dot(a, b, preferred_element_type=F32)


def _dot_nt(a, b):
    return lax.dot_general(a, b, (((1,), (1,)), ((), ())), preferred_element_type=F32)


def _norm_cast_kernel(x_ref, g_ref, h_ref):
    h_ref[...] = _rms(x_ref[...], g_ref[...]).astype(h_ref.dtype)


def _resid_norm_kernel(x_ref, y_ref, g1_ref, g2_ref, xo_ref, h_ref):
    xn = x_ref[...] + _rms(y_ref[...], g1_ref[...])
    xo_ref[...] = xn
    h_ref[...] = _rms(xn, g2_ref[...]).astype(h_ref.dtype)


def _resid_kernel(x_ref, y_ref, g_ref, xo_ref):
    xo_ref[...] = x_ref[...] + _rms(y_ref[...], g_ref[...])


def _row_spec(tr, d):
    return pl.BlockSpec((tr, d), lambda i: (i, 0))


def _vec_spec(d):
    return pl.BlockSpec((1, d), lambda i: (0, 0))


def norm_cast(x, g, tr=256):
    t, d = x.shape
    return pl.pallas_call(
        _norm_cast_kernel,
        grid=(t // tr,),
        in_specs=[_row_spec(tr, d), _vec_spec(d)],
        out_specs=_row_spec(tr, d),
        out_shape=jax.ShapeDtypeStruct((t, d), BF16),
        compiler_params=_params("parallel"),
        name="norm_cast",
    )(x, g.reshape(1, d))


def resid_norm(x, y, g1, g2, tr=256):
    t, d = x.shape
    return pl.pallas_call(
        _resid_norm_kernel,
        grid=(t // tr,),
        in_specs=[_row_spec(tr, d), _row_spec(tr, d), _vec_spec(d), _vec_spec(d)],
        out_specs=[_row_spec(tr, d), _row_spec(tr, d)],
        out_shape=[jax.ShapeDtypeStruct((t, d), F32), jax.ShapeDtypeStruct((t, d), BF16)],
        compiler_params=_params("parallel"),
        name="resid_norm",
    )(x, y, g1.reshape(1, d), g2.reshape(1, d))


def resid(x, y, g, tr=256):
    t, d = x.shape
    return pl.pallas_call(
        _resid_kernel,
        grid=(t // tr,),
        in_specs=[_row_spec(tr, d), _row_spec(tr, d), _vec_spec(d)],
        out_specs=_row_spec(tr, d),
        out_shape=jax.ShapeDtypeStruct((t, d), F32),
        compiler_params=_params("parallel"),
        name="resid",
    )(x, y, g.reshape(1, d))


def _matmul_kernel(*refs, n_pairs, w_t):
    o_ref = refs[2 * n_pairs]
    dot = _dot_nt if w_t else _dot
    acc = dot(refs[0][...], refs[n_pairs][...].astype(BF16))
    for p in range(1, n_pairs):
        acc = acc + dot(refs[p][...], refs[n_pairs + p][...].astype(BF16))
    o_ref[...] = acc.astype(o_ref.dtype)


def matmul(pairs, n, out_dtype, tm, tn, name, w_t=False, a_single_buffer=False):
    m = pairs[0][0].shape[0]
    n_pairs = len(pairs)
    a_specs, w_specs, args_a, args_w = [], [], [], []
    for a, w, layer, rb in pairs:
        k = a.shape[1]
        mode = dict(pipeline_mode=pl.Buffered(1)) if a_single_buffer else {}
        a_specs.append(pl.BlockSpec((tm, k), lambda i, j: (i, 0), **mode))
        if w_t:
            w_specs.append(pl.BlockSpec((None, tn, k), lambda i, j, layer=layer, rb=rb: (layer, j, rb)))
        else:
            w_specs.append(pl.BlockSpec((None, k, tn), lambda i, j, layer=layer, rb=rb: (layer, rb, j)))
        args_a.append(a)
        args_w.append(w)
    return pl.pallas_call(
        functools.partial(_matmul_kernel, n_pairs=n_pairs, w_t=w_t),
        grid=(m // tm, n // tn),
        in_specs=a_specs + w_specs,
        out_specs=pl.BlockSpec((tm, tn), lambda i, j: (i, j)),
        out_shape=jax.ShapeDtypeStruct((m, n), out_dtype),
        compiler_params=_params("parallel", "arbitrary"),
        name=name,
    )(*args_a, *args_w)


def _row_slab_kernel(w_ref, o_ref, *, row0, n_rows):
    row = row0 + pl.program_id(0) * LANES + lax.broadcasted_iota(jnp.int32, w_ref.shape, 0)
    o_ref[...] = jnp.where(row < n_rows, w_ref[...], 0.0).astype(o_ref.dtype)


def row_slab_bf16(w_t, layer, row0, height):
    _, n_rows, k = w_t.shape
    assert row0 % LANES == 0 and height % LANES == 0
    return pl.pallas_call(
        functools.partial(_row_slab_kernel, row0=row0, n_rows=n_rows),
        grid=(height // LANES,),
        in_specs=[pl.BlockSpec((None, LANES, k), lambda j: (layer, row0 // LANES + j, 0))],
        out_specs=pl.BlockSpec((None, LANES, k), lambda j: (0, j, 0)),
        out_shape=jax.ShapeDtypeStruct((1, height, k), BF16),
        compiler_params=_params("parallel"),
        name="row_slab_bf16",
    )(w_t)


def _gelu_tanh(x):
    c = math.sqrt(2.0 / math.pi)
    return x * (0.5 * (1.0 + jnp.tanh(c * (x + 0.044715 * (x * x * x)))))


def _ffn_in_kernel(a_ref, ah_ref, wg_ref, wv_ref, cwg_ref, cwv_ref, cbg_ref, cbv_ref, wo_ref,
                   o_ref, wo16_ref, *, tm, seq):
    i = pl.program_id(0)
    wo16_ref[...] = wo_ref[...].astype(wo16_ref.dtype)
    tn = o_ref.shape[1]
    a = a_ref[...]
    ah = ah_ref[...]
    seq_start = (i * tm) % seq == 0
    row8 = lax.broadcasted_iota(jnp.int32, (8, tn), 0)

    def conv(w_ref, cw_ref, cb_ref):
        w = w_ref[...].astype(BF16)
        u = _dot(a, w)
        uh = _dot(ah, w)
        uh = jnp.where(seq_start, 0.0, uh)
        p1 = uh[BF16_SUBLANES - 1:BF16_SUBLANES, :]
        p2 = uh[BF16_SUBLANES - 2:BF16_SUBLANES - 1, :]
        r1 = pltpu.roll(u, 1, 0)
        r2 = pltpu.roll(u, 2, 0)
        h1 = jnp.where(row8 == 0, p1, r1[:8, :])
        h2 = jnp.where(row8 == 0, p2, jnp.where(row8 == 1, p1, r2[:8, :]))
        u1 = jnp.concatenate([h1, r1[8:, :]], axis=0)
        u2 = jnp.concatenate([h2, r2[8:, :]], axis=0)
        cw = cw_ref[...]
        return cb_ref[...] + (cw[0:1, :] * u2 + cw[1:2, :] * u1 + cw[2:3, :] * u)

    gate = conv(wg_ref, cwg_ref, cbg_ref)
    val = conv(wv_ref, cwv_ref, cbv_ref)
    o_ref[...] = (_gelu_tanh(gate) * val).astype(o_ref.dtype)


def ffn_in(h, w_in, conv_w, conv_b, w_out, layer, seq, tm=2048, tn=256):
    t, k = h.shape
    f = w_in.shape[2] // 2
    d_out = w_out.shape[2]
    nf = f // tn
    n_steps = (t // tm) * nf
    assert f % (n_steps * BF16_SUBLANES) == 0
    slab = f // n_steps
    halo = BF16_SUBLANES
    hb = tm // halo
    conv_b = conv_b.reshape(conv_b.shape[0], 1, 2 * f)
    return pl.pallas_call(
        functools.partial(_ffn_in_kernel, tm=tm, seq=seq),
        grid=(t // tm, nf),
        in_specs=[
            pl.BlockSpec((tm, k), lambda i, j: (i, 0), pipeline_mode=pl.Buffered(1)),
            pl.BlockSpec((halo, k), lambda i, j: (jnp.maximum(i * hb - 1, 0), 0)),
            pl.BlockSpec((None, k, tn), lambda i, j: (layer, 0, j)),
            pl.BlockSpec((None, k, tn), lambda i, j: (layer, 0, j + nf)),
            pl.BlockSpec((None, CONV_W, tn), lambda i, j: (layer, 0, j)),
            pl.BlockSpec((None, CONV_W, tn), lambda i, j: (layer, 0, j + nf)),
            pl.BlockSpec((None, 1, tn), lambda i, j: (layer, 0, j)),
            pl.BlockSpec((None, 1, tn), lambda i, j: (layer, 0, j + nf)),
            pl.BlockSpec((None, slab, d_out), lambda i, j: (layer, i * nf + j, 0)),
        ],
        out_specs=[pl.BlockSpec((tm, tn), lambda i, j: (i, j)),
                   pl.BlockSpec((slab, d_out), lambda i, j: (i * nf + j, 0))],
        out_shape=[jax.ShapeDtypeStruct((t, f), BF16),
                   jax.ShapeDtypeStruct((f, d_out), BF16)],
        compiler_params=_params("parallel", "arbitrary"),
        name="ffn_in",
    )(h, h, w_in, w_in, conv_w, conv_w, conv_b, conv_b, w_out)


def _transpose_value_tiles(v_ref, vt_scr, tq):
    def transpose_tile(kb, c):
        start = pl.multiple_of(kb * tq, tq)
        vt_scr[kb] = v_ref[pl.ds(start, tq), :].astype(F32).T.astype(vt_scr.dtype)
        return c
    lax.fori_loop(0, vt_scr.shape[0], transpose_tile, 0)


def _sb_kernel(q_ref, k_ref, v_ref, o_ref, vt_scr, *, tq, scale, g):
    qi = pl.program_id(2)
    d = HEAD_DIM
    heads = [slice(hh * d, (hh + 1) * d) for hh in range(g)]
    qs = [q_ref[:, cs] for cs in heads]

    @pl.when(qi == 0)
    def _():
        _transpose_value_tiles(v_ref, vt_scr, tq)

    key = lax.broadcasted_iota(jnp.int32, (tq, tq), 0)
    qry = lax.broadcasted_iota(jnp.int32, (tq, tq), 1)
    strict = key < qry
    r2 = lax.broadcasted_iota(jnp.int32, (tq, 2 * tq), 0)
    c2 = lax.broadcasted_iota(jnp.int32, (tq, 2 * tq), 1)
    tri = (jnp.where(c2 >= tq, c2 - tq, c2) >= r2).astype(BF16)

    def block(kb, carry, diag):
        start = pl.multiple_of(kb * tq, tq)
        zs = [_dot_nt(k_ref[pl.ds(start, tq), cs], q) * (scale * LOG2E) for q, cs in zip(qs, heads)]
        csums = []
        for z in zs:
            neg_abs = lax.bitcast_convert_type(
                lax.bitcast_convert_type(z, jnp.uint32) | jnp.uint32(0x80000000), F32)
            sp = jnp.maximum(z, 0.0) + jnp.log2(1.0 + jnp.exp2(neg_abs))
            if diag:
                sp = jnp.where(strict, sp, 0.0)
            hi = sp.astype(BF16)
            lo = (sp - hi.astype(F32)).astype(BF16)
            csums.append(_dot(tri, jnp.concatenate([hi, lo], axis=0)))
        out = []
        for z, csum, cs, (rsum, acc) in zip(zs, csums, heads, carry):
            w = jnp.exp2(z - csum - rsum)
            if diag:
                w = jnp.where(strict, w, 0.0)
            acc = acc + _dot(vt_scr[kb, cs, :], w.astype(BF16))
            rsum = rsum + csum[0:1, :]
            out.append((rsum, acc))
        return tuple(out)

    init = tuple((jnp.zeros((1, tq), F32), jnp.zeros((d, tq), F32)) for _ in range(g))
    carry = block(qi, init, True)

    def body(it, carry):
        return block(qi - 1 - it, carry, False)

    carry = lax.fori_loop(0, qi, body, carry)
    for cs, (_, acc) in zip(heads, carry):
        o_ref[:, cs] = acc.T.astype(o_ref.dtype)


def sb_attention(qkv, batch, seq, n_heads, q_col, k_col, v_col, g=SB_HEADS_PER_STEP):
    tq = ATTN_TILE
    nq = seq // tq
    d = HEAD_DIM
    assert n_heads % g == 0 and q_col % g == 0 and k_col % g == 0 and v_col % g == 0
    return pl.pallas_call(
        functools.partial(_sb_kernel, tq=tq, scale=1.0 / math.sqrt(d), g=g),
        grid=(batch, n_heads // g, nq),
        in_specs=[
            pl.BlockSpec((tq, g * d), lambda b, h, i: (b * nq + i, q_col // g + h)),
            pl.BlockSpec((seq, g * d), lambda b, h, i: (b, k_col // g + h)),
            pl.BlockSpec((seq, g * d), lambda b, h, i: (b, v_col // g + h)),
        ],
        out_specs=pl.BlockSpec((tq, g * d), lambda b, h, i: (b * nq + i, h)),
        out_shape=jax.ShapeDtypeStruct((batch * seq, n_heads * d), BF16),
        scratch_shapes=[pltpu.VMEM((nq, g * d, tq), BF16)],
        compiler_params=_params("parallel", "parallel", "arbitrary"),
        name="sb_attention",
    )(qkv, qkv, qkv)


def _online_step_t(s, vt, m, l, acc):
    m_new = jnp.maximum(m, jnp.max(s, axis=0, keepdims=True))
    alpha = jnp.exp2(m - m_new)
    p = jnp.exp2(s - m_new)
    l = alpha * l + jnp.sum(p, axis=0, keepdims=True)
    acc = alpha * acc + _dot(vt, p.astype(BF16))
    return m_new, l, acc


def _key_major_tile(tq):
    key = lax.broadcasted_iota(jnp.int32, (tq, tq), 0)
    qry = lax.broadcasted_iota(jnp.int32, (tq, tq), 1)
    return key, qry, (key // CHUNK) <= (qry // CHUNK)


def _diff_kernel(q_ref, k_ref, v_ref, lam_ref, g_ref, o_ref, vt_scr, *, tq, scale, lam_init, g):
    hg = pl.program_id(1)
    qi = pl.program_id(2)
    d = HEAD_DIM
    dv = 2 * HEAD_DIM
    qk_cols = [slice(c * d, (c + 1) * d) for c in range(2 * g)]
    v_cols = [slice(hh * dv, (hh + 1) * dv) for hh in range(g)]
    qs = [q_ref[:, cs] for cs in qk_cols]

    @pl.when(qi == 0)
    def _():
        _transpose_value_tiles(v_ref, vt_scr, tq)

    key, qry, visible = _key_major_tile(tq)
    dist = (qry - key).astype(F32)
    nbias_diag, nbias_off, slopes = [], [], []
    for hh in range(g):
        slope = lax.bitcast_convert_type(
            jnp.full((1, 1), (126 - (hg * g + hh)) * (1 << 23), jnp.int32), F32)
        slope = slope * LOG2E
        slopes.append(slope)
        nbias_diag.append(-slope * jnp.abs(dist))
        nbias_off.append(-slope * dist)

    def block(kb, carry, diag):
        start = pl.multiple_of(kb * tq, tq)
        ss = [_dot_nt(k_ref[pl.ds(start, tq), cs], q) * (scale * LOG2E) for q, cs in zip(qs, qk_cols)]
        for c in range(2 * g):
            hh = c // 2
            if diag:
                ss[c] = jnp.where(visible, ss[c] + nbias_diag[hh], NEG)
            else:
                ss[c] = ss[c] + (nbias_off[hh] - slopes[hh] * ((qi - kb) * tq).astype(F32))
        return tuple(_online_step_t(ss[c], vt_scr[kb, v_cols[c // 2], :], *carry[c])
                     for c in range(2 * g))

    def body(kb, carry):
        return block(kb, carry, False)

    init = tuple((jnp.full((1, tq), NEG, F32), jnp.zeros((1, tq), F32), jnp.zeros((dv, tq), F32))
                 for _ in range(2 * g))
    carry = lax.fori_loop(0, qi, body, init)
    carry = block(qi, carry, True)

    lf = lam_ref[...]
    lam = (jnp.exp(jnp.sum(lf[0:1, :] * lf[1:2, :], axis=-1, keepdims=True))
           - jnp.exp(jnp.sum(lf[2:3, :] * lf[3:4, :], axis=-1, keepdims=True)) + lam_init)
    for hh in range(g):
        (_, l1, a1), (_, l2, a2) = carry[2 * hh], carry[2 * hh + 1]
        o = (a1 / l1 - lam * (a2 / l2)).T
        o_ref[:, v_cols[hh]] = (_rms(o, g_ref[...]) * (1.0 - lam_init)).astype(o_ref.dtype)


def diff_attention(qkv, diff_lambda, subln_g, batch, seq, n_heads, q_col, k_col, v_col, lam_init,
                   g=DIFF_HEADS_PER_STEP):
    tq = ATTN_TILE
    nq = seq // tq
    dv = 2 * HEAD_DIM
    assert n_heads == 8, "ALiBi slopes are built as exact powers of two"
    assert n_heads % g == 0 and q_col % g == 0 and k_col % g == 0 and v_col % g == 0
    return pl.pallas_call(
        functools.partial(_diff_kernel, tq=tq, scale=1.0 / math.sqrt(HEAD_DIM), lam_init=lam_init, g=g),
        grid=(batch, n_heads // g, nq),
        in_specs=[
            pl.BlockSpec((tq, g * dv), lambda b, h, i: (b * nq + i, q_col // g + h)),
            pl.BlockSpec((seq, g * dv), lambda b, h, i: (b, k_col // g + h)),
            pl.BlockSpec((seq, g * dv), lambda b, h, i: (b, v_col // g + h)),
            pl.BlockSpec((4, HEAD_DIM), lambda b, h, i: (0, 0)),
            pl.BlockSpec((1, dv), lambda b, h, i: (0, 0)),
        ],
        out_specs=pl.BlockSpec((tq, g * dv), lambda b, h, i: (b * nq + i, h)),
        out_shape=jax.ShapeDtypeStruct((batch * seq, n_heads * dv), BF16),
        scratch_shapes=[pltpu.VMEM((nq, g * dv, tq), BF16)],
        compiler_params=_params("parallel", "parallel", "arbitrary"),
        name="diff_attention",
    )(qkv, qkv, qkv, diff_lambda, subln_g.reshape(1, dv))


def _chunk_kernel(q_ref, k_ref, v_ref, tab_ref, o_ref, *, tq, scale, n_win, g):
    qi = pl.program_id(2)
    d = HEAD_DIM
    heads = [slice(hh * d, (hh + 1) * d) for hh in range(g)]
    kbs = [qi - (n_win - 1) + w for w in range(n_win)]
    starts = [pl.multiple_of(jnp.maximum(kb, 0) * tq, tq) for kb in kbs]
    scores = []
    for hh, cs in enumerate(heads):
        q = q_ref[:, cs]
        s_blocks = []
        for w in range(n_win):
            s = (_dot_nt(q, k_ref[pl.ds(starts[w], tq), cs]) * (scale * LOG2E)
                 + tab_ref[hh, :, w * tq:(w + 1) * tq])
            if w < n_win - 1:
                s = jnp.where(kbs[w] >= 0, s, NEG)
            s_blocks.append(s)
        scores.append(s_blocks)
    for cs, s_blocks in zip(heads, scores):
        m = s_blocks[0].max(axis=-1, keepdims=True)
        for s in s_blocks[1:]:
            m = jnp.maximum(m, s.max(axis=-1, keepdims=True))
        l = jnp.zeros_like(m)
        acc = jnp.zeros((tq, d), F32)
        for w, s in enumerate(s_blocks):
            p = jnp.exp2(s - m)
            l = l + jnp.sum(p, axis=-1, keepdims=True)
            acc = acc + _dot(p.astype(BF16), v_ref[pl.ds(starts[w], tq), cs])
        o_ref[:, cs] = (acc / l).astype(o_ref.dtype)


def chunk_bias_table(rel_bias, tq):
    left = LEFT_CHUNKS * CHUNK
    n_heads = rel_bias.shape[0]
    width = tq + left
    period = 2 * left
    assert left >= REL_CLIP and tq <= left
    rb = rel_bias.astype(F32)
    hi = jnp.broadcast_to(rb[:, -1:], (n_heads, left - REL_CLIP))
    mid = rb[:, ::-1]
    lo = jnp.broadcast_to(rb[:, :1], (n_heads, tq - REL_CLIP - 1))
    neg_d = jnp.broadcast_to(rb[:, -1:], (n_heads, period - width))
    diag = jnp.concatenate([hi, mid, lo, neg_d], axis=1)
    flat = jnp.tile(diag, (1, tq))[:, :tq * (period - 1)]
    bias = flat.reshape(n_heads, tq, period - 1)[:, :, :width]
    i = jnp.arange(tq)[:, None]
    j = jnp.arange(width)[None, :]
    qc, kc = i // CHUNK, j // CHUNK
    allowed = (kc >= qc) & (kc <= qc + LEFT_CHUNKS)
    return jnp.where(allowed[None], bias * LOG2E, NEG)


def chunk_attention(qkv, rel_bias, batch, seq, n_heads, q_col, k_col, v_col, g=CHUNK_HEADS_PER_STEP):
    tq = ATTN_TILE
    nq = seq // tq
    d = HEAD_DIM
    left = LEFT_CHUNKS * CHUNK
    assert left % tq == 0
    assert n_heads % g == 0 and q_col % g == 0 and k_col % g == 0 and v_col % g == 0
    n_win = left // tq + 1
    table = chunk_bias_table(rel_bias, tq)
    return pl.pallas_call(
        functools.partial(_chunk_kernel, tq=tq, scale=1.0 / math.sqrt(d), n_win=n_win, g=g),
        grid=(n_heads // g, batch, nq),
        in_specs=[
            pl.BlockSpec((tq, g * d), lambda h, b, i: (b * nq + i, q_col // g + h)),
            pl.BlockSpec((seq, g * d), lambda h, b, i: (b, k_col // g + h)),
            pl.BlockSpec((seq, g * d), lambda h, b, i: (b, v_col // g + h)),
            pl.BlockSpec((g, tq, tq + left), lambda h, b, i: (h, 0, 0)),
        ],
        out_specs=pl.BlockSpec((tq, g * d), lambda h, b, i: (b * nq + i, h)),
        out_shape=jax.ShapeDtypeStruct((batch * seq, n_heads * d), BF16),
        compiler_params=_params("parallel", "parallel", "arbitrary"),
        name="chunk_attention",
    )(qkv, qkv, qkv, table)


def rope_tables(seq):
    half = QK_ROPE // 2
    pos = jnp.arange(seq, dtype=F32)
    inv_freq = ROPE_THETA ** (-jnp.arange(0, QK_ROPE, 2, dtype=F32) / QK_ROPE)
    ang = pos[:, None] * inv_freq[None, :]
    cos, sin = jnp.cos(ang), jnp.sin(ang)
    z = lambda n: jnp.zeros((seq, n), F32)
    cos_t = jnp.concatenate([cos, cos, z(LANES - 2 * half)], axis=1)
    sin_a = jnp.concatenate([-sin, z(LANES - half)], axis=1)
    sin_b = jnp.concatenate([z(half), sin, z(LANES - 2 * half)], axis=1)
    return cos_t, sin_a, sin_b


def _rope(x, cos_t, sin_a, sin_b):
    half = QK_ROPE // 2
    return x * cos_t + pltpu.roll(x, LANES - half, 1) * sin_a + pltpu.roll(x, half, 1) * sin_b


def _mla_q_kernel(c_ref, g_ref, w_ref, cos_ref, sa_ref, sb_ref, o_ref, an_ref, *, hp):
    @pl.when(pl.program_id(1) == 0)
    def _():
        an_ref[...] = _rms(c_ref[...], g_ref[...]).astype(an_ref.dtype)

    hw = QK_NOPE + LANES
    res = _dot(an_ref[...], w_ref[...])
    cos_t, sin_a, sin_b = cos_ref[...], sa_ref[...], sb_ref[...]
    for hh in range(hp):
        c0 = hh * hw
        o_ref[:, c0:c0 + QK_NOPE] = res[:, c0:c0 + QK_NOPE].astype(o_ref.dtype)
        o_ref[:, c0 + QK_NOPE:c0 + hw] = _rope(res[:, c0 + QK_NOPE:c0 + hw],
                                               cos_t, sin_a, sin_b).astype(o_ref.dtype)


def mla_q(lat, g, w_q, tables, seq, n_heads, tm=1024, hp=MLA_PROJ_HEADS_PER_STEP):
    t = lat.shape[0]
    kq = g.shape[0]
    hw = QK_NOPE + LANES
    npos = seq // tm
    tab_spec = pl.BlockSpec((tm, LANES), lambda i, j: (i % npos, 0))
    return pl.pallas_call(
        functools.partial(_mla_q_kernel, hp=hp),
        grid=(t // tm, n_heads // hp),
        in_specs=[
            pl.BlockSpec((tm, kq), lambda i, j: (i, 0)),
            pl.BlockSpec((1, kq), lambda i, j: (0, 0)),
            pl.BlockSpec((kq, hp * hw), lambda i, j: (0, j)),
            tab_spec, tab_spec, tab_spec,
        ],
        out_specs=pl.BlockSpec((tm, hp * hw), lambda i, j: (i, j)),
        out_shape=jax.ShapeDtypeStruct((t, n_heads * hw), BF16),
        scratch_shapes=[pltpu.VMEM((tm, kq), BF16)],
        compiler_params=_params("parallel", "arbitrary"),
        name="mla_q",
    )(lat, g.reshape(1, kq), w_q, *tables)


def _mla_kv_kernel(c_ref, kr_ref, g_ref, w_ref, cos_ref, sa_ref, sb_ref, k_ref, vt_ref,
                   an_ref, kr_scr, *, hp, tm, tk):
    @pl.when(pl.program_id(1) == 0)
    def _():
        an_ref[...] = _rms(c_ref[...], g_ref[...]).astype(an_ref.dtype)
        kr_scr[...] = _rope(kr_ref[...], cos_ref[...], sa_ref[...], sb_ref[...]).astype(kr_scr.dtype)

    hw = QK_NOPE + LANES
    wv = QK_NOPE + V_MLA
    res = _dot(an_ref[...], w_ref[...])
    kr = kr_scr[...]
    for hh in range(hp):
        k_ref[:, hh * hw:hh * hw + QK_NOPE] = res[:, hh * wv:hh * wv + QK_NOPE].astype(k_ref.dtype)
        k_ref[:, hh * hw + QK_NOPE:(hh + 1) * hw] = kr
        v = res[:, hh * wv + QK_NOPE:(hh + 1) * wv]
        for kt in range(tm // tk):
            vt_ref[kt, hh * V_MLA:(hh + 1) * V_MLA, :] = v[kt * tk:(kt + 1) * tk, :].T.astype(vt_ref.dtype)


def mla_kv(lat, g, w_kv, tables, seq, n_heads, ckv_col, kr_col, tm=1024, hp=MLA_PROJ_HEADS_PER_STEP):
    t = lat.shape[0]
    kkv = g.shape[0]
    hw = QK_NOPE + LANES
    tk = ATTN_TILE
    assert tm % tk == 0
    npos = seq // tm
    tab_spec = pl.BlockSpec((tm, LANES), lambda i, j: (i % npos, 0))
    return pl.pallas_call(
        functools.partial(_mla_kv_kernel, hp=hp, tm=tm, tk=tk),
        grid=(t // tm, n_heads // hp),
        in_specs=[
            pl.BlockSpec((tm, kkv), lambda i, j: (i, ckv_col)),
            pl.BlockSpec((tm, LANES), lambda i, j: (i, kr_col)),
            pl.BlockSpec((1, kkv), lambda i, j: (0, 0)),
            pl.BlockSpec((kkv, hp * (QK_NOPE + V_MLA)), lambda i, j: (0, j)),
            tab_spec, tab_spec, tab_spec,
        ],
        out_specs=[pl.BlockSpec((tm, hp * hw), lambda i, j: (i, j)),
                   pl.BlockSpec((tm // tk, hp * V_MLA, tk), lambda i, j: (i, j, 0))],
        out_shape=[jax.ShapeDtypeStruct((t, n_heads * hw), BF16),
                   jax.ShapeDtypeStruct((t // tk, n_heads * V_MLA, tk), BF16)],
        scratch_shapes=[pltpu.VMEM((tm, kkv), BF16), pltpu.VMEM((tm, LANES), BF16)],
        compiler_params=_params("parallel", "arbitrary"),
        name="mla_kv",
    )(lat, lat, g.reshape(1, kkv), w_kv, *tables)


def _mla_attn_kernel(q_ref, k_ref, vt_ref, o_ref, *, tq, scale, g):
    qi = pl.program_id(2)
    hw = QK_NOPE + LANES
    dv = V_MLA
    qk_cols = [slice(hh * hw, (hh + 1) * hw) for hh in range(g)]
    v_cols = [slice(hh * dv, (hh + 1) * dv) for hh in range(g)]
    qs = [q_ref[:, cs] for cs in qk_cols]
    _, _, visible = _key_major_tile(tq)

    def scores(kb, diag):
        start = pl.multiple_of(kb * tq, tq)
        ss = [_dot_nt(k_ref[pl.ds(start, tq), cs], q) * (scale * LOG2E) for q, cs in zip(qs, qk_cols)]
        if diag:
            ss = [jnp.where(visible, s, NEG) for s in ss]
        return ss

    def update(kb, ss, carry):
        return tuple(_online_step_t(s, vt_ref[kb, vs, :], *st)
                     for s, vs, st in zip(ss, v_cols, carry))

    def pair(kb, carry, diag_second):
        ss_a = scores(kb, False)
        ss_b = scores(kb + 1, diag_second)
        return update(kb + 1, ss_b, update(kb, ss_a, carry))

    init = tuple((jnp.full((1, tq), NEG, F32), jnp.zeros((1, tq), F32), jnp.zeros((dv, tq), F32))
                 for _ in range(g))
    carry = lax.fori_loop(0, qi // 2, lambda p, c: pair(2 * p, c, False), init)
    carry = lax.cond(qi % 2 == 1,
                     lambda c: pair(qi - 1, c, True),
                     lambda c: update(qi, scores(qi, True), c),
                     carry)
    for vs, (_, l, acc) in zip(v_cols, carry):
        o_ref[:, vs] = (acc / l).T.astype(o_ref.dtype)


def mla_attention(q, k, vt, batch, seq, n_heads, g=MLA_HEADS_PER_STEP):
    tq = ATTN_TILE
    nq = seq // tq
    hw = QK_NOPE + LANES
    assert n_heads % g == 0
    return pl.pallas_call(
        functools.partial(_mla_attn_kernel, tq=tq, scale=1.0 / math.sqrt(QK_NOPE + QK_ROPE), g=g),
        grid=(batch, n_heads // g, nq),
        in_specs=[
            pl.BlockSpec((tq, g * hw), lambda b, h, i: (b * nq + i, h)),
            pl.BlockSpec((seq, g * hw), lambda b, h, i: (b, h), pipeline_mode=pl.Buffered(1)),
            pl.BlockSpec((nq, g * V_MLA, tq), lambda b, h, i: (b, h, 0), pipeline_mode=pl.Buffered(1)),
        ],
        out_specs=pl.BlockSpec((tq, g * V_MLA), lambda b, h, i: (b * nq + i, h)),
        out_shape=jax.ShapeDtypeStruct((batch * seq, n_heads * V_MLA), BF16),
        compiler_params=_params("parallel", "parallel", "arbitrary"),
        name="mla_attention",
    )(q, k, vt)


def _even_mixer(hn, w_in, w_out, i, diff_lambda, subln_g, batch, seq, layer):
    d_model = hn.shape[1]
    w_sb = d_model // 2
    n_sb = w_sb // HEAD_DIM
    n_diff = w_sb // (2 * HEAD_DIM)
    cb = w_sb // LANES
    qkv = matmul([(hn, w_in, i, 0)], w_in.shape[2], BF16, tm=2048, tn=512, name="even_in_proj",
                 a_single_buffer=True)
    a = sb_attention(qkv, batch, seq, n_sb, 0, cb, 2 * cb)
    lam_init = 0.8 - 0.6 * math.exp(-0.3 * layer)
    bo = diff_attention(qkv, diff_lambda, subln_g, batch, seq, n_diff,
                        3 * cb // 2, 4 * cb // 2, 5 * cb // 2, lam_init)
    return matmul([(a, w_out, i, 0), (bo, w_out, i, 1)], d_model, F32, tm=1024, tn=512,
                  name="mix_out_proj")


def _odd_mixer(hn, w_in, w_out, i, rel_bias, q_norm_g, w_uq, kv_norm_g, w_ukv, batch, seq):
    d_model = hn.shape[1]
    w_ch = d_model // 2
    n_ch = w_ch // HEAD_DIM
    n_mla = w_ch // HEAD_DIM
    cb = w_ch // LANES
    q_lora, kv_lora = q_norm_g.shape[0], kv_norm_g.shape[0]
    n_attn = 3 * w_ch
    n_lat = q_lora + kv_lora + QK_ROPE
    lat_pad = -n_lat % LANES
    w_in_t = jnp.swapaxes(w_in, 1, 2)
    w_lat_t = row_slab_bf16(w_in_t, i, n_attn, n_lat + lat_pad)
    qkv = matmul([(hn, w_in_t, i, 0)], n_attn, BF16, tm=2048, tn=512, name="odd_in_proj", w_t=True,
                 a_single_buffer=True)
    lat = matmul([(hn, w_lat_t, 0, 0)], n_lat + lat_pad, F32, tm=512, tn=n_lat + lat_pad,
                 name="odd_lat_proj", w_t=True)
    c = chunk_attention(qkv, rel_bias, batch, seq, n_ch, 0, cb, 2 * cb)

    tables = rope_tables(seq)
    hw = QK_NOPE + LANES
    wq = w_uq.reshape(q_lora, n_mla, QK_NOPE + QK_ROPE)
    wq = jnp.pad(wq, ((0, 0), (0, 0), (0, hw - QK_NOPE - QK_ROPE))).reshape(q_lora, n_mla * hw).astype(BF16)
    q = mla_q(lat, q_norm_g, wq, tables, seq, n_mla)
    assert q_lora % kv_lora == 0 and (q_lora + kv_lora) % LANES == 0
    k, v = mla_kv(lat, kv_norm_g, w_ukv.astype(BF16), tables, seq, n_mla,
                  q_lora // kv_lora, (q_lora + kv_lora) // LANES)
    dm = mla_attention(q, k, v, batch, seq, n_mla)
    return matmul([(c, w_out, i, 0), (dm, w_out, i, 1)], d_model, F32, tm=1024, tn=512,
                  name="mix_out_proj")


def _ffn(h, w_in, conv_w, conv_b, w_out, layer, seq):
    g, w_out16 = ffn_in(h, w_in, conv_w, conv_b, w_out, layer, seq)
    return matmul([(g, w_out16[None], 0, 0)], w_out.shape[2], F32, tm=512, tn=256,
                  name="ffn_out_proj")


def kernel(x, norm_g, even_w_in, even_w_out, diff_lambda, diff_subln_g, odd_w_in, odd_w_out,
           ch_rel_bias, mla_q_norm_g, mla_w_uq, mla_kv_norm_g, mla_w_ukv, ffn_w_in, ffn_conv_w,
           ffn_conv_b, ffn_w_out):
    batch, seq, d_model = x.shape
    depth = norm_g.shape[0]
    xf = x.reshape(batch * seq, d_model)
    hn = norm_cast(xf, norm_g[0, 0])
    for layer in range(depth):
        g = norm_g[layer]
        i = layer // 2
        if layer % 2 == 0:
            mix = _even_mixer(hn, even_w_in, even_w_out, i, diff_lambda[i], diff_subln_g[i],
                              batch, seq, layer)
        else:
            mix = _odd_mixer(hn, odd_w_in, odd_w_out, i, ch_rel_bias[i], mla_q_norm_g[i],
                             mla_w_uq[i], mla_kv_norm_g[i], mla_w_ukv[i], batch, seq)
        xf, h2 = resid_norm(xf, mix, g[1], g[2])
        f = _ffn(h2, ffn_w_in, ffn_conv_w, ffn_conv_b, ffn_w_out, layer, seq)
        if layer + 1 < depth:
            xf, hn = resid_norm(xf, f, g[3], norm_g[layer + 1, 0])
        else:
            xf = resid(xf, f, g[3])
    return xf.reshape(batch, seq, d_model)
```

```python
import functools
import math

import jax
import jax.numpy as jnp
from jax import lax
from jax.experimental import pallas as pl
from jax.experimental.pallas import tpu as pltpu

F32 = jnp.float32
BF16 = jnp.bfloat16

CHUNK = 64
HEAD_DIM = 128
LEFT_CHUNKS = 8
REL_CLIP = 128
QK_NOPE = 128
QK_ROPE = 64
V_MLA = 128
ROPE_THETA = 10000.0
CONV_W = 3
EPS = 1e-6
NEG = -1e30
LOG2E = math.log2(math.e)

LANES = 128
BF16_SUBLANES = 16
VMEM_LIMIT = 52 * 1024 * 1024

ATTN_TILE = 256
SB_HEADS_PER_STEP = 8
MLA_HEADS_PER_STEP = 8
DIFF_HEADS_PER_STEP = 4
CHUNK_HEADS_PER_STEP = 4
MLA_PROJ_HEADS_PER_STEP = 4


def _params(*sem):
    return pltpu.CompilerParams(dimension_semantics=sem, vmem_limit_bytes=VMEM_LIMIT)


def _rms(x, g):
    ms = jnp.mean(x * x, axis=-1, keepdims=True)
    return x * lax.rsqrt(ms + EPS) * g


def _dot(a, b):
    return jnp.dot(a, b, preferred_element_type=F32)


def _dot_nt(a, b):
    return lax.dot_general(a, b, (((1,), (1,)), ((), ())), preferred_element_type=F32)


def _norm_cast_kernel(x_ref, g_ref, h_ref):
    h_ref[...] = _rms(x_ref[...], g_ref[...]).astype(h_ref.dtype)


def _resid_norm_kernel(x_ref, y_ref, g1_ref, g2_ref, xo_ref, h_ref):
    xn = x_ref[...] + _rms(y_ref[...], g1_ref[...])
    xo_ref[...] = xn
    h_ref[...] = _rms(xn, g2_ref[...]).astype(h_ref.dtype)


def _resid_kernel(x_ref, y_ref, g_ref, xo_ref):
    xo_ref[...] = x_ref[...] + _rms(y_ref[...], g_ref[...])


def _row_spec(tr, d):
    return pl.BlockSpec((tr, d), lambda i: (i, 0))


def _vec_spec(d):
    return pl.BlockSpec((1, d), lambda i: (0, 0))


def norm_cast(x, g, tr=256):
    t, d = x.shape
    assert t % tr == 0
    return pl.pallas_call(
        _norm_cast_kernel,
        grid=(t // tr,),
        in_specs=[_row_spec(tr, d), _vec_spec(d)],
        out_specs=_row_spec(tr, d),
        out_shape=jax.ShapeDtypeStruct((t, d), BF16),
        compiler_params=_params("parallel"),
        name="norm_cast",
    )(x, g.reshape(1, d))


def resid_norm(x, y, g1, g2, tr=256):
    t, d = x.shape
    assert t % tr == 0
    return pl.pallas_call(
        _resid_norm_kernel,
        grid=(t // tr,),
        in_specs=[_row_spec(tr, d), _row_spec(tr, d), _vec_spec(d), _vec_spec(d)],
        out_specs=[_row_spec(tr, d), _row_spec(tr, d)],
        out_shape=[jax.ShapeDtypeStruct((t, d), F32), jax.ShapeDtypeStruct((t, d), BF16)],
        compiler_params=_params("parallel"),
        name="resid_norm",
    )(x, y, g1.reshape(1, d), g2.reshape(1, d))


def resid(x, y, g, tr=256):
    t, d = x.shape
    assert t % tr == 0
    return pl.pallas_call(
        _resid_kernel,
        grid=(t // tr,),
        in_specs=[_row_spec(tr, d), _row_spec(tr, d), _vec_spec(d)],
        out_specs=_row_spec(tr, d),
        out_shape=jax.ShapeDtypeStruct((t, d), F32),
        compiler_params=_params("parallel"),
        name="resid",
    )(x, y, g.reshape(1, d))


def _matmul_kernel(*refs, n_pairs, w_t):
    o_ref = refs[2 * n_pairs]
    dot = _dot_nt if w_t else _dot
    acc = dot(refs[0][...], refs[n_pairs][...].astype(BF16))
    for p in range(1, n_pairs):
        acc = acc + dot(refs[p][...], refs[n_pairs + p][...].astype(BF16))
    o_ref[...] = acc.astype(o_ref.dtype)


def matmul(pairs, n, out_dtype, tm, tn, name, w_t=False, a_single_buffer=False):
    m = pairs[0][0].shape[0]
    assert m % tm == 0 and n % tn == 0
    n_pairs = len(pairs)
    a_specs, w_specs, args_a, args_w = [], [], [], []
    for a, w, layer, rb in pairs:
        k = a.shape[1]
        mode = dict(pipeline_mode=pl.Buffered(1)) if a_single_buffer else {}
        a_specs.append(pl.BlockSpec((tm, k), lambda i, j: (i, 0), **mode))
        if w_t:
            w_specs.append(pl.BlockSpec((None, tn, k), lambda i, j, layer=layer, rb=rb: (layer, j, rb)))
        else:
            w_specs.append(pl.BlockSpec((None, k, tn), lambda i, j, layer=layer, rb=rb: (layer, rb, j)))
        args_a.append(a)
        args_w.append(w)
    return pl.pallas_call(
        functools.partial(_matmul_kernel, n_pairs=n_pairs, w_t=w_t),
        grid=(m // tm, n // tn),
        in_specs=a_specs + w_specs,
        out_specs=pl.BlockSpec((tm, tn), lambda i, j: (i, j)),
        out_shape=jax.ShapeDtypeStruct((m, n), out_dtype),
        compiler_params=_params("parallel", "arbitrary"),
        name=name,
    )(*args_a, *args_w)


def _row_slab_kernel(w_ref, o_ref, *, row0, n_rows):
    row = row0 + pl.program_id(0) * LANES + lax.broadcasted_iota(jnp.int32, w_ref.shape, 0)
    o_ref[...] = jnp.where(row < n_rows, w_ref[...], 0.0).astype(o_ref.dtype)


def row_slab_bf16(w_t, layer, row0, height):
    _, n_rows, k = w_t.shape
    assert row0 % LANES == 0 and height % LANES == 0
    return pl.pallas_call(
        functools.partial(_row_slab_kernel, row0=row0, n_rows=n_rows),
        grid=(height // LANES,),
        in_specs=[pl.BlockSpec((None, LANES, k), lambda j: (layer, row0 // LANES + j, 0))],
        out_specs=pl.BlockSpec((None, LANES, k), lambda j: (0, j, 0)),
        out_shape=jax.ShapeDtypeStruct((1, height, k), BF16),
        compiler_params=_params("parallel"),
        name="row_slab_bf16",
    )(w_t)


def _gelu_tanh(x):
    c = math.sqrt(2.0 / math.pi)
    return x * (0.5 * (1.0 + jnp.tanh(c * (x + 0.044715 * (x * x * x)))))


def _ffn_in_kernel(a_ref, ah_ref, wg_ref, wv_ref, cwg_ref, cwv_ref, cbg_ref, cbv_ref, wo_ref,
                   o_ref, wo16_ref, *, tm, seq):
    i = pl.program_id(0)
    wo16_ref[...] = wo_ref[...].astype(wo16_ref.dtype)
    tn = o_ref.shape[1]
    a = a_ref[...]
    ah = ah_ref[...]
    seq_start = (i * tm) % seq == 0
    row8 = lax.broadcasted_iota(jnp.int32, (8, tn), 0)

    def conv(w_ref, cw_ref, cb_ref):
        w = w_ref[...].astype(BF16)
        u = _dot(a, w)
        uh = _dot(ah, w)
        uh = jnp.where(seq_start, 0.0, uh)
        p1 = uh[BF16_SUBLANES - 1:BF16_SUBLANES, :]
        p2 = uh[BF16_SUBLANES - 2:BF16_SUBLANES - 1, :]
        r1 = pltpu.roll(u, 1, 0)
        r2 = pltpu.roll(u, 2, 0)
        h1 = jnp.where(row8 == 0, p1, r1[:8, :])
        h2 = jnp.where(row8 == 0, p2, jnp.where(row8 == 1, p1, r2[:8, :]))
        u1 = jnp.concatenate([h1, r1[8:, :]], axis=0)
        u2 = jnp.concatenate([h2, r2[8:, :]], axis=0)
        cw = cw_ref[...]
        return cb_ref[...] + (cw[0:1, :] * u2 + cw[1:2, :] * u1 + cw[2:3, :] * u)

    gate = conv(wg_ref, cwg_ref, cbg_ref)
    val = conv(wv_ref, cwv_ref, cbv_ref)
    o_ref[...] = (_gelu_tanh(gate) * val).astype(o_ref.dtype)


def ffn_in(h, w_in, conv_w, conv_b, w_out, layer, seq, tm=2048, tn=256):
    t, k = h.shape
    f = w_in.shape[2] // 2
    d_out = w_out.shape[2]
    nf = f // tn
    n_steps = (t // tm) * nf
    assert f % (n_steps * BF16_SUBLANES) == 0
    slab = f // n_steps
    halo = BF16_SUBLANES
    assert seq % tm == 0 and tm % halo == 0
    hb = tm // halo
    conv_b = conv_b.reshape(conv_b.shape[0], 1, 2 * f)
    return pl.pallas_call(
        functools.partial(_ffn_in_kernel, tm=tm, seq=seq),
        grid=(t // tm, nf),
        in_specs=[
            pl.BlockSpec((tm, k), lambda i, j: (i, 0), pipeline_mode=pl.Buffered(1)),
            pl.BlockSpec((halo, k), lambda i, j: (jnp.maximum(i * hb - 1, 0), 0)),
            pl.BlockSpec((None, k, tn), lambda i, j: (layer, 0, j)),
            pl.BlockSpec((None, k, tn), lambda i, j: (layer, 0, j + nf)),
            pl.BlockSpec((None, CONV_W, tn), lambda i, j: (layer, 0, j)),
            pl.BlockSpec((None, CONV_W, tn), lambda i, j: (layer, 0, j + nf)),
            pl.BlockSpec((None, 1, tn), lambda i, j: (layer, 0, j)),
            pl.BlockSpec((None, 1, tn), lambda i, j: (layer, 0, j + nf)),
            pl.BlockSpec((None, slab, d_out), lambda i, j: (layer, i * nf + j, 0)),
        ],
        out_specs=[pl.BlockSpec((tm, tn), lambda i, j: (i, j)),
                   pl.BlockSpec((slab, d_out), lambda i, j: (i * nf + j, 0))],
        out_shape=[jax.ShapeDtypeStruct((t, f), BF16),
                   jax.ShapeDtypeStruct((f, d_out), BF16)],
        compiler_params=_params("parallel", "arbitrary"),
        name="ffn_in",
    )(h, h, w_in, w_in, conv_w, conv_w, conv_b, conv_b, w_out)


def _transpose_value_tiles(v_ref, vt_scr, tq):
    def transpose_tile(kb, c):
        start = pl.multiple_of(kb * tq, tq)
        vt_scr[kb] = v_ref[pl.ds(start, tq), :].astype(F32).T.astype(vt_scr.dtype)
        return c
    lax.fori_loop(0, vt_scr.shape[0], transpose_tile, 0)


def _sb_kernel(q_ref, k_ref, v_ref, o_ref, vt_scr, *, tq, scale, g):
    qi = pl.program_id(2)
    d = HEAD_DIM
    heads = [slice(hh * d, (hh + 1) * d) for hh in range(g)]
    qs = [q_ref[:, cs] for cs in heads]

    @pl.when(qi == 0)
    def _():
        _transpose_value_tiles(v_ref, vt_scr, tq)

    key = lax.broadcasted_iota(jnp.int32, (tq, tq), 0)
    qry = lax.broadcasted_iota(jnp.int32, (tq, tq), 1)
    strict = key < qry
    r2 = lax.broadcasted_iota(jnp.int32, (tq, 2 * tq), 0)
    c2 = lax.broadcasted_iota(jnp.int32, (tq, 2 * tq), 1)
    tri = (jnp.where(c2 >= tq, c2 - tq, c2) >= r2).astype(BF16)

    def block(kb, carry, diag):
        start = pl.multiple_of(kb * tq, tq)
        zs = [_dot_nt(k_ref[pl.ds(start, tq), cs], q) * (scale * LOG2E) for q, cs in zip(qs, heads)]
        csums = []
        for z in zs:
            neg_abs = lax.bitcast_convert_type(
                lax.bitcast_convert_type(z, jnp.uint32) | jnp.uint32(0x80000000), F32)
            sp = jnp.maximum(z, 0.0) + jnp.log2(1.0 + jnp.exp2(neg_abs))
            if diag:
                sp = jnp.where(strict, sp, 0.0)
            hi = sp.astype(BF16)
            lo = (sp - hi.astype(F32)).astype(BF16)
            csums.append(_dot(tri, jnp.concatenate([hi, lo], axis=0)))
        out = []
        for z, csum, cs, (rsum, acc) in zip(zs, csums, heads, carry):
            w = jnp.exp2(z - csum - rsum)
            if diag:
                w = jnp.where(strict, w, 0.0)
            acc = acc + _dot(vt_scr[kb, cs, :], w.astype(BF16))
            rsum = rsum + csum[0:1, :]
            out.append((rsum, acc))
        return tuple(out)

    init = tuple((jnp.zeros((1, tq), F32), jnp.zeros((d, tq), F32)) for _ in range(g))
    carry = block(qi, init, True)

    def body(it, carry):
        return block(qi - 1 - it, carry, False)

    carry = lax.fori_loop(0, qi, body, carry)
    for cs, (_, acc) in zip(heads, carry):
        o_ref[:, cs] = acc.T.astype(o_ref.dtype)


def sb_attention(qkv, batch, seq, n_heads, q_col, k_col, v_col, g=SB_HEADS_PER_STEP):
    tq = ATTN_TILE
    assert seq % tq == 0 and tq % CHUNK == 0
    nq = seq // tq
    d = HEAD_DIM
    assert n_heads % g == 0 and q_col % g == 0 and k_col % g == 0 and v_col % g == 0
    return pl.pallas_call(
        functools.partial(_sb_kernel, tq=tq, scale=1.0 / math.sqrt(d), g=g),
        grid=(batch, n_heads // g, nq),
        in_specs=[
            pl.BlockSpec((tq, g * d), lambda b, h, i: (b * nq + i, q_col // g + h)),
            pl.BlockSpec((seq, g * d), lambda b, h, i: (b, k_col // g + h)),
            pl.BlockSpec((seq, g * d), lambda b, h, i: (b, v_col // g + h)),
        ],
        out_specs=pl.BlockSpec((tq, g * d), lambda b, h, i: (b * nq + i, h)),
        out_shape=jax.ShapeDtypeStruct((batch * seq, n_heads * d), BF16),
        scratch_shapes=[pltpu.VMEM((nq, g * d, tq), BF16)],
        compiler_params=_params("parallel", "parallel", "arbitrary"),
        name="sb_attention",
    )(qkv, qkv, qkv)


def _online_step_t(s, vt, m, l, acc):
    m_new = jnp.maximum(m, jnp.max(s, axis=0, keepdims=True))
    alpha = jnp.exp2(m - m_new)
    p = jnp.exp2(s - m_new)
    l = alpha * l + jnp.sum(p, axis=0, keepdims=True)
    acc = alpha * acc + _dot(vt, p.astype(BF16))
    return m_new, l, acc


def _key_major_tile(tq):
    key = lax.broadcasted_iota(jnp.int32, (tq, tq), 0)
    qry = lax.broadcasted_iota(jnp.int32, (tq, tq), 1)
    return key, qry, (key // CHUNK) <= (qry // CHUNK)


def _diff_kernel(q_ref, k_ref, v_ref, lam_ref, g_ref, o_ref, vt_scr, *, tq, scale, lam_init, g):
    hg = pl.program_id(1)
    qi = pl.program_id(2)
    d = HEAD_DIM
    dv = 2 * HEAD_DIM
    qk_cols = [slice(c * d, (c + 1) * d) for c in range(2 * g)]
    v_cols = [slice(hh * dv, (hh + 1) * dv) for hh in range(g)]
    qs = [q_ref[:, cs] for cs in qk_cols]

    @pl.when(qi == 0)
    def _():
        _transpose_value_tiles(v_ref, vt_scr, tq)

    key, qry, visible = _key_major_tile(tq)
    dist = (qry - key).astype(F32)
    nbias_diag, nbias_off, slopes = [], [], []
    for hh in range(g):
        slope = lax.bitcast_convert_type(
            jnp.full((1, 1), (126 - (hg * g + hh)) * (1 << 23), jnp.int32), F32)
        slope = slope * LOG2E
        slopes.append(slope)
        nbias_diag.append(-slope * jnp.abs(dist))
        nbias_off.append(-slope * dist)

    def block(kb, carry, diag):
        start = pl.multiple_of(kb * tq, tq)
        ss = [_dot_nt(k_ref[pl.ds(start, tq), cs], q) * (scale * LOG2E) for q, cs in zip(qs, qk_cols)]
        for c in range(2 * g):
            hh = c // 2
            if diag:
                ss[c] = jnp.where(visible, ss[c] + nbias_diag[hh], NEG)
            else:
                ss[c] = ss[c] + (nbias_off[hh] - slopes[hh] * ((qi - kb) * tq).astype(F32))
        return tuple(_online_step_t(ss[c], vt_scr[kb, v_cols[c // 2], :], *carry[c])
                     for c in range(2 * g))

    def body(kb, carry):
        return block(kb, carry, False)

    init = tuple((jnp.full((1, tq), NEG, F32), jnp.zeros((1, tq), F32), jnp.zeros((dv, tq), F32))
                 for _ in range(2 * g))
    carry = lax.fori_loop(0, qi, body, init)
    carry = block(qi, carry, True)

    lf = lam_ref[...]
    lam = (jnp.exp(jnp.sum(lf[0:1, :] * lf[1:2, :], axis=-1, keepdims=True))
           - jnp.exp(jnp.sum(lf[2:3, :] * lf[3:4, :], axis=-1, keepdims=True)) + lam_init)
    for hh in range(g):
        (_, l1, a1), (_, l2, a2) = carry[2 * hh], carry[2 * hh + 1]
        o = (a1 / l1 - lam * (a2 / l2)).T
        o_ref[:, v_cols[hh]] = (_rms(o, g_ref[...]) * (1.0 - lam_init)).astype(o_ref.dtype)


def diff_attention(qkv, diff_lambda, subln_g, batch, seq, n_heads, q_col, k_col, v_col, lam_init,
                   g=DIFF_HEADS_PER_STEP):
    tq = ATTN_TILE
    assert seq % tq == 0 and tq % CHUNK == 0
    nq = seq // tq
    dv = 2 * HEAD_DIM
    assert n_heads == 8, "ALiBi slopes are built as exact powers of two"
    assert n_heads % g == 0 and q_col % g == 0 and k_col % g == 0 and v_col % g == 0
    return pl.pallas_call(
        functools.partial(_diff_kernel, tq=tq, scale=1.0 / math.sqrt(HEAD_DIM), lam_init=lam_init, g=g),
        grid=(batch, n_heads // g, nq),
        in_specs=[
            pl.BlockSpec((tq, g * dv), lambda b, h, i: (b * nq + i, q_col // g + h)),
            pl.BlockSpec((seq, g * dv), lambda b, h, i: (b, k_col // g + h)),
            pl.BlockSpec((seq, g * dv), lambda b, h, i: (b, v_col // g + h)),
            pl.BlockSpec((4, HEAD_DIM), lambda b, h, i: (0, 0)),
            pl.BlockSpec((1, dv), lambda b, h, i: (0, 0)),
        ],
        out_specs=pl.BlockSpec((tq, g * dv), lambda b, h, i: (b * nq + i, h)),
        out_shape=jax.ShapeDtypeStruct((batch * seq, n_heads * dv), BF16),
        scratch_shapes=[pltpu.VMEM((nq, g * dv, tq), BF16)],
        compiler_params=_params("parallel", "parallel", "arbitrary"),
        name="diff_attention",
    )(qkv, qkv, qkv, diff_lambda, subln_g.reshape(1, dv))


def _chunk_kernel(q_ref, k_ref, v_ref, tab_ref, o_ref, *, tq, scale, n_win, g):
    qi = pl.program_id(2)
    d = HEAD_DIM
    heads = [slice(hh * d, (hh + 1) * d) for hh in range(g)]
    kbs = [qi - (n_win - 1) + w for w in range(n_win)]
    starts = [pl.multiple_of(jnp.maximum(kb, 0) * tq, tq) for kb in kbs]
    scores = []
    for hh, cs in enumerate(heads):
        q = q_ref[:, cs]
        s_blocks = []
        for w in range(n_win):
            s = (_dot_nt(q, k_ref[pl.ds(starts[w], tq), cs]) * (scale * LOG2E)
                 + tab_ref[hh, :, w * tq:(w + 1) * tq])
            if w < n_win - 1:
                s = jnp.where(kbs[w] >= 0, s, NEG)
            s_blocks.append(s)
        scores.append(s_blocks)
    for cs, s_blocks in zip(heads, scores):
        m = s_blocks[0].max(axis=-1, keepdims=True)
        for s in s_blocks[1:]:
            m = jnp.maximum(m, s.max(axis=-1, keepdims=True))
        l = jnp.zeros_like(m)
        acc = jnp.zeros((tq, d), F32)
        for w, s in enumerate(s_blocks):
            p = jnp.exp2(s - m)
            l = l + jnp.sum(p, axis=-1, keepdims=True)
            acc = acc + _dot(p.astype(BF16), v_ref[pl.ds(starts[w], tq), cs])
        o_ref[:, cs] = (acc / l).astype(o_ref.dtype)


def chunk_bias_table(rel_bias, tq):
    left = LEFT_CHUNKS * CHUNK
    n_heads = rel_bias.shape[0]
    width = tq + left
    period = 2 * left
    assert left >= REL_CLIP and tq <= left
    rb = rel_bias.astype(F32)
    hi = jnp.broadcast_to(rb[:, -1:], (n_heads, left - REL_CLIP))
    mid = rb[:, ::-1]
    lo = jnp.broadcast_to(rb[:, :1], (n_heads, tq - REL_CLIP - 1))
    neg_d = jnp.broadcast_to(rb[:, -1:], (n_heads, period - width))
    diag = jnp.concatenate([hi, mid, lo, neg_d], axis=1)
    flat = jnp.tile(diag, (1, tq))[:, :tq * (period - 1)]
    bias = flat.reshape(n_heads, tq, period - 1)[:, :, :width]
    i = jnp.arange(tq)[:, None]
    j = jnp.arange(width)[None, :]
    qc, kc = i // CHUNK, j // CHUNK
    allowed = (kc >= qc) & (kc <= qc + LEFT_CHUNKS)
    return jnp.where(allowed[None], bias * LOG2E, NEG)


def chunk_attention(qkv, rel_bias, batch, seq, n_heads, q_col, k_col, v_col, g=CHUNK_HEADS_PER_STEP):
    tq = ATTN_TILE
    assert seq % tq == 0 and tq % CHUNK == 0
    nq = seq // tq
    d = HEAD_DIM
    left = LEFT_CHUNKS * CHUNK
    assert left % tq == 0
    assert n_heads % g == 0 and q_col % g == 0 and k_col % g == 0 and v_col % g == 0
    n_win = left // tq + 1
    table = chunk_bias_table(rel_bias, tq)
    return pl.pallas_call(
        functools.partial(_chunk_kernel, tq=tq, scale=1.0 / math.sqrt(d), n_win=n_win, g=g),
        grid=(n_heads // g, batch, nq),
        in_specs=[
            pl.BlockSpec((tq, g * d), lambda h, b, i: (b * nq + i, q_col // g + h)),
            pl.BlockSpec((seq, g * d), lambda h, b, i: (b, k_col // g + h)),
            pl.BlockSpec((seq, g * d), lambda h, b, i: (b, v_col // g + h)),
            pl.BlockSpec((g, tq, tq + left), lambda h, b, i: (h, 0, 0)),
        ],
        out_specs=pl.BlockSpec((tq, g * d), lambda h, b, i: (b * nq + i, h)),
        out_shape=jax.ShapeDtypeStruct((batch * seq, n_heads * d), BF16),
        compiler_params=_params("parallel", "parallel", "arbitrary"),
        name="chunk_attention",
    )(qkv, qkv, qkv, table)


def rope_tables(seq):
    half = QK_ROPE // 2
    pos = jnp.arange(seq, dtype=F32)
    inv_freq = ROPE_THETA ** (-jnp.arange(0, QK_ROPE, 2, dtype=F32) / QK_ROPE)
    ang = pos[:, None] * inv_freq[None, :]
    cos, sin = jnp.cos(ang), jnp.sin(ang)
    z = lambda n: jnp.zeros((seq, n), F32)
    cos_t = jnp.concatenate([cos, cos, z(LANES - 2 * half)], axis=1)
    sin_a = jnp.concatenate([-sin, z(LANES - half)], axis=1)
    sin_b = jnp.concatenate([z(half), sin, z(LANES - 2 * half)], axis=1)
    return cos_t, sin_a, sin_b


def _rope(x, cos_t, sin_a, sin_b):
    half = QK_ROPE // 2
    return x * cos_t + pltpu.roll(x, LANES - half, 1) * sin_a + pltpu.roll(x, half, 1) * sin_b


def _mla_q_kernel(c_ref, g_ref, w_ref, cos_ref, sa_ref, sb_ref, o_ref, an_ref, *, hp):
    @pl.when(pl.program_id(1) == 0)
    def _():
        an_ref[...] = _rms(c_ref[...], g_ref[...]).astype(an_ref.dtype)

    hw = QK_NOPE + LANES
    res = _dot(an_ref[...], w_ref[...])
    cos_t, sin_a, sin_b = cos_ref[...], sa_ref[...], sb_ref[...]
    for hh in range(hp):
        c0 = hh * hw
        o_ref[:, c0:c0 + QK_NOPE] = res[:, c0:c0 + QK_NOPE].astype(o_ref.dtype)
        o_ref[:, c0 + QK_NOPE:c0 + hw] = _rope(res[:, c0 + QK_NOPE:c0 + hw],
                                               cos_t, sin_a, sin_b).astype(o_ref.dtype)


def mla_q(lat, g, w_q, tables, seq, n_heads, tm=1024, hp=MLA_PROJ_HEADS_PER_STEP):
    t = lat.shape[0]
    kq = g.shape[0]
    hw = QK_NOPE + LANES
    assert seq % tm == 0
    npos = seq // tm
    tab_spec = pl.BlockSpec((tm, LANES), lambda i, j: (i % npos, 0))
    return pl.pallas_call(
        functools.partial(_mla_q_kernel, hp=hp),
        grid=(t // tm, n_heads // hp),
        in_specs=[
            pl.BlockSpec((tm, kq), lambda i, j: (i, 0)),
            pl.BlockSpec((1, kq), lambda i, j: (0, 0)),
            pl.BlockSpec((kq, hp * hw), lambda i, j: (0, j)),
            tab_spec, tab_spec, tab_spec,
        ],
        out_specs=pl.BlockSpec((tm, hp * hw), lambda i, j: (i, j)),
        out_shape=jax.ShapeDtypeStruct((t, n_heads * hw), BF16),
        scratch_shapes=[pltpu.VMEM((tm, kq), BF16)],
        compiler_params=_params("parallel", "arbitrary"),
        name="mla_q",
    )(lat, g.reshape(1, kq), w_q, *tables)


def _mla_kv_kernel(c_ref, kr_ref, g_ref, w_ref, cos_ref, sa_ref, sb_ref, k_ref, vt_ref,
                   an_ref, kr_scr, *, hp, tm, tk):
    @pl.when(pl.program_id(1) == 0)
    def _():
        an_ref[...] = _rms(c_ref[...], g_ref[...]).astype(an_ref.dtype)
        kr_scr[...] = _rope(kr_ref[...], cos_ref[...], sa_ref[...], sb_ref[...]).astype(kr_scr.dtype)

    hw = QK_NOPE + LANES
    wv = QK_NOPE + V_MLA
    res = _dot(an_ref[...], w_ref[...])
    kr = kr_scr[...]
    for hh in range(hp):
        k_ref[:, hh * hw:hh * hw + QK_NOPE] = res[:, hh * wv:hh * wv + QK_NOPE].astype(k_ref.dtype)
        k_ref[:, hh * hw + QK_NOPE:(hh + 1) * hw] = kr
        v = res[:, hh * wv + QK_NOPE:(hh + 1) * wv]
        for kt in range(tm // tk):
            vt_ref[kt, hh * V_MLA:(hh + 1) * V_MLA, :] = v[kt * tk:(kt + 1) * tk, :].T.astype(vt_ref.dtype)


def mla_kv(lat, g, w_kv, tables, seq, n_heads, ckv_col, kr_col, tm=1024, hp=MLA_PROJ_HEADS_PER_STEP):
    t = lat.shape[0]
    kkv = g.shape[0]
    hw = QK_NOPE + LANES
    tk = ATTN_TILE
    assert tm % tk == 0
    assert seq % tm == 0
    npos = seq // tm
    tab_spec = pl.BlockSpec((tm, LANES), lambda i, j: (i % npos, 0))
    return pl.pallas_call(
        functools.partial(_mla_kv_kernel, hp=hp, tm=tm, tk=tk),
        grid=(t // tm, n_heads // hp),
        in_specs=[
            pl.BlockSpec((tm, kkv), lambda i, j: (i, ckv_col)),
            pl.BlockSpec((tm, LANES), lambda i, j: (i, kr_col)),
            pl.BlockSpec((1, kkv), lambda i, j: (0, 0)),
            pl.BlockSpec((kkv, hp * (QK_NOPE + V_MLA)), lambda i, j: (0, j)),
            tab_spec, tab_spec, tab_spec,
        ],
        out_specs=[pl.BlockSpec((tm, hp * hw), lambda i, j: (i, j)),
                   pl.BlockSpec((tm // tk, hp * V_MLA, tk), lambda i, j: (i, j, 0))],
        out_shape=[jax.ShapeDtypeStruct((t, n_heads * hw), BF16),
                   jax.ShapeDtypeStruct((t // tk, n_heads * V_MLA, tk), BF16)],
        scratch_shapes=[pltpu.VMEM((tm, kkv), BF16), pltpu.VMEM((tm, LANES), BF16)],
        compiler_params=_params("parallel", "arbitrary"),
        name="mla_kv",
    )(lat, lat, g.reshape(1, kkv), w_kv, *tables)


def _mla_attn_kernel(q_ref, k_ref, vt_ref, o_ref, *, tq, scale, g):
    qi = pl.program_id(2)
    hw = QK_NOPE + LANES
    dv = V_MLA
    qk_cols = [slice(hh * hw, (hh + 1) * hw) for hh in range(g)]
    v_cols = [slice(hh * dv, (hh + 1) * dv) for hh in range(g)]
    qs = [q_ref[:, cs] for cs in qk_cols]
    _, _, visible = _key_major_tile(tq)

    def scores(kb, diag):
        start = pl.multiple_of(kb * tq, tq)
        ss = [_dot_nt(k_ref[pl.ds(start, tq), cs], q) * (scale * LOG2E) for q, cs in zip(qs, qk_cols)]
        if diag:
            ss = [jnp.where(visible, s, NEG) for s in ss]
        return ss

    def update(kb, ss, carry):
        return tuple(_online_step_t(s, vt_ref[kb, vs, :], *st)
                     for s, vs, st in zip(ss, v_cols, carry))

    def pair(kb, carry, diag_second):
        ss_a = scores(kb, False)
        ss_b = scores(kb + 1, diag_second)
        return update(kb + 1, ss_b, update(kb, ss_a, carry))

    init = tuple((jnp.full((1, tq), NEG, F32), jnp.zeros((1, tq), F32), jnp.zeros((dv, tq), F32))
                 for _ in range(g))
    carry = lax.fori_loop(0, qi // 2, lambda p, c: pair(2 * p, c, False), init)
    carry = lax.cond(qi % 2 == 1,
                     lambda c: pair(qi - 1, c, True),
                     lambda c: update(qi, scores(qi, True), c),
                     carry)
    for vs, (_, l, acc) in zip(v_cols, carry):
        o_ref[:, vs] = (acc / l).T.astype(o_ref.dtype)


def mla_attention(q, k, vt, batch, seq, n_heads, g=MLA_HEADS_PER_STEP):
    tq = ATTN_TILE
    assert seq % tq == 0 and tq % CHUNK == 0
    nq = seq // tq
    hw = QK_NOPE + LANES
    assert n_heads % g == 0
    return pl.pallas_call(
        functools.partial(_mla_attn_kernel, tq=tq, scale=1.0 / math.sqrt(QK_NOPE + QK_ROPE), g=g),
        grid=(batch, n_heads // g, nq),
        in_specs=[
            pl.BlockSpec((tq, g * hw), lambda b, h, i: (b * nq + i, h)),
            pl.BlockSpec((seq, g * hw), lambda b, h, i: (b, h), pipeline_mode=pl.Buffered(1)),
            pl.BlockSpec((nq, g * V_MLA, tq), lambda b, h, i: (b, h, 0), pipeline_mode=pl.Buffered(1)),
        ],
        out_specs=pl.BlockSpec((tq, g * V_MLA), lambda b, h, i: (b * nq + i, h)),
        out_shape=jax.ShapeDtypeStruct((batch * seq, n_heads * V_MLA), BF16),
        compiler_params=_params("parallel", "parallel", "arbitrary"),
        name="mla_attention",
    )(q, k, vt)


def _even_mixer(hn, w_in, w_out, i, diff_lambda, subln_g, batch, seq, layer):
    d_model = hn.shape[1]
    w_sb = d_model // 2
    n_sb = w_sb // HEAD_DIM
    n_diff = w_sb // (2 * HEAD_DIM)
    cb = w_sb // LANES
    qkv = matmul([(hn, w_in, i, 0)], w_in.shape[2], BF16, tm=2048, tn=512, name="even_in_proj",
                 a_single_buffer=True)
    a = sb_attention(qkv, batch, seq, n_sb, 0, cb, 2 * cb)
    lam_init = 0.8 - 0.6 * math.exp(-0.3 * layer)
    bo = diff_attention(qkv, diff_lambda, subln_g, batch, seq, n_diff,
                        3 * cb // 2, 4 * cb // 2, 5 * cb // 2, lam_init)
    return matmul([(a, w_out, i, 0), (bo, w_out, i, 1)], d_model, F32, tm=1024, tn=512,
                  name="mix_out_proj")


def _odd_mixer(hn, w_in, w_out, i, rel_bias, q_norm_g, w_uq, kv_norm_g, w_ukv, batch, seq):
    d_model = hn.shape[1]
    w_ch = d_model // 2
    n_ch = w_ch // HEAD_DIM
    n_mla = w_ch // HEAD_DIM
    cb = w_ch // LANES
    q_lora, kv_lora = q_norm_g.shape[0], kv_norm_g.shape[0]
    n_attn = 3 * w_ch
    n_lat = q_lora + kv_lora + QK_ROPE
    lat_pad = -n_lat % LANES
    w_in_t = jnp.swapaxes(w_in, 1, 2)
    w_lat_t = row_slab_bf16(w_in_t, i, n_attn, n_lat + lat_pad)
    qkv = matmul([(hn, w_in_t, i, 0)], n_attn, BF16, tm=2048, tn=512, name="odd_in_proj", w_t=True,
                 a_single_buffer=True)
    lat = matmul([(hn, w_lat_t, 0, 0)], n_lat + lat_pad, F32, tm=512, tn=n_lat + lat_pad,
                 name="odd_lat_proj", w_t=True)
    c = chunk_attention(qkv, rel_bias, batch, seq, n_ch, 0, cb, 2 * cb)

    tables = rope_tables(seq)
    hw = QK_NOPE + LANES
    wq = w_uq.reshape(q_lora, n_mla, QK_NOPE + QK_ROPE)
    wq = jnp.pad(wq, ((0, 0), (0, 0), (0, hw - QK_NOPE - QK_ROPE))).reshape(q_lora, n_mla * hw).astype(BF16)
    q = mla_q(lat, q_norm_g, wq, tables, seq, n_mla)
    assert q_lora % kv_lora == 0 and (q_lora + kv_lora) % LANES == 0
    k, v = mla_kv(lat, kv_norm_g, w_ukv.astype(BF16), tables, seq, n_mla,
                  q_lora // kv_lora, (q_lora + kv_lora) // LANES)
    dm = mla_attention(q, k, v, batch, seq, n_mla)
    return matmul([(c, w_out, i, 0), (dm, w_out, i, 1)], d_model, F32, tm=1024, tn=512,
                  name="mix_out_proj")


def _ffn(h, w_in, conv_w, conv_b, w_out, layer, seq):
    g, w_out16 = ffn_in(h, w_in, conv_w, conv_b, w_out, layer, seq)
    return matmul([(g, w_out16[None], 0, 0)], w_out.shape[2], F32, tm=512, tn=256,
                  name="ffn_out_proj")


def kernel(x, norm_g, even_w_in, even_w_out, diff_lambda, diff_subln_g, odd_w_in, odd_w_out,
           ch_rel_bias, mla_q_norm_g, mla_w_uq, mla_kv_norm_g, mla_w_ukv, ffn_w_in, ffn_conv_w,
           ffn_conv_b, ffn_w_out):
    batch, seq, d_model = x.shape
    depth = norm_g.shape[0]
    xf = x.reshape(batch * seq, d_model)
    hn = norm_cast(xf, norm_g[0, 0])
    for layer in range(depth):
        g = norm_g[layer]
        i = layer // 2
        if layer % 2 == 0:
            mix = _even_mixer(hn, even_w_in, even_w_out, i, diff_lambda[i], diff_subln_g[i],
                              batch, seq, layer)
        else:
            mix = _odd_mixer(hn, odd_w_in, odd_w_out, i, ch_rel_bias[i], mla_q_norm_g[i],
                             mla_w_uq[i], mla_kv_norm_g[i], mla_w_ukv[i], batch, seq)
        xf, h2 = resid_norm(xf, mix, g[1], g[2])
        f = _ffn(h2, ffn_w_in, ffn_conv_w, ffn_conv_b, ffn_w_out, layer, seq)
        if layer + 1 < depth:
            xf, hn = resid_norm(xf, f, g[3], norm_g[layer + 1, 0])
        else:
            xf = resid(xf, f, g[3])
    return xf.reshape(batch, seq, d_model)
```

```python
import functools
import math

import jax
import jax.numpy as jnp
from jax import lax
from jax.experimental import pallas as pl
from jax.experimental.pallas import tpu as pltpu

F32 = jnp.float32
BF16 = jnp.bfloat16

CHUNK = 64
HEAD_DIM = 128
LEFT_CHUNKS = 8
REL_CLIP = 128
QK_NOPE = 128
QK_ROPE = 64
V_MLA = 128
ROPE_THETA = 10000.0
CONV_W = 3
EPS = 1e-6
NEG = -1e30
LOG2E = math.log2(math.e)

LANES = 128
BF16_SUBLANES = 16
VMEM_LIMIT = 52 * 1024 * 1024

ATTN_TILE = 256
SB_HEADS_PER_STEP = 8
MLA_HEADS_PER_STEP = 8
DIFF_HEADS_PER_STEP = 4
CHUNK_HEADS_PER_STEP = 4
MLA_PROJ_HEADS_PER_STEP = 4


def _params(*sem):
    return pltpu.CompilerParams(dimension_semantics=sem, vmem_limit_bytes=VMEM_LIMIT)


def _rms(x, g):
    ms = jnp.mean(x * x, axis=-1, keepdims=True)
    return x * lax.rsqrt(ms + EPS) * g


def _dot(a, b):
    return jnp.dot(a, b, preferred_element_type=F32)


def _dot_nt(a, b):
    return lax.dot_general(a, b, (((1,), (1,)), ((), ())), preferred_element_type=F32)


def _norm_cast_kernel(x_ref, g_ref, h_ref):
    h_ref[...] = _rms(x_ref[...], g_ref[...]).astype(h_ref.dtype)


def _resid_norm_kernel(x_ref, y_ref, g1_ref, g2_ref, xo_ref, h_ref):
    xn = x_ref[...] + _rms(y_ref[...], g1_ref[...])
    xo_ref[...] = xn
    h_ref[...] = _rms(xn, g2_ref[...]).astype(h_ref.dtype)


def _resid_kernel(x_ref, y_ref, g_ref, xo_ref):
    xo_ref[...] = x_ref[...] + _rms(y_ref[...], g_ref[...])


def _row_spec(tr, d):
    return pl.BlockSpec((tr, d), lambda i: (i, 0))


def _vec_spec(d):
    return pl.BlockSpec((1, d), lambda i: (0, 0))


def norm_cast(x, g, tr=256):
    t, d = x.shape
    assert t % tr == 0
    return pl.pallas_call(
        _norm_cast_kernel,
        grid=(t // tr,),
        in_specs=[_row_spec(tr, d), _vec_spec(d)],
        out_specs=_row_spec(tr, d),
        out_shape=jax.ShapeDtypeStruct((t, d), BF16),
        compiler_params=_params("parallel"),
        name="norm_cast",
    )(x, g.reshape(1, d))


def resid_norm(x, y, g1, g2, tr=256):
    t, d = x.shape
    assert t % tr == 0
    return pl.pallas_call(
        _resid_norm_kernel,
        grid=(t // tr,),
        in_specs=[_row_spec(tr, d), _row_spec(tr, d), _vec_spec(d), _vec_spec(d)],
        out_specs=[_row_spec(tr, d), _row_spec(tr, d)],
        out_shape=[jax.ShapeDtypeStruct((t, d), F32), jax.ShapeDtypeStruct((t, d), BF16)],
        compiler_params=_params("parallel"),
        name="resid_norm",
    )(x, y, g1.reshape(1, d), g2.reshape(1, d))


def resid(x, y, g, tr=256):
    t, d = x.shape
    assert t % tr == 0
    return pl.pallas_call(
        _resid_kernel,
        grid=(t // tr,),
        in_specs=[_row_spec(tr, d), _row_spec(tr, d), _vec_spec(d)],
        out_specs=_row_spec(tr, d),
        out_shape=jax.ShapeDtypeStruct((t, d), F32),
        compiler_params=_params("parallel"),
        name="resid",
    )(x, y, g.reshape(1, d))


def _matmul_kernel(*refs, n_pairs, w_t):
    o_ref = refs[2 * n_pairs]
    dot = _dot_nt if w_t else _dot
    acc = dot(refs[0][...], refs[n_pairs][...].astype(BF16))
    for p in range(1, n_pairs):
        acc = acc + dot(refs[p][...], refs[n_pairs + p][...].astype(BF16))
    o_ref[...] = acc.astype(o_ref.dtype)


def matmul(pairs, n, out_dtype, tm, tn, name, w_t=False, a_single_buffer=False):
    m = pairs[0][0].shape[0]
    assert m % tm == 0 and n % tn == 0
    n_pairs = len(pairs)
    a_specs, w_specs, args_a, args_w = [], [], [], []
    for a, w, layer, rb in pairs:
        k = a.shape[1]
        mode = dict(pipeline_mode=pl.Buffered(1)) if a_single_buffer else {}
        a_specs.append(pl.BlockSpec((tm, k), lambda i, j: (i, 0), **mode))
        if w_t:
            w_specs.append(pl.BlockSpec((None, tn, k), lambda i, j, layer=layer, rb=rb: (layer, j, rb)))
        else:
            w_specs.append(pl.BlockSpec((None, k, tn), lambda i, j, layer=layer, rb=rb: (layer, rb, j)))
        args_a.append(a)
        args_w.append(w)
    return pl.pallas_call(
        functools.partial(_matmul_kernel, n_pairs=n_pairs, w_t=w_t),
        grid=(m // tm, n // tn),
        in_specs=a_specs + w_specs,
        out_specs=pl.BlockSpec((tm, tn), lambda i, j: (i, j)),
        out_shape=jax.ShapeDtypeStruct((m, n), out_dtype),
        compiler_params=_params("parallel", "arbitrary"),
        name=name,
    )(*args_a, *args_w)


def _row_slab_kernel(w_ref, o_ref, *, row0, n_rows):
    row = row0 + pl.program_id(0) * LANES + lax.broadcasted_iota(jnp.int32, w_ref.shape, 0)
    o_ref[...] = jnp.where(row < n_rows, w_ref[...], 0.0).astype(o_ref.dtype)


def row_slab_bf16(w_t, layer, row0, height):
    _, n_rows, k = w_t.shape
    assert row0 % LANES == 0 and height % LANES == 0
    return pl.pallas_call(
        functools.partial(_row_slab_kernel, row0=row0, n_rows=n_rows),
        grid=(height // LANES,),
        in_specs=[pl.BlockSpec((None, LANES, k), lambda j: (layer, row0 // LANES + j, 0))],
        out_specs=pl.BlockSpec((None, LANES, k), lambda j: (0, j, 0)),
        out_shape=jax.ShapeDtypeStruct((1, height, k), BF16),
        compiler_params=_params("parallel"),
        name="row_slab_bf16",
    )(w_t)


def _gelu_tanh(x):
    c = math.sqrt(2.0 / math.pi)
    return x * (0.5 * (1.0 + jnp.tanh(c * (x + 0.044715 * (x * x * x)))))


def _ffn_in_kernel(a_ref, ah_ref, wg_ref, wv_ref, cwg_ref, cwv_ref, cbg_ref, cbv_ref, wo_ref,
                   o_ref, wo16_ref, *, tm, seq):
    i = pl.program_id(0)
    wo16_ref[...] = wo_ref[...].astype(wo16_ref.dtype)
    tn = o_ref.shape[1]
    a = a_ref[...]
    ah = ah_ref[...]
    seq_start = (i * tm) % seq == 0
    row8 = lax.broadcasted_iota(jnp.int32, (8, tn), 0)

    def conv(w_ref, cw_ref, cb_ref):
        w = w_ref[...].astype(BF16)
        u = _dot(a, w)
        uh = _dot(ah, w)
        uh = jnp.where(seq_start, 0.0, uh)
        p1 = uh[BF16_SUBLANES - 1:BF16_SUBLANES, :]
        p2 = uh[BF16_SUBLANES - 2:BF16_SUBLANES - 1, :]
        r1 = pltpu.roll(u, 1, 0)
        r2 = pltpu.roll(u, 2, 0)
        h1 = jnp.where(row8 == 0, p1, r1[:8, :])
        h2 = jnp.where(row8 == 0, p2, jnp.where(row8 == 1, p1, r2[:8, :]))
        u1 = jnp.concatenate([h1, r1[8:, :]], axis=0)
        u2 = jnp.concatenate([h2, r2[8:, :]], axis=0)
        cw = cw_ref[...]
        return cb_ref[...] + (cw[0:1, :] * u2 + cw[1:2, :] * u1 + cw[2:3, :] * u)

    gate = conv(wg_ref, cwg_ref, cbg_ref)
    val = conv(wv_ref, cwv_ref, cbv_ref)
    o_ref[...] = (_gelu_tanh(gate) * val).astype(o_ref.dtype)


def ffn_in(h, w_in, conv_w, conv_b, w_out, layer, seq, tm=2048, tn=256):
    t, k = h.shape
    f = w_in.shape[2] // 2
    d_out = w_out.shape[2]
    nf = f // tn
    n_steps = (t // tm) * nf
    assert f % (n_steps * BF16_SUBLANES) == 0
    slab = f // n_steps
    halo = BF16_SUBLANES
    assert seq % tm == 0 and tm % halo == 0
    hb = tm // halo
    conv_b = conv_b.reshape(conv_b.shape[0], 1, 2 * f)
    return pl.pallas_call(
        functools.partial(_ffn_in_kernel, tm=tm, seq=seq),
        grid=(t // tm, nf),
        in_specs=[
            pl.BlockSpec((tm, k), lambda i, j: (i, 0), pipeline_mode=pl.Buffered(1)),
            pl.BlockSpec((halo, k), lambda i, j: (jnp.maximum(i * hb - 1, 0), 0)),
            pl.BlockSpec((None, k, tn), lambda i, j: (layer, 0, j)),
            pl.BlockSpec((None, k, tn), lambda i, j: (layer, 0, j + nf)),
            pl.BlockSpec((None, CONV_W, tn), lambda i, j: (layer, 0, j)),
            pl.BlockSpec((None, CONV_W, tn), lambda i, j: (layer, 0, j + nf)),
            pl.BlockSpec((None, 1, tn), lambda i, j: (layer, 0, j)),
            pl.BlockSpec((None, 1, tn), lambda i, j: (layer, 0, j + nf)),
            pl.BlockSpec((None, slab, d_out), lambda i, j: (layer, i * nf + j, 0)),
        ],
        out_specs=[pl.BlockSpec((tm, tn), lambda i, j: (i, j)),
                   pl.BlockSpec((slab, d_out), lambda i, j: (i * nf + j, 0))],
        out_shape=[jax.ShapeDtypeStruct((t, f), BF16),
                   jax.ShapeDtypeStruct((f, d_out), BF16)],
        compiler_params=_params("parallel", "arbitrary"),
        name="ffn_in",
    )(h, h, w_in, w_in, conv_w, conv_w, conv_b, conv_b, w_out)


def _transpose_value_tiles(v_ref, vt_scr, tq):
    def transpose_tile(kb, c):
        start = pl.multiple_of(kb * tq, tq)
        vt_scr[kb] = v_ref[pl.ds(start, tq), :].astype(F32).T.astype(vt_scr.dtype)
        return c
    lax.fori_loop(0, vt_scr.shape[0], transpose_tile, 0)


def _sb_kernel(q_ref, k_ref, v_ref, o_ref, vt_scr, *, tq, scale, g):
    qi = pl.program_id(2)
    d = HEAD_DIM
    heads = [slice(hh * d, (hh + 1) * d) for hh in range(g)]
    qs = [q_ref[:, cs] for cs in heads]

    @pl.when(qi == 0)
    def _():
        _transpose_value_tiles(v_ref, vt_scr, tq)

    key = lax.broadcasted_iota(jnp.int32, (tq, tq), 0)
    qry = lax.broadcasted_iota(jnp.int32, (tq, tq), 1)
    strict = key < qry
    r2 = lax.broadcasted_iota(jnp.int32, (tq, 2 * tq), 0)
    c2 = lax.broadcasted_iota(jnp.int32, (tq, 2 * tq), 1)
    tri = (jnp.where(c2 >= tq, c2 - tq, c2) >= r2).astype(BF16)

    def block(kb, carry, diag):
        start = pl.multiple_of(kb * tq, tq)
        zs = [_dot_nt(k_ref[pl.ds(start, tq), cs], q) * (scale * LOG2E) for q, cs in zip(qs, heads)]
        csums = []
        for z in zs:
            neg_abs = lax.bitcast_convert_type(
                lax.bitcast_convert_type(z, jnp.uint32) | jnp.uint32(0x80000000), F32)
            sp = jnp.maximum(z, 0.0) + jnp.log2(1.0 + jnp.exp2(neg_abs))
            if diag:
                sp = jnp.where(strict, sp, 0.0)
            hi = sp.astype(BF16)
            lo = (sp - hi.astype(F32)).astype(BF16)
            csums.append(_dot(tri, jnp.concatenate([hi, lo], axis=0)))
        out = []
        for z, csum, cs, (rsum, acc) in zip(zs, csums, heads, carry):
            w = jnp.exp2(z - csum - rsum)
            if diag:
                w = jnp.where(strict, w, 0.0)
            acc = acc + _dot(vt_scr[kb, cs, :], w.astype(BF16))
            rsum = rsum + csum[0:1, :]
            out.append((rsum, acc))
        return tuple(out)

    init = tuple((jnp.zeros((1, tq), F32), jnp.zeros((d, tq), F32)) for _ in range(g))
    carry = block(qi, init, True)

    def body(it, carry):
        return block(qi - 1 - it, carry, False)

    carry = lax.fori_loop(0, qi, body, carry)
    for cs, (_, acc) in zip(heads, carry):
        o_ref[:, cs] = acc.T.astype(o_ref.dtype)


def sb_attention(qkv, batch, seq, n_heads, q_col, k_col, v_col, g=SB_HEADS_PER_STEP):
    tq = ATTN_TILE
    assert seq % tq == 0 and tq % CHUNK == 0
    nq = seq // tq
    d = HEAD_DIM
    assert n_heads % g == 0 and q_col % g == 0 and k_col % g == 0 and v_col % g == 0
    return pl.pallas_call(
        functools.partial(_sb_kernel, tq=tq, scale=1.0 / math.sqrt(d), g=g),
        grid=(batch, n_heads // g, nq),
        in_specs=[
            pl.BlockSpec((tq, g * d), lambda b, h, i: (b * nq + i, q_col // g + h)),
            pl.BlockSpec((seq, g * d), lambda b, h, i: (b, k_col // g + h)),
            pl.BlockSpec((seq, g * d), lambda b, h, i: (b, v_col // g + h)),
        ],
        out_specs=pl.BlockSpec((tq, g * d), lambda b, h, i: (b * nq + i, h)),
        out_shape=jax.ShapeDtypeStruct((batch * seq, n_heads * d), BF16),
        scratch_shapes=[pltpu.VMEM((nq, g * d, tq), BF16)],
        compiler_params=_params("parallel", "parallel", "arbitrary"),
        name="sb_attention",
    )(qkv, qkv, qkv)


def _online_step_t(s, vt, m, l, acc):
    m_new = jnp.maximum(m, jnp.max(s, axis=0, keepdims=True))
    alpha = jnp.exp2(m - m_new)
    p = jnp.exp2(s - m_new)
    l = alpha * l + jnp.sum(p, axis=0, keepdims=True)
    acc = alpha * acc + _dot(vt, p.astype(BF16))
    return m_new, l, acc


def _key_major_tile(tq):
    key = lax.broadcasted_iota(jnp.int32, (tq, tq), 0)
    qry = lax.broadcasted_iota(jnp.int32, (tq, tq), 1)
    return key, qry, (key // CHUNK) <= (qry // CHUNK)


def _diff_kernel(q_ref, k_ref, v_ref, lam_ref, g_ref, o_ref, vt_scr, *, tq, scale, lam_init, g):
    hg = pl.program_id(1)
    qi = pl.program_id(2)
    d = HEAD_DIM
    dv = 2 * HEAD_DIM
    qk_cols = [slice(c * d, (c + 1) * d) for c in range(2 * g)]
    v_cols = [slice(hh * dv, (hh + 1) * dv) for hh in range(g)]
    qs = [q_ref[:, cs] for cs in qk_cols]

    @pl.when(qi == 0)
    def _():
        _transpose_value_tiles(v_ref, vt_scr, tq)

    key, qry, visible = _key_major_tile(tq)
    dist = (qry - key).astype(F32)
    nbias_diag, nbias_off, slopes = [], [], []
    for hh in range(g):
        slope = lax.bitcast_convert_type(
            jnp.full((1, 1), (126 - (hg * g + hh)) * (1 << 23), jnp.int32), F32)
        slope = slope * LOG2E
        slopes.append(slope)
        nbias_diag.append(-slope * jnp.abs(dist))
        nbias_off.append(-slope * dist)

    def block(kb, carry, diag):
        start = pl.multiple_of(kb * tq, tq)
        ss = [_dot_nt(k_ref[pl.ds(start, tq), cs], q) * (scale * LOG2E) for q, cs in zip(qs, qk_cols)]
        for c in range(2 * g):
            hh = c // 2
            if diag:
                ss[c] = jnp.where(visible, ss[c] + nbias_diag[hh], NEG)
            else:
                ss[c] = ss[c] + (nbias_off[hh] - slopes[hh] * ((qi - kb) * tq).astype(F32))
        return tuple(_online_step_t(ss[c], vt_scr[kb, v_cols[c // 2], :], *carry[c])
                     for c in range(2 * g))

    def body(kb, carry):
        return block(kb, carry, False)

    init = tuple((jnp.full((1, tq), NEG, F32), jnp.zeros((1, tq), F32), jnp.zeros((dv, tq), F32))
                 for _ in range(2 * g))
    carry = lax.fori_loop(0, qi, body, init)
    carry = block(qi, carry, True)

    lf = lam_ref[...]
    lam = (jnp.exp(jnp.sum(lf[0:1, :] * lf[1:2, :], axis=-1, keepdims=True))
           - jnp.exp(jnp.sum(lf[2:3, :] * lf[3:4, :], axis=-1, keepdims=True)) + lam_init)
    for hh in range(g):
        (_, l1, a1), (_, l2, a2) = carry[2 * hh], carry[2 * hh + 1]
        o = (a1 / l1 - lam * (a2 / l2)).T
        o_ref[:, v_cols[hh]] = (_rms(o, g_ref[...]) * (1.0 - lam_init)).astype(o_ref.dtype)


def diff_attention(qkv, diff_lambda, subln_g, batch, seq, n_heads, q_col, k_col, v_col, lam_init,
                   g=DIFF_HEADS_PER_STEP):
    tq = ATTN_TILE
    assert seq % tq == 0 and tq % CHUNK == 0
    nq = seq // tq
    dv = 2 * HEAD_DIM
    assert n_heads == 8, "ALiBi slopes are built as exact powers of two"
    assert n_heads % g == 0 and q_col % g == 0 and k_col % g == 0 and v_col % g == 0
    return pl.pallas_call(
        functools.partial(_diff_kernel, tq=tq, scale=1.0 / math.sqrt(HEAD_DIM), lam_init=lam_init, g=g),
        grid=(batch, n_heads // g, nq),
        in_specs=[
            pl.BlockSpec((tq, g * dv), lambda b, h, i: (b * nq + i, q_col // g + h)),
            pl.BlockSpec((seq, g * dv), lambda b, h, i: (b, k_col // g + h)),
            pl.BlockSpec((seq, g * dv), lambda b, h, i: (b, v_col // g + h)),
            pl.BlockSpec((4, HEAD_DIM), lambda b, h, i: (0, 0)),
            pl.BlockSpec((1, dv), lambda b, h, i: (0, 0)),
        ],
        out_specs=pl.BlockSpec((tq, g * dv), lambda b, h, i: (b * nq + i, h)),
        out_shape=jax.ShapeDtypeStruct((batch * seq, n_heads * dv), BF16),
        scratch_shapes=[pltpu.VMEM((nq, g * dv, tq), BF16)],
        compiler_params=_params("parallel", "parallel", "arbitrary"),
        name="diff_attention",
    )(qkv, qkv, qkv, diff_lambda, subln_g.reshape(1, dv))


def _chunk_kernel(q_ref, k_ref, v_ref, tab_ref, o_ref, *, tq, scale, n_win, g):
    qi = pl.program_id(2)
    d = HEAD_DIM
    heads = [slice(hh * d, (hh + 1) * d) for hh in range(g)]
    kbs = [qi - (n_win - 1) + w for w in range(n_win)]
    starts = [pl.multiple_of(jnp.maximum(kb, 0) * tq, tq) for kb in kbs]
    scores = []
    for hh, cs in enumerate(heads):
        q = q_ref[:, cs]
        s_blocks = []
        for w in range(n_win):
            s = (_dot_nt(q, k_ref[pl.ds(starts[w], tq), cs]) * (scale * LOG2E)
                 + tab_ref[hh, :, w * tq:(w + 1) * tq])
            if w < n_win - 1:
                s = jnp.where(kbs[w] >= 0, s, NEG)
            s_blocks.append(s)
        scores.append(s_blocks)
    for cs, s_blocks in zip(heads, scores):
        m = s_blocks[0].max(axis=-1, keepdims=True)
        for s in s_blocks[1:]:
            m = jnp.maximum(m, s.max(axis=-1, keepdims=True))
        l = jnp.zeros_like(m)
        acc = jnp.zeros((tq, d), F32)
        for w, s in enumerate(s_blocks):
            p = jnp.exp2(s - m)
            l = l + jnp.sum(p, axis=-1, keepdims=True)
            acc = acc + _dot(p.astype(BF16), v_ref[pl.ds(starts[w], tq), cs])
        o_ref[:, cs] = (acc / l).astype(o_ref.dtype)


def chunk_bias_table(rel_bias, tq):
    left = LEFT_CHUNKS * CHUNK
    n_heads = rel_bias.shape[0]
    width = tq + left
    period = 2 * left
    assert left >= REL_CLIP and tq <= left
    rb = rel_bias.astype(F32)
    hi = jnp.broadcast_to(rb[:, -1:], (n_heads, left - REL_CLIP))
    mid = rb[:, ::-1]
    lo = jnp.broadcast_to(rb[:, :1], (n_heads, tq - REL_CLIP - 1))
    neg_d = jnp.broadcast_to(rb[:, -1:], (n_heads, period - width))
    diag = jnp.concatenate([hi, mid, lo, neg_d], axis=1)
    flat = jnp.tile(diag, (1, tq))[:, :tq * (period - 1)]
    bias = flat.reshape(n_heads, tq, period - 1)[:, :, :width]
    i = jnp.arange(tq)[:, None]
    j = jnp.arange(width)[None, :]
    qc, kc = i // CHUNK, j // CHUNK
    allowed = (kc >= qc) & (kc <= qc + LEFT_CHUNKS)
    return jnp.where(allowed[None], bias * LOG2E, NEG)


def chunk_attention(qkv, rel_bias, batch, seq, n_heads, q_col, k_col, v_col, g=CHUNK_HEADS_PER_STEP):
    tq = ATTN_TILE
    assert seq % tq == 0 and tq % CHUNK == 0
    nq = seq // tq
    d = HEAD_DIM
    left = LEFT_CHUNKS * CHUNK
    assert left % tq == 0
    assert n_heads % g == 0 and q_col % g == 0 and k_col % g == 0 and v_col % g == 0
    n_win = left // tq + 1
    table = chunk_bias_table(rel_bias, tq)
    return pl.pallas_call(
        functools.partial(_chunk_kernel, tq=tq, scale=1.0 / math.sqrt(d), n_win=n_win, g=g),
        grid=(n_heads // g, batch, nq),
        in_specs=[
            pl.BlockSpec((tq, g * d), lambda h, b, i: (b * nq + i, q_col // g + h)),
            pl.BlockSpec((seq, g * d), lambda h, b, i: (b, k_col // g + h)),
            pl.BlockSpec((seq, g * d), lambda h, b, i: (b, v_col // g + h)),
            pl.BlockSpec((g, tq, tq + left), lambda h, b, i: (h, 0, 0)),
        ],
        out_specs=pl.BlockSpec((tq, g * d), lambda h, b, i: (b * nq + i, h)),
        out_shape=jax.ShapeDtypeStruct((batch * seq, n_heads * d), BF16),
        compiler_params=_params("parallel", "parallel", "arbitrary"),
        name="chunk_attention",
    )(qkv, qkv, qkv, table)


def rope_tables(seq):
    half = QK_ROPE // 2
    pos = jnp.arange(seq, dtype=F32)
    inv_freq = ROPE_THETA ** (-jnp.arange(0, QK_ROPE, 2, dtype=F32) / QK_ROPE)
    ang = pos[:, None] * inv_freq[None, :]
    cos, sin = jnp.cos(ang), jnp.sin(ang)
    z = lambda n: jnp.zeros((seq, n), F32)
    cos_t = jnp.concatenate([cos, cos, z(LANES - 2 * half)], axis=1)
    sin_a = jnp.concatenate([-sin, z(LANES - half)], axis=1)
    sin_b = jnp.concatenate([z(half), sin, z(LANES - 2 * half)], axis=1)
    return cos_t, sin_a, sin_b


def _rope(x, cos_t, sin_a, sin_b):
    half = QK_ROPE // 2
    return x * cos_t + pltpu.roll(x, LANES - half, 1) * sin_a + pltpu.roll(x, half, 1) * sin_b


def _mla_q_kernel(c_ref, g_ref, w_ref, cos_ref, sa_ref, sb_ref, o_ref, an_ref, *, hp):
    @pl.when(pl.program_id(1) == 0)
    def _():
        an_ref[...] = _rms(c_ref[...], g_ref[...]).astype(an_ref.dtype)

    hw = QK_NOPE + LANES
    res = _dot(an_ref[...], w_ref[...])
    cos_t, sin_a, sin_b = cos_ref[...], sa_ref[...], sb_ref[...]
    for hh in range(hp):
        c0 = hh * hw
        o_ref[:, c0:c0 + QK_NOPE] = res[:, c0:c0 + QK_NOPE].astype(o_ref.dtype)
        o_ref[:, c0 + QK_NOPE:c0 + hw] = _rope(res[:, c0 + QK_NOPE:c0 + hw],
                                               cos_t, sin_a, sin_b).astype(o_ref.dtype)


def mla_q(lat, g, w_q, tables, seq, n_heads, tm=1024, hp=MLA_PROJ_HEADS_PER_STEP):
    t = lat.shape[0]
    kq = g.shape[0]
    hw = QK_NOPE + LANES
    assert seq % tm == 0
    npos = seq // tm
    tab_spec = pl.BlockSpec((tm, LANES), lambda i, j: (i % npos, 0))
    return pl.pallas_call(
        functools.partial(_mla_q_kernel, hp=hp),
        grid=(t // tm, n_heads // hp),
        in_specs=[
            pl.BlockSpec((tm, kq), lambda i, j: (i, 0)),
            pl.BlockSpec((1, kq), lambda i, j: (0, 0)),
            pl.BlockSpec((kq, hp * hw), lambda i, j: (0, j)),
            tab_spec, tab_spec, tab_spec,
        ],
        out_specs=pl.BlockSpec((tm, hp * hw), lambda i, j: (i, j)),
        out_shape=jax.ShapeDtypeStruct((t, n_heads * hw), BF16),
        scratch_shapes=[pltpu.VMEM((tm, kq), BF16)],
        compiler_params=_params("parallel", "arbitrary"),
        name="mla_q",
    )(lat, g.reshape(1, kq), w_q, *tables)


def _mla_kv_kernel(c_ref, kr_ref, g_ref, w_ref, cos_ref, sa_ref, sb_ref, k_ref, vt_ref,
                   an_ref, kr_scr, *, hp, tm, tk):
    @pl.when(pl.program_id(1) == 0)
    def _():
        an_ref[...] = _rms(c_ref[...], g_ref[...]).astype(an_ref.dtype)
        kr_scr[...] = _rope(kr_ref[...], cos_ref[...], sa_ref[...], sb_ref[...]).astype(kr_scr.dtype)

    hw = QK_NOPE + LANES
    wv = QK_NOPE + V_MLA
    res = _dot(an_ref[...], w_ref[...])
    kr = kr_scr[...]
    for hh in range(hp):
        k_ref[:, hh * hw:hh * hw + QK_NOPE] = res[:, hh * wv:hh * wv + QK_NOPE].astype(k_ref.dtype)
        k_ref[:, hh * hw + QK_NOPE:(hh + 1) * hw] = kr
        v = res[:, hh * wv + QK_NOPE:(hh + 1) * wv]
        for kt in range(tm // tk):
            vt_ref[kt, hh * V_MLA:(hh + 1) * V_MLA, :] = v[kt * tk:(kt + 1) * tk, :].T.astype(vt_ref.dtype)


def mla_kv(lat, g, w_kv, tables, seq, n_heads, ckv_col, kr_col, tm=1024, hp=MLA_PROJ_HEADS_PER_STEP):
    t = lat.shape[0]
    kkv = g.shape[0]
    hw = QK_NOPE + LANES
    tk = ATTN_TILE
    assert tm % tk == 0
    assert seq % tm == 0
    npos = seq // tm
    tab_spec = pl.BlockSpec((tm, LANES), lambda i, j: (i % npos, 0))
    return pl.pallas_call(
        functools.partial(_mla_kv_kernel, hp=hp, tm=tm, tk=tk),
        grid=(t // tm, n_heads // hp),
        in_specs=[
            pl.BlockSpec((tm, kkv), lambda i, j: (i, ckv_col)),
            pl.BlockSpec((tm, LANES), lambda i, j: (i, kr_col)),
            pl.BlockSpec((1, kkv), lambda i, j: (0, 0)),
            pl.BlockSpec((kkv, hp * (QK_NOPE + V_MLA)), lambda i, j: (0, j)),
            tab_spec, tab_spec, tab_spec,
        ],
        out_specs=[pl.BlockSpec((tm, hp * hw), lambda i, j: (i, j)),
                   pl.BlockSpec((tm // tk, hp * V_MLA, tk), lambda i, j: (i, j, 0))],
        out_shape=[jax.ShapeDtypeStruct((t, n_heads * hw), BF16),
                   jax.ShapeDtypeStruct((t // tk, n_heads * V_MLA, tk), BF16)],
        scratch_shapes=[pltpu.VMEM((tm, kkv), BF16), pltpu.VMEM((tm, LANES), BF16)],
        compiler_params=_params("parallel", "arbitrary"),
        name="mla_kv",
    )(lat, lat, g.reshape(1, kkv), w_kv, *tables)


def _mla_attn_kernel(q_ref, k_ref, vt_ref, o_ref, *, tq, scale, g):
    qi = pl.program_id(2)
    hw = QK_NOPE + LANES
    dv = V_MLA
    qk_cols = [slice(hh * hw, (hh + 1) * hw) for hh in range(g)]
    v_cols = [slice(hh * dv, (hh + 1) * dv) for hh in range(g)]
    qs = [q_ref[:, cs] for cs in qk_cols]
    _, _, visible = _key_major_tile(tq)

    def scores(kb, diag):
        start = pl.multiple_of(kb * tq, tq)
        ss = [_dot_nt(k_ref[pl.ds(start, tq), cs], q) * (scale * LOG2E) for q, cs in zip(qs, qk_cols)]
        if diag:
            ss = [jnp.where(visible, s, NEG) for s in ss]
        return ss

    def update(kb, ss, carry):
        return tuple(_online_step_t(s, vt_ref[kb, vs, :], *st)
                     for s, vs, st in zip(ss, v_cols, carry))

    def pair(kb, carry, diag_second):
        ss_a = scores(kb, False)
        ss_b = scores(kb + 1, diag_second)
        return update(kb + 1, ss_b, update(kb, ss_a, carry))

    init = tuple((jnp.full((1, tq), NEG, F32), jnp.zeros((1, tq), F32), jnp.zeros((dv, tq), F32))
                 for _ in range(g))
    carry = lax.fori_loop(0, qi // 2, lambda p, c: pair(2 * p, c, False), init)
    carry = lax.cond(qi % 2 == 1,
                     lambda c: pair(qi - 1, c, True),
                     lambda c: update(qi, scores(qi, True), c),
                     carry)
    for vs, (_, l, acc) in zip(v_cols, carry):
        o_ref[:, vs] = (acc / l).T.astype(o_ref.dtype)


def mla_attention(q, k, vt, batch, seq, n_heads, g=MLA_HEADS_PER_STEP):
    tq = ATTN_TILE
    assert seq % tq == 0 and tq % CHUNK == 0
    nq = seq // tq
    hw = QK_NOPE + LANES
    assert n_heads % g == 0
    return pl.pallas_call(
        functools.partial(_mla_attn_kernel, tq=tq, scale=1.0 / math.sqrt(QK_NOPE + QK_ROPE), g=g),
        grid=(batch, n_heads // g, nq),
        in_specs=[
            pl.BlockSpec((tq, g * hw), lambda b, h, i: (b * nq + i, h)),
            pl.BlockSpec((seq, g * hw), lambda b, h, i: (b, h), pipeline_mode=pl.Buffered(1)),
            pl.BlockSpec((nq, g * V_MLA, tq), lambda b, h, i: (b, h, 0), pipeline_mode=pl.Buffered(1)),
        ],
        out_specs=pl.BlockSpec((tq, g * V_MLA), lambda b, h, i: (b * nq + i, h)),
        out_shape=jax.ShapeDtypeStruct((batch * seq, n_heads * V_MLA), BF16),
        compiler_params=_params("parallel", "parallel", "arbitrary"),
        name="mla_attention",
    )(q, k, vt)


def _even_mixer(hn, w_in, w_out, i, diff_lambda, subln_g, batch, seq, layer):
    d_model = hn.shape[1]
    w_sb = d_model // 2
    n_sb = w_sb // HEAD_DIM
    n_diff = w_sb // (2 * HEAD_DIM)
    cb = w_sb // LANES
    qkv = matmul([(hn, w_in, i, 0)], w_in.shape[2], BF16, tm=2048, tn=512, name="even_in_proj",
                 a_single_buffer=True)
    a = sb_attention(qkv, batch, seq, n_sb, 0, cb, 2 * cb)
    lam_init = 0.8 - 0.6 * math.exp(-0.3 * layer)
    bo = diff_attention(qkv, diff_lambda, subln_g, batch, seq, n_diff,
                        3 * cb // 2, 4 * cb // 2, 5 * cb // 2, lam_init)
    return matmul([(a, w_out, i, 0), (bo, w_out, i, 1)], d_model, F32, tm=1024, tn=512,
                  name="mix_out_proj")


def _odd_mixer(hn, w_in, w_out, i, rel_bias, q_norm_g, w_uq, kv_norm_g, w_ukv, batch, seq):
    d_model = hn.shape[1]
    w_ch = d_model // 2
    n_ch = w_ch // HEAD_DIM
    n_mla = w_ch // HEAD_DIM
    cb = w_ch // LANES
    q_lora, kv_lora = q_norm_g.shape[0], kv_norm_g.shape[0]
    n_attn = 3 * w_ch
    n_lat = q_lora + kv_lora + QK_ROPE
    lat_pad = -n_lat % LANES
    w_in_t = jnp.swapaxes(w_in, 1, 2)
    w_lat_t = row_slab_bf16(w_in_t, i, n_attn, n_lat + lat_pad)
    qkv = matmul([(hn, w_in_t, i, 0)], n_attn, BF16, tm=2048, tn=512, name="odd_in_proj", w_t=True,
                 a_single_buffer=True)
    lat = matmul([(hn, w_lat_t, 0, 0)], n_lat + lat_pad, F32, tm=512, tn=n_lat + lat_pad,
                 name="odd_lat_proj", w_t=True)
    c = chunk_attention(qkv, rel_bias, batch, seq, n_ch, 0, cb, 2 * cb)

    tables = rope_tables(seq)
    hw = QK_NOPE + LANES
    wq = w_uq.reshape(q_lora, n_mla, QK_NOPE + QK_ROPE)
    wq = jnp.pad(wq, ((0, 0), (0, 0), (0, hw - QK_NOPE - QK_ROPE))).reshape(q_lora, n_mla * hw).astype(BF16)
    q = mla_q(lat, q_norm_g, wq, tables, seq, n_mla)
    assert q_lora % kv_lora == 0 and (q_lora + kv_lora) % LANES == 0
    k, v = mla_kv(lat, kv_norm_g, w_ukv.astype(BF16), tables, seq, n_mla,
                  q_lora // kv_lora, (q_lora + kv_lora) // LANES)
    dm = mla_attention(q, k, v, batch, seq, n_mla)
    return matmul([(c, w_out, i, 0), (dm, w_out, i, 1)], d_model, F32, tm=1024, tn=512,
                  name="mix_out_proj")


def _ffn(h, w_in, conv_w, conv_b, w_out, layer, seq):
    g, w_out16 = ffn_in(h, w_in, conv_w, conv_b, w_out, layer, seq)
    return matmul([(g, w_out16[None], 0, 0)], w_out.shape[2], F32, tm=512, tn=512,
                  name="ffn_out_proj")


def kernel(x, norm_g, even_w_in, even_w_out, diff_lambda, diff_subln_g, odd_w_in, odd_w_out,
           ch_rel_bias, mla_q_norm_g, mla_w_uq, mla_kv_norm_g, mla_w_ukv, ffn_w_in, ffn_conv_w,
           ffn_conv_b, ffn_w_out):
    batch, seq, d_model = x.shape
    depth = norm_g.shape[0]
    xf = x.reshape(batch * seq, d_model)
    hn = norm_cast(xf, norm_g[0, 0])
    for layer in range(depth):
        g = norm_g[layer]
        i = layer // 2
        if layer % 2 == 0:
            mix = _even_mixer(hn, even_w_in, even_w_out, i, diff_lambda[i], diff_subln_g[i],
                              batch, seq, layer)
        else:
            mix = _odd_mixer(hn, odd_w_in, odd_w_out, i, ch_rel_bias[i], mla_q_norm_g[i],
                             mla_w_uq[i], mla_kv_norm_g[i], mla_w_ukv[i], batch, seq)
        xf, h2 = resid_norm(xf, mix, g[1], g[2])
        f = _ffn(h2, ffn_w_in, ffn_conv_w, ffn_conv_b, ffn_w_out, layer, seq)
        if layer + 1 < depth:
            xf, hn = resid_norm(xf, f, g[3], norm_g[layer + 1, 0])
        else:
            xf = resid(xf, f, g[3])
    return xf.reshape(batch, seq, d_model)
```
